```python
import jax
import jax.numpy as jnp
from jax import lax
import numpy as np

D_MODEL = 1024
BATCH = 32
SEQ = 256
DEPTH = 4
DEC_BATCH = 2
DEC_SEQ = 2048
PAST_LEN = 512

GRID_W = 64
N_MIXERS = 4
D_FF = 2816
EPS = 1e-6
N_MLSTM = (DEPTH + 3) // N_MIXERS
N_FOURIER = (DEPTH + 2) // N_MIXERS
N_GMLP = (DEPTH + 1) // N_MIXERS
N_NA = DEPTH // N_MIXERS

MLSTM_HEADS = 4
MLSTM_DK = D_MODEL // MLSTM_HEADS
MLSTM_DV = D_MODEL // MLSTM_HEADS
MLSTM_CHUNK = 128

FOURIER_GROUPS = 4

GMLP_WIDTH = D_MODEL
GMLP_GROUPS = 4
GMLP_CHUNK = 128

NA_HEADS = 16
NA_HD = D_MODEL // NA_HEADS
NA_KH = 8
NA_KW = 16
NA_QCB = 16
NA_KCB = 32
ATTN_QBLOCK = 128

kernel_name = 'hybrid_flow_backbone_step'


def rmsnorm(x, g):
    xf = x.astype(jnp.float32)
    y = xf * lax.rsqrt(jnp.mean(xf * xf, axis=-1, keepdims=True) + EPS)
    return y.astype(x.dtype) * g


def adaln(cond, w, b):
    m = jax.nn.silu(cond) @ w + b
    return m.reshape(m.shape[0], 1, 9, D_MODEL)


def sub_input(x, g, mods, idx):
    shift = mods[:, :, 3 * idx]
    scale = mods[:, :, 3 * idx + 1]
    gate = mods[:, :, 3 * idx + 2]
    return rmsnorm(x, g) * (1 + scale) + shift, gate


def swiglu(h, w1, w3, w2):
    return (jax.nn.silu(h @ w1) * (h @ w3)) @ w2


def mlstm_scan(q, k, v, i_pre, logf, C0, n0, m0):
    B, S = q.shape[0], q.shape[1]
    nc = S // MLSTM_CHUNK

    def to_chunks(a):
        return jnp.moveaxis(a.reshape(B, nc, MLSTM_CHUNK, *a.shape[2:]), 1, 0)

    xs = (to_chunks(q), to_chunks(k), to_chunks(v), to_chunks(i_pre), to_chunks(logf))
    causal = jnp.tril(jnp.ones((MLSTM_CHUNK, MLSTM_CHUNK), bool))[None, :, :, None]

    def step(carry, inp):
        C, n, m = carry
        qc, kc, vc, ic, fc = inp
        b = jnp.cumsum(fc, axis=1)
        dmat = b[:, :, None, :] - b[:, None, :, :] + ic[:, None, :, :]
        dmat = jnp.where(causal, dmat, -jnp.inf)
        inter = b + m[:, None, :]
        m_t = jnp.maximum(inter, jnp.max(dmat, axis=2))
        a = jnp.exp(dmat - m_t[:, :, None, :]) * jnp.einsum('bthk,bshk->btsh', qc, kc)
        w_inter = jnp.exp(inter - m_t)
        num = jnp.einsum('btsh,bshv->bthv', a, vc) + w_inter[..., None] * jnp.einsum('bhkv,bthk->bthv', C, qc)
        den = jnp.sum(a, axis=2) + w_inter * jnp.einsum('bhk,bthk->bth', n, qc)
        h = num / jnp.maximum(jnp.abs(den), jnp.exp(-m_t))[..., None]
        b_last = b[:, -1, :]
        g_s = b_last[:, None, :] - b + ic
        m_new = jnp.maximum(b_last + m, jnp.max(g_s, axis=1))
        w_s = jnp.exp(g_s - m_new[:, None, :])
        decay = jnp.exp(b_last + m - m_new)
        C_new = decay[..., None, None] * C + jnp.einsum('bsh,bshk,bshv->bhkv', w_s, kc, vc)
        n_new = decay[..., None] * n + jnp.einsum('bsh,bshk->bhk', w_s, kc)
        return (C_new, n_new, m_new), h

    init = (C0.astype(jnp.float32), n0.astype(jnp.float32), m0.astype(jnp.float32))
    (C, n, m), hs = lax.scan(step, init, xs)
    h = jnp.moveaxis(hs, 0, 1).reshape(B, S, MLSTM_HEADS, MLSTM_DV)
    return h, C, n, m


def mlstm_mixer(h, w_qkv, w_if, b_if, w_og, head_g, w_out, C0, n0, m0):
    B, S, _ = h.shape
    qkv = h @ w_qkv
    q = qkv[..., :D_MODEL].reshape(B, S, MLSTM_HEADS, MLSTM_DK).astype(jnp.float32)
    k = qkv[..., D_MODEL:2 * D_MODEL].reshape(B, S, MLSTM_HEADS, MLSTM_DK).astype(jnp.float32) * (MLSTM_DK ** -0.5)
    v = qkv[..., 2 * D_MODEL:].reshape(B, S, MLSTM_HEADS, MLSTM_DV).astype(jnp.float32)
    hs, Cs, ns, ms = [], [], [], []
    for d in range(2):
        g = (h @ w_if[d] + b_if[d]).astype(jnp.float32)
        seq = (q, k, v, g[..., :MLSTM_HEADS], jax.nn.log_sigmoid(g[..., MLSTM_HEADS:]))
        if d == 1:
            seq = tuple(jnp.flip(a, axis=1) for a in seq)
        hd, C, n, m = mlstm_scan(*seq, C0[:, d], n0[:, d], m0[:, d])
        if d == 1:
            hd = jnp.flip(hd, axis=1)
        hs.append(hd)
        Cs.append(C)
        ns.append(n)
        ms.append(m)
    hsum = hs[0] + hs[1]
    hn = hsum * lax.rsqrt(jnp.mean(hsum * hsum, axis=-1, keepdims=True) + EPS)
    hn = hn.reshape(B, S, D_MODEL).astype(h.dtype) * head_g
    y = (jax.nn.sigmoid(h @ w_og) * hn) @ w_out
    return y, (jnp.stack(Cs, axis=1), jnp.stack(ns, axis=1), jnp.stack(ms, axis=1))


def fourier_mixer(h, w_out, b_out):
    B, S, _ = h.shape
    hg = h.astype(jnp.float32).reshape(B, S, FOURIER_GROUPS, D_MODEL // FOURIER_GROUPS)
    f = jnp.fft.fft2(hg, axes=(1, 3), norm='ortho').real
    return f.reshape(B, S, D_MODEL).astype(h.dtype) @ w_out + b_out


def gmlp_mixer(h, w_in, b_in, v_g, w_s, b_s, w_out):
    B, S, _ = h.shape
    z = jax.nn.gelu(h @ w_in + b_in)
    u, v = z[..., :GMLP_WIDTH], z[..., GMLP_WIDTH:]
    v = rmsnorm(v, v_g)
    vg = v.reshape(B, S // GMLP_CHUNK, GMLP_CHUNK, GMLP_GROUPS, GMLP_WIDTH // GMLP_GROUPS)
    sv = jnp.einsum('gts,bnsgc->bntgc', w_s, vg) + b_s.T[None, None, :, :, None]
    return (u * sv.reshape(B, S, GMLP_WIDTH)) @ w_out


def na_qkv(h, w_qkv):
    B, S, _ = h.shape
    qkv = (h @ w_qkv).reshape(B, S, 3, NA_HEADS, NA_HD)
    return qkv[:, :, 0], qkv[:, :, 1], qkv[:, :, 2]


def context_attention(q, k, v):
    B, S, H, hd = q.shape
    nb = S // ATTN_QBLOCK
    qb = jnp.moveaxis(q.reshape(B, nb, ATTN_QBLOCK, H, hd), 1, 0)

    def block(qblk):
        s = jnp.einsum('bqhd,bkhd->bhqk', qblk, k).astype(jnp.float32) * (hd ** -0.5)
        p = jax.nn.softmax(s, axis=-1).astype(v.dtype)
        return jnp.einsum('bhqk,bkhd->bqhd', p, v)

    o = lax.map(block, qb)
    return jnp.moveaxis(o, 0, 1).reshape(B, S, H * hd)


def na_tables(rows):
    kh = min(NA_KH, rows)
    r = np.arange(rows)
    row_start = np.clip(r - NA_KH // 2, 0, rows - kh)
    key_rows = row_start[:, None] + np.arange(kh)[None, :]
    dr_idx = key_rows - r[:, None] + (NA_KH - 1)
    n_cb = GRID_W // NA_QCB
    qcols = np.arange(n_cb)[:, None] * NA_QCB + np.arange(NA_QCB)[None, :]
    col_start = np.clip(np.arange(n_cb) * NA_QCB - NA_KW // 2, 0, GRID_W - NA_KCB)
    key_cols = col_start[:, None] + np.arange(NA_KCB)[None, :]
    q_start = np.clip(qcols - NA_KW // 2, 0, GRID_W - NA_KW)
    kc = key_cols[:, None, :]
    mask = (kc >= q_start[:, :, None]) & (kc < q_start[:, :, None] + NA_KW)
    dc_idx = np.clip(kc - qcols[:, :, None] + (NA_KW - 1), 0, 2 * NA_KW - 2)
    return kh, key_rows, dr_idx, key_cols, mask, dc_idx


def na_latent(q, k, v, k_ctx, v_ctx, rpb):
    B, N, H, hd = q.shape
    rows = N // GRID_W
    kh, key_rows, dr_idx, key_cols, mask, dc_idx = na_tables(rows)
    n_cb = GRID_W // NA_QCB
    qg = q.reshape(B, rows, n_cb, NA_QCB, H, hd)
    kg = k.reshape(B, rows, GRID_W, H, hd)
    vg = v.reshape(B, rows, GRID_W, H, hd)
    ri = key_rows[:, None, :, None]
    ci = key_cols[None, :, None, :]
    kb = kg[:, ri, ci]
    vb = vg[:, ri, ci]
    scale = hd ** -0.5
    bias = jnp.moveaxis(rpb[:, dr_idx[:, None, None, :, None], dc_idx[None, :, :, None, :]], 0, 3)
    s_win = jnp.einsum('brcqhd,brcyxhd->brcqhyx', qg, kb).astype(jnp.float32) * scale + bias.astype(jnp.float32)
    s_win = jnp.where(mask[None, None, :, :, None, None, :], s_win, -jnp.inf)
    s_ctx = jnp.einsum('brcqhd,bkhd->brcqhk', qg, k_ctx).astype(jnp.float32) * scale
    n_win = kh * NA_KCB
    logits = jnp.concatenate([s_win.reshape(*s_win.shape[:5], n_win), s_ctx], axis=-1)
    p = jax.nn.softmax(logits, axis=-1).astype(v.dtype)
    p_win = p[..., :n_win].reshape(s_win.shape)
    out = jnp.einsum('brcqhyx,brcyxhd->brcqhd', p_win, vb) + jnp.einsum('brcqhk,bkhd->brcqhd', p[..., n_win:], v_ctx)
    return out.reshape(B, N, H * hd)


def setup_inputs(seed: int = 0) -> dict:
    key = jax.random.key(seed)
    ks = iter(list(jax.random.split(key, 40)))
    D = D_MODEL

    def nrm(shape, s=1.0):
        return jax.random.normal(next(ks), shape, jnp.float32) * s

    base_if = jnp.concatenate([jnp.zeros((MLSTM_HEADS,), jnp.float32), jnp.linspace(3.0, 6.0, MLSTM_HEADS)])
    return {
        'x_prompt': nrm((BATCH, SEQ, D)),
        'x_sample': nrm((DEC_BATCH, DEC_SEQ, D)),
        'state_mlstm_C': nrm((DEC_BATCH, N_MLSTM, 2, MLSTM_HEADS, MLSTM_DK, MLSTM_DV), 0.02),
        'state_mlstm_n': nrm((DEC_BATCH, N_MLSTM, 2, MLSTM_HEADS, MLSTM_DK), 0.1),
        'state_mlstm_m': nrm((DEC_BATCH, N_MLSTM, 2, MLSTM_HEADS), 0.5),
        'cache_na_k': nrm((DEC_BATCH, N_NA, PAST_LEN, NA_HEADS, NA_HD)),
        'cache_na_v': nrm((DEC_BATCH, N_NA, PAST_LEN, NA_HEADS, NA_HD)),
        'c': nrm((DEC_BATCH, D)),
        'c_ctx': nrm((D,)),
        'w_ada': nrm((DEPTH, D, 9 * D), 0.5 * D ** -0.5),
        'b_ada': nrm((DEPTH, 9 * D), 0.01),
        'norm_g': 1.0 + nrm((DEPTH, 3, D), 0.05),
        'final_g': 1.0 + nrm((D,), 0.05),
        'ffn_w1': nrm((DEPTH, 2, D, D_FF), D ** -0.5),
        'ffn_w3': nrm((DEPTH, 2, D, D_FF), D ** -0.5),
        'ffn_w2': nrm((DEPTH, 2, D_FF, D), D_FF ** -0.5),
        'ml_w_qkv': nrm((N_MLSTM, D, 3 * D), D ** -0.5),
        'ml_w_if': nrm((N_MLSTM, 2, D, 2 * MLSTM_HEADS), 0.1 * D ** -0.5),
        'ml_b_if': base_if + nrm((N_MLSTM, 2, 2 * MLSTM_HEADS), 0.1),
        'ml_w_og': nrm((N_MLSTM, D, D), D ** -0.5),
        'ml_head_g': 1.0 + nrm((N_MLSTM, D), 0.05),
        'ml_w_out': nrm((N_MLSTM, D, D), D ** -0.5),
        'fn_w_out': nrm((N_FOURIER, D, D), D ** -0.5),
        'fn_b_out': nrm((N_FOURIER, D), 0.01),
        'gm_w_in': nrm((N_GMLP, D, 2 * GMLP_WIDTH), D ** -0.5),
        'gm_b_in': nrm((N_GMLP, 2 * GMLP_WIDTH), 0.01),
        'gm_v_g': 1.0 + nrm((N_GMLP, GMLP_WIDTH), 0.05),
        'gm_w_s': nrm((N_GMLP, GMLP_GROUPS, GMLP_CHUNK, GMLP_CHUNK), GMLP_CHUNK ** -0.5),
        'gm_b_s': 1.0 + nrm((N_GMLP, GMLP_GROUPS, GMLP_CHUNK), 0.1),
        'gm_w_out': nrm((N_GMLP, GMLP_WIDTH, D), GMLP_WIDTH ** -0.5),
        'na_w_qkv': nrm((N_NA, D, 3 * D), D ** -0.5),
        'na_w_out': nrm((N_NA, D, D), D ** -0.5),
        'na_rpb': nrm((N_NA, NA_HEADS, 2 * NA_KH - 1, 2 * NA_KW - 1), 0.5),
    }


def reference(x_prompt, x_sample, state_mlstm_C, state_mlstm_n, state_mlstm_m, cache_na_k, cache_na_v, c, c_ctx,
              w_ada, b_ada, norm_g, final_g, ffn_w1, ffn_w3, ffn_w2,
              ml_w_qkv, ml_w_if, ml_b_if, ml_w_og, ml_head_g, ml_w_out,
              fn_w_out, fn_b_out, gm_w_in, gm_b_in, gm_v_g, gm_w_s, gm_b_s, gm_w_out,
              na_w_qkv, na_w_out, na_rpb):
    def run_layer(x, mods, l, mixer):
        h, gate = sub_input(x, norm_g[l, 0], mods, 0)
        x = x + 0.5 * gate * swiglu(h, ffn_w1[l, 0], ffn_w3[l, 0], ffn_w2[l, 0])
        h, gate = sub_input(x, norm_g[l, 1], mods, 1)
        y, aux = mixer(h)
        x = x + gate * y
        h, gate = sub_input(x, norm_g[l, 2], mods, 2)
        x = x + 0.5 * gate * swiglu(h, ffn_w1[l, 1], ffn_w3[l, 1], ffn_w2[l, 1])
        return x, aux

    xp, xs = x_prompt, x_sample
    B = xp.shape[0]
    new_C, new_n, new_m, new_k, new_v = [], [], [], [], []
    for l in range(DEPTH):
        kind, j = l % N_MIXERS, l // N_MIXERS
        mods_p = adaln(c_ctx[None, :], w_ada[l], b_ada[l])
        mods_s = adaln(c, w_ada[l], b_ada[l])
        if kind == 0:
            ml = (ml_w_qkv[j], ml_w_if[j], ml_b_if[j], ml_w_og[j], ml_head_g[j], ml_w_out[j])
            zC = jnp.zeros((B, 2, MLSTM_HEADS, MLSTM_DK, MLSTM_DV), jnp.float32)
            zn = jnp.zeros((B, 2, MLSTM_HEADS, MLSTM_DK), jnp.float32)
            zm = jnp.zeros((B, 2, MLSTM_HEADS), jnp.float32)
            xp, st = run_layer(xp, mods_p, l, lambda h: mlstm_mixer(h, *ml, zC, zn, zm))
            xs, _ = run_layer(xs, mods_s, l, lambda h: mlstm_mixer(
                h, *ml, state_mlstm_C[:, j], state_mlstm_n[:, j], state_mlstm_m[:, j]))
            new_C.append(st[0])
            new_n.append(st[1])
            new_m.append(st[2])
        elif kind == 1:
            fmix = lambda h: (fourier_mixer(h, fn_w_out[j], fn_b_out[j]), None)
            xp, _ = run_layer(xp, mods_p, l, fmix)
            xs, _ = run_layer(xs, mods_s, l, fmix)
        elif kind == 2:
            gmix = lambda h: (gmlp_mixer(h, gm_w_in[j], gm_b_in[j], gm_v_g[j], gm_w_s[j], gm_b_s[j], gm_w_out[j]), None)
            xp, _ = run_layer(xp, mods_p, l, gmix)
            xs, _ = run_layer(xs, mods_s, l, gmix)
        else:
            def na_ctx(h):
                q, k, v = na_qkv(h, na_w_qkv[j])
                return context_attention(q, k, v) @ na_w_out[j], (k, v)

            def na_lat(h):
                q, k, v = na_qkv(h, na_w_qkv[j])
                o = na_latent(q, k, v, cache_na_k[:, j], cache_na_v[:, j], na_rpb[j])
                return o @ na_w_out[j], None

            xp, kv = run_layer(xp, mods_p, l, na_ctx)
            xs, _ = run_layer(xs, mods_s, l, na_lat)
            new_k.append(kv[0])
            new_v.append(kv[1])
    return (rmsnorm(xp, final_g), rmsnorm(xs, final_g), jnp.stack(new_C, axis=1), jnp.stack(new_n, axis=1),
            jnp.stack(new_m, axis=1), jnp.stack(new_k, axis=1), jnp.stack(new_v, axis=1))
```

```python
import functools

import numpy as np
import jax
import jax.numpy as jnp
from jax import lax
from jax.experimental import pallas as pl
from jax.experimental.pallas import tpu as pltpu

D = 1024
BATCH = 32
SEQ = 256
DEPTH = 4
DEC_BATCH = 2
DEC_SEQ = 2048
PAST_LEN = 512
GRID_W = 64
D_FF = 2816
EPS = 1e-6

NP_TOK = BATCH * SEQ
NS_TOK = DEC_BATCH * DEC_SEQ
N_TOK = NP_TOK + NS_TOK

ML_HEADS = 4
ML_DK = D // ML_HEADS
ML_CHUNK = 128
N_STATE = 2 * ML_HEADS

FOURIER_GROUPS = 4
FG = D // FOURIER_GROUPS

GM_W = D
GM_GROUPS = 4
GM_CHUNK = 128

NA_HEADS = 16
NA_HD = D // NA_HEADS
NA_KH = 8
NA_KW = 16
NA_ROWS = DEC_SEQ // GRID_W
NA_WIN = NA_KH * GRID_W
NA_PAIRS = NA_HEADS // 2
N_COND = 8

LANE = 128
VMEM_LIMIT = 56 * 1024 * 1024

F32 = jnp.float32
BF16 = jnp.bfloat16
NEG_INF = float("-inf")


def _params(n_axes, last_arbitrary=False):
    sem = ["parallel"] * n_axes
    if last_arbitrary:
        sem[-1] = "arbitrary"
    return pltpu.CompilerParams(dimension_semantics=tuple(sem), vmem_limit_bytes=VMEM_LIMIT)


def _resident(shape):
    zeros = (0,) * len(shape)
    return pl.BlockSpec(shape, lambda *_: zeros, pipeline_mode=pl.Buffered(1))


def _tile_cond(i, tm):
    row0 = i * tm
    return jnp.where(row0 < NP_TOK, 0, 1 + (row0 - NP_TOK) // DEC_SEQ)


def _x_spec(tm, off):
    return pl.BlockSpec((tm, D), lambda i: (i + off, 0))


def _mods_spec(layer, tm, off):
    return pl.BlockSpec((None, 9, D), lambda i: (layer * N_COND + _tile_cond(i + off, tm), 0, 0))


def _dot(a, b):
    return jnp.dot(a, b, preferred_element_type=F32)


def _dot_nt(a, b):
    return lax.dot_general(a, b, (((1,), (1,)), ((), ())), preferred_element_type=F32)


def _rms(x):
    return x * lax.rsqrt(jnp.mean(x * x, axis=-1, keepdims=True) + EPS)


def _mod_norm(x, g_ref, mods_ref, idx):
    h = _rms(x) * g_ref[...]
    return h * (1.0 + mods_ref[3 * idx + 1:3 * idx + 2, :]) + mods_ref[3 * idx:3 * idx + 1, :]


def _gate(mods_ref, idx):
    return mods_ref[3 * idx + 2:3 * idx + 3, :]


def _adaln_kernel(c_ref, w_ref, b_ref, o_ref):
    c = c_ref[...]
    s = (c * jax.nn.sigmoid(c)).astype(BF16)
    o_ref[...] = _dot(s, w_ref[...].astype(BF16)) + b_ref[...]


def _adaln(cond, w_ada, b_ada):
    tn = 1152
    nj = 9 * D // tn
    return pl.pallas_call(
        _adaln_kernel,
        grid=(DEPTH, nj),
        in_specs=[
            pl.BlockSpec((N_COND, D), lambda l, j: (0, 0)),
            pl.BlockSpec((None, D, tn), lambda l, j: (l, 0, j)),
            pl.BlockSpec((None, 1, tn), lambda l, j: (l, 0, j)),
        ],
        out_specs=pl.BlockSpec((None, N_COND, tn), lambda l, j: (l, 0, j)),
        out_shape=jax.ShapeDtypeStruct((DEPTH, N_COND, 9 * D), F32),
        compiler_params=_params(2),
        name="adaln",
    )(cond, w_ada, b_ada.reshape(DEPTH, 1, 9 * D))


def _ffn_kernel(x_ref, mods_ref, g_ref, w1_ref, w3_ref, w2_ref, *rest, idx, final):
    o_ref = rest[-1]
    x = x_ref[...]
    hb = _mod_norm(x, g_ref, mods_ref, idx).astype(BF16)
    a = _dot(hb, w1_ref[...])
    b = _dot(hb, w3_ref[...])
    act = (a * jax.nn.sigmoid(a) * b).astype(BF16)
    y = x + (0.5 * _gate(mods_ref, idx)) * _dot(act, w2_ref[...])
    if final:
        y = _rms(y) * rest[0][...]
    o_ref[...] = y


def _ffn(x, mods, g, w1, w3, w2, layer, idx, off, n, final_g=None, tm=512):
    final = final_g is not None
    in_specs = [_x_spec(tm, off), _mods_spec(layer, tm, off), _resident((1, D)),
                _resident((D, D_FF)), _resident((D, D_FF)), _resident((D_FF, D))]
    args = [x, mods, g.reshape(1, D), w1, w3, w2]
    if final:
        in_specs.append(_resident((1, D)))
        args.append(final_g.reshape(1, D))
        out_spec = pl.BlockSpec((tm, D), lambda i: (i, 0))
        out_shape = jax.ShapeDtypeStruct((n * tm, D), F32)
        aliases = {}
    else:
        out_spec = _x_spec(tm, off)
        out_shape = jax.ShapeDtypeStruct(x.shape, F32)
        aliases = {0: 0}
    return pl.pallas_call(
        functools.partial(_ffn_kernel, idx=idx, final=final),
        grid=(n,), in_specs=in_specs, out_specs=out_spec, out_shape=out_shape,
        input_output_aliases=aliases, compiler_params=_params(1), name="ffn",
    )(*args)


def _proj_kernel(x_ref, a_ref, mods_ref, w_ref, o_ref):
    y = _dot(a_ref[...].astype(BF16), w_ref[...])
    o_ref[...] = x_ref[...] + _gate(mods_ref, 1) * y


def _proj_residual(x, a, mods, w, layer, off, n, tm=512):
    return pl.pallas_call(
        _proj_kernel,
        grid=(n,),
        in_specs=[_x_spec(tm, off), pl.BlockSpec((tm, D), lambda i: (i, 0)),
                  _mods_spec(layer, tm, off), _resident((D, D))],
        out_specs=_x_spec(tm, off),
        out_shape=jax.ShapeDtypeStruct(x.shape, F32),
        input_output_aliases={0: 0}, compiler_params=_params(1), name="proj_residual",
    )(x, a, mods, w)


def _ml_pre_kernel(x_ref, mods_ref, g_ref, wqkv_ref, wog_ref, wif_ref, bif_ref,
                   q_ref, k_ref, v_ref, og_ref, gates_ref):
    hb = _mod_norm(x_ref[...], g_ref, mods_ref, 1).astype(BF16)
    qkv = _dot(hb, wqkv_ref[...])
    q_ref[...] = qkv[:, :D].astype(BF16)
    k_ref[...] = (qkv[:, D:2 * D] * (ML_DK ** -0.5)).astype(BF16)
    v_ref[...] = qkv[:, 2 * D:].astype(BF16)
    og_ref[...] = jax.nn.sigmoid(_dot(hb, wog_ref[...]))
    gates_ref[...] = _dot(hb, wif_ref[...]) + bif_ref[...]


def _ml_pre(x, mods, g, wqkv, wog, wif, bif, layer, off, n, tm=512):
    tok = pl.BlockSpec((tm, D), lambda i: (i, 0))
    return pl.pallas_call(
        _ml_pre_kernel,
        grid=(n,),
        in_specs=[_x_spec(tm, off), _mods_spec(layer, tm, off), _resident((1, D)),
                  _resident((D, 3 * D)), _resident((D, D)), _resident((D, LANE)), _resident((1, LANE))],
        out_specs=[tok, tok, tok, tok, pl.BlockSpec((tm, LANE), lambda i: (i, 0))],
        out_shape=[jax.ShapeDtypeStruct((n * tm, D), BF16)] * 3
        + [jax.ShapeDtypeStruct((n * tm, D), F32), jax.ShapeDtypeStruct((n * tm, LANE), F32)],
        compiler_params=_params(1), name="mlstm_pre",
    )(x, mods, g.reshape(1, D), wqkv, wog, wif, bif)


def _log_sigmoid(x):
    return jnp.minimum(x, 0.0) - jnp.log1p(jnp.exp(-jnp.abs(x)))


def _cumsum_rows(x, reverse):
    n = x.shape[0]
    row = lax.broadcasted_iota(jnp.int32, x.shape, 0)
    sh = 1
    while sh < n:
        if reverse:
            x = x + jnp.where(row < n - sh, pltpu.roll(x, n - sh, 0), 0.0)
        else:
            x = x + jnp.where(row >= sh, pltpu.roll(x, sh, 0), 0.0)
        sh *= 2
    return x


def _ml_scan_kernel(*refs, with_init, emit_state):
    (qf_ref, kf_ref, vf_ref, gf_ref, qb_ref, kb_ref, vb_ref, gb_ref), refs = refs[:8], refs[8:]
    if with_init:
        (c0_ref, n0_ref, m0_ref), refs = refs[:3], refs[3:]
    (hf_ref, hb_ref), refs = refs[:2], refs[2:]
    if emit_state:
        (co_ref, no_ref, mo_ref), refs = refs[:3], refs[3:]
    c_sc, n_sc, m_sc = refs
    c = pl.program_id(1)
    L = ML_CHUNK

    @pl.when(c == 0)
    def _():
        if with_init:
            c_sc[...] = c0_ref[...]
            n_sc[...] = n0_ref[...]
            m_sc[...] = m0_ref[...]
        else:
            c_sc[...] = jnp.zeros_like(c_sc)
            n_sc[...] = jnp.zeros_like(n_sc)
            m_sc[...] = jnp.zeros_like(m_sc)

    t_idx = lax.broadcasted_iota(jnp.int32, (L, L), 0)
    s_idx = lax.broadcasted_iota(jnp.int32, (L, L), 1)
    for d, (q_ref, k_ref, v_ref, g_ref, h_ref) in enumerate(
            ((qf_ref, kf_ref, vf_ref, gf_ref, hf_ref), (qb_ref, kb_ref, vb_ref, gb_ref, hb_ref))):
        reverse = d == 1
        gts = g_ref[...]
        bsum = _cumsum_rows(_log_sigmoid(gts), reverse)
        gts_t = gts.T
        bsum_t = bsum.T
        causal = (s_idx >= t_idx) if reverse else (s_idx <= t_idx)
        for hd in range(ML_HEADS):
            r = d * ML_HEADS + hd
            ci, cf = d * 8 + hd, d * 8 + 4 + hd
            lo, hi = hd * ML_DK, (hd + 1) * ML_DK
            bcol, icol = bsum[:, cf:cf + 1], gts[:, ci:ci + 1]
            brow, irow = bsum_t[cf:cf + 1, :], gts_t[ci:ci + 1, :]
            m_old = m_sc[r:r + 1, 0:1]
            n_old = n_sc[r:r + 1, :]
            c_old = c_sc[r]
            qh, kh, vh = q_ref[:, lo:hi], k_ref[:, lo:hi], v_ref[:, lo:hi]

            dmat = jnp.where(causal, bcol - brow + irow, NEG_INF)
            inter = bcol + m_old
            m_t = jnp.maximum(inter, jnp.max(dmat, axis=1, keepdims=True))
            a = jnp.exp(dmat - m_t) * _dot_nt(qh, kh)
            w_inter = jnp.exp(inter - m_t)
            num = _dot(a.astype(BF16), vh) + w_inter * _dot(qh, c_old.astype(BF16))
            den = jnp.sum(a, axis=1, keepdims=True) + w_inter * jnp.sum(
                qh.astype(F32) * n_old, axis=1, keepdims=True)
            h_ref[:, lo:hi] = num / jnp.maximum(jnp.abs(den), jnp.exp(-m_t))

            b_last = bcol[0:1, :] if reverse else bcol[L - 1:L, :]
            g_col = b_last - bcol + icol
            m_new = jnp.maximum(b_last + m_old, jnp.max(g_col, axis=0, keepdims=True))
            decay = jnp.exp(b_last + m_old - m_new)
            kw = kh.astype(F32) * jnp.exp(g_col - m_new)
            c_sc[r] = decay * c_old + _dot(kw.T.astype(BF16), vh)
            n_sc[r:r + 1, :] = decay * n_old + jnp.sum(kw, axis=0, keepdims=True)
            m_sc[r:r + 1, :] = jnp.broadcast_to(m_new, (1, LANE))

    if emit_state:
        @pl.when(c == pl.num_programs(1) - 1)
        def _():
            co_ref[...] = c_sc[...]
            no_ref[...] = n_sc[...]
            mo_ref[...] = m_sc[...]


def _ml_scan(q, k, v, gates, nb, seq, init=None, emit_state=False):
    nc = seq // ML_CHUNK
    fwd = lambda b, c: (b * nc + c, 0)
    bwd = lambda b, c: (b * nc + (nc - 1 - c), 0)
    tok = lambda im: pl.BlockSpec((ML_CHUNK, D), im)
    gsp = lambda im: pl.BlockSpec((ML_CHUNK, LANE), im)
    st_c = pl.BlockSpec((None, N_STATE, ML_DK, ML_DK), lambda b, c: (b, 0, 0, 0))
    st_n = pl.BlockSpec((None, N_STATE, ML_DK), lambda b, c: (b, 0, 0))
    st_m = pl.BlockSpec((None, N_STATE, LANE), lambda b, c: (b, 0, 0))
    in_specs = [tok(fwd), tok(fwd), tok(fwd), gsp(fwd), tok(bwd), tok(bwd), tok(bwd), gsp(bwd)]
    args = [q, k, v, gates, q, k, v, gates]
    if init is not None:
        in_specs += [st_c, st_n, st_m]
        args += list(init)
    out_specs = [tok(fwd), tok(bwd)]
    out_shape = [jax.ShapeDtypeStruct((nb * seq, D), F32)] * 2
    if emit_state:
        out_specs += [st_c, st_n, st_m]
        out_shape += [jax.ShapeDtypeStruct((nb, N_STATE, ML_DK, ML_DK), F32),
                      jax.ShapeDtypeStruct((nb, N_STATE, ML_DK), F32),
                      jax.ShapeDtypeStruct((nb, N_STATE, LANE), F32)]
    return pl.pallas_call(
        functools.partial(_ml_scan_kernel, with_init=init is not None, emit_state=emit_state),
        grid=(nb, nc), in_specs=in_specs, out_specs=out_specs, out_shape=out_shape,
        scratch_shapes=[pltpu.VMEM((N_STATE, ML_DK, ML_DK), F32), pltpu.VMEM((N_STATE, ML_DK), F32),
                        pltpu.VMEM((N_STATE, LANE), F32)],
        compiler_params=_params(2, last_arbitrary=True), name="mlstm_scan",
    )(*args)


def _ml_post_kernel(x_ref, hf_ref, hb_ref, og_ref, mods_ref, hg_ref, w_ref, o_ref):
    hsum = hf_ref[...] + hb_ref[...]
    hn = jnp.concatenate(
        [_rms(hsum[:, hd * ML_DK:(hd + 1) * ML_DK]) for hd in range(ML_HEADS)], axis=1)
    y = _dot((og_ref[...] * (hn * hg_ref[...])).astype(BF16), w_ref[...])
    o_ref[...] = x_ref[...] + _gate(mods_ref, 1) * y


def _ml_post(x, hf, hb, og, mods, head_g, w_out, layer, off, n, tm=512):
    tok = pl.BlockSpec((tm, D), lambda i: (i, 0))
    return pl.pallas_call(
        _ml_post_kernel,
        grid=(n,),
        in_specs=[_x_spec(tm, off), tok, tok, tok, _mods_spec(layer, tm, off),
                  _resident((1, D)), _resident((D, D))],
        out_specs=_x_spec(tm, off),
        out_shape=jax.ShapeDtypeStruct(x.shape, F32),
        input_output_aliases={0: 0}, compiler_params=_params(1), name="mlstm_post",
    )(x, hf, hb, og, mods, head_g.reshape(1, D), w_out)


def _fn_chan_kernel(x_ref, mods_ref, g_ref, cc_ref, sc_ref, a_ref, b_ref):
    hb = _mod_norm(x_ref[...], g_ref, mods_ref, 1).astype(BF16)
    for gi in range(FOURIER_GROUPS):
        sl = slice(gi * FG, (gi + 1) * FG)
        a_ref[:, sl] = _dot(hb[:, sl], cc_ref[...]).astype(BF16)
        b_ref[:, sl] = _dot(hb[:, sl], sc_ref[...]).astype(BF16)


def _fn_chan(x, mods, g, cc, sc, layer, off, n, tm=512):
    tok = pl.BlockSpec((tm, D), lambda i: (i, 0))
    return pl.pallas_call(
        _fn_chan_kernel,
        grid=(n,),
        in_specs=[_x_spec(tm, off), _mods_spec(layer, tm, off), _resident((1, D)),
                  _resident((FG, FG)), _resident((FG, FG))],
        out_specs=[tok, tok],
        out_shape=[jax.ShapeDtypeStruct((n * tm, D), BF16)] * 2,
        compiler_params=_params(1), name="fourier_chan",
    )(x, mods, g.reshape(1, D), cc, sc)


def _fn_seq_kernel(x_ref, a_ref, b_ref, cs_ref, ss_ref, mods_ref, w_ref, bias_ref, o_ref, *, scale):
    f = (_dot(cs_ref[...], a_ref[...]) - _dot(ss_ref[...], b_ref[...])) * scale
    y = _dot(f.astype(BF16), w_ref[...]) + bias_ref[...]
    o_ref[...] = x_ref[...] + _gate(mods_ref, 1) * y


def _fn_seq(x, a, b, cs, ss, mods, w_out, b_out, layer, off_tok, nb, seq, tr):
    nt = seq // tr
    off = off_tok // tr
    xs = pl.BlockSpec((tr, D), lambda bi, t: (off + bi * nt + t, 0))
    ab = pl.BlockSpec((seq, D), lambda bi, t: (bi, 0))
    tbl = pl.BlockSpec((tr, seq), lambda bi, t: (t, 0))
    mods_spec = pl.BlockSpec(
        (None, 9, D), lambda bi, t: (layer * N_COND + _tile_cond(off + bi * nt + t, tr), 0, 0))
    return pl.pallas_call(
        functools.partial(_fn_seq_kernel, scale=float((seq * FG) ** -0.5)),
        grid=(nb, nt),
        in_specs=[xs, ab, ab, tbl, tbl, mods_spec, _resident((D, D)), _resident((1, D))],
        out_specs=xs,
        out_shape=jax.ShapeDtypeStruct(x.shape, F32),
        input_output_aliases={0: 0}, compiler_params=_params(2), name="fourier_seq",
    )(x, a, b, cs, ss, mods, w_out, b_out.reshape(1, D))


def _dft_tables(n):
    ang = 2.0 * np.pi * ((np.arange(n)[:, None] * np.arange(n)[None, :]) % n) / n
    return (jnp.asarray(np.cos(ang), F32).astype(BF16), jnp.asarray(np.sin(ang), F32).astype(BF16))


def _gm_kernel(x_ref, mods_ref, g_ref, win_ref, bin_ref, vg_ref, ws_ref, bs_ref, wout_ref, o_ref, sv_sc):
    x = x_ref[...]
    hb = _mod_norm(x, g_ref, mods_ref, 1).astype(BF16)
    z = _dot(hb, win_ref[...]) + bin_ref[...]
    z = z * (0.5 * (1.0 + jnp.tanh(np.sqrt(2.0 / np.pi) * (z + 0.044715 * (z * z * z)))))
    u = z[:, :GM_W]
    v = (_rms(z[:, GM_W:]) * vg_ref[...]).astype(BF16)
    gw = GM_W // GM_GROUPS
    for ch in range(x.shape[0] // GM_CHUNK):
        rows = slice(ch * GM_CHUNK, (ch + 1) * GM_CHUNK)
        for gi in range(GM_GROUPS):
            cols = slice(gi * gw, (gi + 1) * gw)
            sv_sc[rows, cols] = _dot(ws_ref[gi], v[rows, cols]) + bs_ref[:, gi:gi + 1]
    y = _dot((u * sv_sc[...]).astype(BF16), wout_ref[...])
    o_ref[...] = x + _gate(mods_ref, 1) * y


def _gmlp(x, mods, g, w_in, b_in, v_g, w_s, b_s_t, w_out, layer, tm=512):
    return pl.pallas_call(
        _gm_kernel,
        grid=(N_TOK // tm,),
        in_specs=[_x_spec(tm, 0), _mods_spec(layer, tm, 0), _resident((1, D)),
                  _resident((D, 2 * GM_W)), _resident((1, 2 * GM_W)), _resident((1, GM_W)),
                  _resident((GM_GROUPS, GM_CHUNK, GM_CHUNK)), _resident((GM_CHUNK, GM_GROUPS)),
                  _resident((GM_W, D))],
        out_specs=_x_spec(tm, 0),
        out_shape=jax.ShapeDtypeStruct(x.shape, F32),
        scratch_shapes=[pltpu.VMEM((tm, GM_W), F32)],
        input_output_aliases={0: 0}, compiler_params=_params(1), name="gmlp",
    )(x, mods, g.reshape(1, D), w_in, b_in.reshape(1, 2 * GM_W), v_g.reshape(1, GM_W), w_s, b_s_t, w_out)


def _na_pre_kernel(x_ref, mods_ref, g_ref, w_ref, q_ref, k_ref, v_ref):
    hb = _mod_norm(x_ref[...], g_ref, mods_ref, 1).astype(BF16)
    qkv = _dot(hb, w_ref[...])
    q_ref[...] = (qkv[:, :D] * (NA_HD ** -0.5)).astype(BF16)
    k_ref[...] = qkv[:, D:2 * D].astype(k_ref.dtype)
    v_ref[...] = qkv[:, 2 * D:].astype(v_ref.dtype)


def _na_pre(x, mods, g, w_qkv, layer, off, n, kv_dtype, tm=512):
    tok = pl.BlockSpec((tm, D), lambda i: (i, 0))
    return pl.pallas_call(
        _na_pre_kernel,
        grid=(n,),
        in_specs=[_x_spec(tm, off), _mods_spec(layer, tm, off), _resident((1, D)), _resident((D, 3 * D))],
        out_specs=[tok, tok, tok],
        out_shape=[jax.ShapeDtypeStruct((n * tm, D), BF16),
                   jax.ShapeDtypeStruct((n * tm, D), kv_dtype),
                   jax.ShapeDtypeStruct((n * tm, D), kv_dtype)],
        compiler_params=_params(1), name="na_qkv",
    )(x, mods, g.reshape(1, D), w_qkv)


def _half_mask(shape, e):
    lane = lax.broadcasted_iota(jnp.int32, shape, len(shape) - 1)
    return (lane < NA_HD) if e == 0 else (lane >= NA_HD)


def _ctx_attn_kernel(x_ref, q_ref, k_ref, v_ref, mods_ref, w_ref, o_ref, att_sc):
    for j in range(NA_PAIRS):
        sl = slice(j * LANE, (j + 1) * LANE)
        qp = q_ref[:, sl]
        kp = k_ref[:, sl].astype(BF16)
        vp = v_ref[:, sl].astype(BF16)
        outs = []
        for e in range(2):
            s = _dot_nt(jnp.where(_half_mask(qp.shape, e), qp, jnp.zeros_like(qp)), kp)
            p = jnp.exp(s - jnp.max(s, axis=1, keepdims=True))
            o = _dot(p.astype(BF16), vp)
            outs.append(o / jnp.sum(p, axis=1, keepdims=True))
        att_sc[:, sl] = jnp.where(_half_mask(outs[0].shape, 0), outs[0], outs[1]).astype(BF16)
    o_ref[...] = x_ref[...] + _gate(mods_ref, 1) * _dot(att_sc[...], w_ref[...])


def _ctx_attn(x, q, k, v, mods, w_out, layer):
    tok = pl.BlockSpec((SEQ, D), lambda b: (b, 0))
    return pl.pallas_call(
        _ctx_attn_kernel,
        grid=(BATCH,),
        in_specs=[tok, tok, tok, tok, _mods_spec(layer, SEQ, 0), _resident((D, D))],
        out_specs=tok,
        out_shape=jax.ShapeDtypeStruct(x.shape, F32),
        scratch_shapes=[pltpu.VMEM((SEQ, D), BF16)],
        input_output_aliases={0: 0}, compiler_params=_params(1), name="ctx_attn",
    )(x, q, k, v, mods, w_out)


def _na_bias_tables(rpb):
    q = np.arange(GRID_W)[:, None]
    x = np.arange(GRID_W)[None, :]
    q_start = np.clip(q - NA_KW // 2, 0, GRID_W - NA_KW)
    mask = (x >= q_start) & (x < q_start + NA_KW)
    dc = np.clip(x - q + (NA_KW - 1), 0, 2 * NA_KW - 2)
    dr = np.arange(NA_KH)[None, :] - np.arange(NA_KH)[:, None] + (NA_KH - 1)
    t = rpb[:, dr[:, :, None, None], dc[None, None, :, :]]
    t = jnp.where(mask[None, None, None], t, NEG_INF)
    return jnp.transpose(t, (1, 0, 3, 2, 4)).reshape(NA_KH, NA_HEADS, GRID_W, NA_WIN)


def _na_lat_kernel(q_ref, k_ref, v_ref, kc_ref, vc_ref, bias_ref, o_ref):
    r = pl.program_id(1)
    start = pl.multiple_of(jnp.clip(r - NA_KH // 2, 0, NA_ROWS - NA_KH) * GRID_W, GRID_W)
    for j in range(NA_PAIRS):
        sl = slice(j * LANE, (j + 1) * LANE)
        qp = q_ref[:, sl]
        kw = k_ref[pl.ds(start, NA_WIN), sl]
        vw = v_ref[pl.ds(start, NA_WIN), sl]
        kc = kc_ref[:, sl]
        vc = vc_ref[:, sl]
        outs = []
        for e in range(2):
            qm = jnp.where(_half_mask(qp.shape, e), qp, jnp.zeros_like(qp))
            s_w = _dot_nt(qm, kw) + bias_ref[2 * j + e]
            s_c = _dot_nt(qm, kc)
            m = jnp.maximum(jnp.max(s_w, axis=1, keepdims=True), jnp.max(s_c, axis=1, keepdims=True))
            p_w = jnp.exp(s_w - m)
            p_c = jnp.exp(s_c - m)
            o = _dot(p_w.astype(BF16), vw) + _dot(p_c.astype(BF16), vc)
            outs.append(o / (jnp.sum(p_w, axis=1, keepdims=True) + jnp.sum(p_c, axis=1, keepdims=True)))
        o_ref[:, sl] = jnp.where(_half_mask(outs[0].shape, 0), outs[0], outs[1])


def _na_latent(q, k, v, kc, vc, bias):
    row_class = lambda r: r - jnp.clip(r - NA_KH // 2, 0, NA_ROWS - NA_KH)
    seq_kv = pl.BlockSpec((None, DEC_SEQ, D), lambda b, r: (b, 0, 0))
    ctx_kv = pl.BlockSpec((None, PAST_LEN, D), lambda b, r: (b, 0, 0))
    return pl.pallas_call(
        _na_lat_kernel,
        grid=(DEC_BATCH, NA_ROWS),
        in_specs=[pl.BlockSpec((GRID_W, D), lambda b, r: (b * NA_ROWS + r, 0)),
                  seq_kv, seq_kv, ctx_kv, ctx_kv,
                  pl.BlockSpec((None, NA_HEADS, GRID_W, NA_WIN), lambda b, r: (row_class(r), 0, 0, 0))],
        out_specs=pl.BlockSpec((GRID_W, D), lambda b, r: (b * NA_ROWS + r, 0)),
        out_shape=jax.ShapeDtypeStruct((NS_TOK, D), F32),
        compiler_params=_params(2), name="na_latent",
    )(q, k.reshape(DEC_BATCH, DEC_SEQ, D), v.reshape(DEC_BATCH, DEC_SEQ, D), kc, vc, bias)


def kernel(x_prompt, x_sample, state_mlstm_C, state_mlstm_n, state_mlstm_m, cache_na_k, cache_na_v, c, c_ctx, w_ada, b_ada, norm_g, final_g, ffn_w1, ffn_w3, ffn_w2, ml_w_qkv, ml_w_if, ml_b_if, ml_w_og, ml_head_g, ml_w_out, fn_w_out, fn_b_out, gm_w_in, gm_b_in, gm_v_g, gm_w_s, gm_b_s, gm_w_out, na_w_qkv, na_w_out, na_rpb):
    tm = 512
    n_p, n_s, n_all = NP_TOK // tm, NS_TOK // tm, N_TOK // tm
    x = jnp.concatenate([x_prompt.reshape(NP_TOK, D), x_sample.reshape(NS_TOK, D)], axis=0)

    cond = jnp.zeros((N_COND, D), F32).at[0].set(c_ctx).at[1:1 + DEC_BATCH].set(c)
    mods = _adaln(cond, w_ada, b_ada).reshape(DEPTH * N_COND, 9, D)

    w1, w3, w2 = ffn_w1.astype(BF16), ffn_w3.astype(BF16), ffn_w2.astype(BF16)
    outs = {}
    for l in range(DEPTH):
        kind, j = l % 4, l // 4
        x = _ffn(x, mods, norm_g[l, 0], w1[l, 0], w3[l, 0], w2[l, 0], l, 0, 0, n_all)
        g = norm_g[l, 1]
        if kind == 0:
            wif = jnp.pad(jnp.transpose(ml_w_if[j], (1, 0, 2)).reshape(D, 16), ((0, 0), (0, LANE - 16)))
            bif = jnp.pad(ml_b_if[j].reshape(1, 16), ((0, 0), (0, LANE - 16)))
            wqkv, wog, wout = ml_w_qkv[j].astype(BF16), ml_w_og[j].astype(BF16), ml_w_out[j].astype(BF16)
            for off, n, nb, seq in ((0, n_p, BATCH, SEQ), (n_p, n_s, DEC_BATCH, DEC_SEQ)):
                q, k, v, og, gates = _ml_pre(x, mods, g, wqkv, wog, wif.astype(BF16), bif, l, off, n)
                if off == 0:
                    hf, hb, c_new, n_new, m_new = _ml_scan(q, k, v, gates, nb, seq, emit_state=True)
                    outs["C"] = c_new.reshape(BATCH, 1, 2, ML_HEADS, ML_DK, ML_DK)
                    outs["n"] = n_new.reshape(BATCH, 1, 2, ML_HEADS, ML_DK)
                    outs["m"] = m_new[:, :, 0].reshape(BATCH, 1, 2, ML_HEADS)
                else:
                    init = (state_mlstm_C[:, j].reshape(DEC_BATCH, N_STATE, ML_DK, ML_DK),
                            state_mlstm_n[:, j].reshape(DEC_BATCH, N_STATE, ML_DK),
                            jnp.broadcast_to(state_mlstm_m[:, j].reshape(DEC_BATCH, N_STATE, 1),
                                             (DEC_BATCH, N_STATE, LANE)))
                    hf, hb = _ml_scan(q, k, v, gates, nb, seq, init=init)
                x = _ml_post(x, hf, hb, og, mods, ml_head_g[j], wout, l, off, n)
        elif kind == 1:
            cc, sc = _dft_tables(FG)
            wout = fn_w_out[j].astype(BF16)
            for off, n, nb, seq, tr in ((0, n_p, BATCH, SEQ, SEQ), (n_p, n_s, DEC_BATCH, DEC_SEQ, 512)):
                a, b = _fn_chan(x, mods, g, cc, sc, l, off, n)
                cs, ss = _dft_tables(seq)
                x = _fn_seq(x, a, b, cs, ss, mods, wout, fn_b_out[j], l, off * tm, nb, seq, tr)
        elif kind == 2:
            x = _gmlp(x, mods, g, gm_w_in[j].astype(BF16), gm_b_in[j], gm_v_g[j],
                      gm_w_s[j].astype(BF16), gm_b_s[j].T, gm_w_out[j].astype(BF16), l)
        else:
            wqkv, wout = na_w_qkv[j].astype(BF16), na_w_out[j].astype(BF16)
            q, k, v = _na_pre(x, mods, g, wqkv, l, 0, n_p, F32)
            outs["k"] = k.reshape(BATCH, 1, SEQ, NA_HEADS, NA_HD)
            outs["v"] = v.reshape(BATCH, 1, SEQ, NA_HEADS, NA_HD)
            x = _ctx_attn(x, q, k, v, mods, wout, l)
            q, k, v = _na_pre(x, mods, g, wqkv, l, n_p, n_s, BF16)
            att = _na_latent(q, k, v,
                             cache_na_k[:, j].reshape(DEC_BATCH, PAST_LEN, D).astype(BF16),
                             cache_na_v[:, j].reshape(DEC_BATCH, PAST_LEN, D).astype(BF16),
                             _na_bias_tables(na_rpb[j]))
            x = _proj_residual(x, att, mods, wout, l, n_p, n_s)
        if l < DEPTH - 1:
            x = _ffn(x, mods, norm_g[l, 2], w1[l, 1], w3[l, 1], w2[l, 1], l, 2, 0, n_all)
        else:
            y_p = _ffn(x, mods, norm_g[l, 2], w1[l, 1], w3[l, 1], w2[l, 1], l, 2, 0, n_p, final_g=final_g)
            y_s = _ffn(x, mods, norm_g[l, 2], w1[l, 1], w3[l, 1], w2[l, 1], l, 2, n_p, n_s, final_g=final_g)
    return (y_p.reshape(BATCH, SEQ, D), y_s.reshape(DEC_BATCH, DEC_SEQ, D),
            outs["C"], outs["n"], outs["m"], outs["k"], outs["v"])
```

```python
import functools

import numpy as np
import jax
import jax.numpy as jnp
from jax import lax
from jax.experimental import pallas as pl
from jax.experimental.pallas import tpu as pltpu

D = 1024
BATCH = 32
SEQ = 256
DEPTH = 4
DEC_BATCH = 2
DEC_SEQ = 2048
PAST_LEN = 512
GRID_W = 64
D_FF = 2816
EPS = 1e-6

NP_TOK = BATCH * SEQ
NS_TOK = DEC_BATCH * DEC_SEQ
N_TOK = NP_TOK + NS_TOK

ML_HEADS = 4
ML_DK = D // ML_HEADS
ML_CHUNK = 128
N_STATE = 2 * ML_HEADS

FOURIER_GROUPS = 4
FG = D // FOURIER_GROUPS

GM_W = D
GM_GROUPS = 4
GM_CHUNK = 128

NA_HEADS = 16
NA_HD = D // NA_HEADS
NA_KH = 8
NA_KW = 16
NA_ROWS = DEC_SEQ // GRID_W
NA_WIN = NA_KH * GRID_W
NA_PAIRS = NA_HEADS // 2
N_COND = 8

LANE = 128
BF16_ROWS = 16
VMEM_LIMIT = 56 * 1024 * 1024

F32 = jnp.float32
BF16 = jnp.bfloat16
NEG_INF = float("-inf")


def _params(n_axes, last_arbitrary=False):
    sem = ["parallel"] * n_axes
    if last_arbitrary:
        sem[-1] = "arbitrary"
    return pltpu.CompilerParams(dimension_semantics=tuple(sem), vmem_limit_bytes=VMEM_LIMIT)


def _resident(shape):
    zeros = (0,) * len(shape)
    return pl.BlockSpec(shape, lambda *_: zeros, pipeline_mode=pl.Buffered(1))


def _tile_cond(i, tm):
    row0 = i * tm
    return jnp.where(row0 < NP_TOK, 0, 1 + (row0 - NP_TOK) // DEC_SEQ)


def _x_spec(tm, off):
    return pl.BlockSpec((tm, D), lambda i: (i + off, 0))


def _mods_spec(layer, tm, off):
    return pl.BlockSpec((None, 9, D), lambda i: (layer * N_COND + _tile_cond(i + off, tm), 0, 0))


def _dot(a, b):
    return jnp.dot(a, b, preferred_element_type=F32)


def _dot_nt(a, b):
    return lax.dot_general(a, b, (((1,), (1,)), ((), ())), preferred_element_type=F32)


def _rms(x):
    return x * lax.rsqrt(jnp.mean(x * x, axis=-1, keepdims=True) + EPS)


def _mod_norm(x, g_ref, mods_ref, idx):
    h = _rms(x) * g_ref[...]
    return h * (1.0 + mods_ref[3 * idx + 1:3 * idx + 2, :]) + mods_ref[3 * idx:3 * idx + 1, :]


def _gate(mods_ref, idx):
    return mods_ref[3 * idx + 2:3 * idx + 3, :]


def _adaln_kernel(c_ref, w_ref, b_ref, o_ref):
    c = c_ref[...]
    s = (c * jax.nn.sigmoid(c)).astype(BF16)
    o_ref[...] = _dot(s, w_ref[...].astype(BF16)) + b_ref[...]


def _adaln(cond, w_ada, b_ada):
    tn = 1152
    nj = 9 * D // tn
    return pl.pallas_call(
        _adaln_kernel,
        grid=(DEPTH, nj),
        in_specs=[
            pl.BlockSpec((N_COND, D), lambda l, j: (0, 0)),
            pl.BlockSpec((None, D, tn), lambda l, j: (l, 0, j)),
            pl.BlockSpec((None, 1, tn), lambda l, j: (l, 0, j)),
        ],
        out_specs=pl.BlockSpec((None, N_COND, tn), lambda l, j: (l, 0, j)),
        out_shape=jax.ShapeDtypeStruct((DEPTH, N_COND, 9 * D), F32),
        compiler_params=_params(2),
        name="adaln",
    )(cond, w_ada, b_ada.reshape(DEPTH, 1, 9 * D))


def _ffn_kernel(x_ref, mods_ref, g_ref, w1_ref, w3_ref, w2_ref, *rest, idx, final):
    o_ref = rest[-1]
    x = x_ref[...]
    hb = _mod_norm(x, g_ref, mods_ref, idx).astype(BF16)
    a = _dot(hb, w1_ref[...])
    b = _dot(hb, w3_ref[...])
    act = (a * jax.nn.sigmoid(a) * b).astype(BF16)
    y = x + (0.5 * _gate(mods_ref, idx)) * _dot(act, w2_ref[...])
    if final:
        y = _rms(y) * rest[0][...]
    o_ref[...] = y


def _ffn(x, mods, g, w1, w3, w2, layer, f, off, n, final_g=None, tm=512):
    final = final_g is not None
    idx = 2 * f

    def stacked(*shape):
        return pl.BlockSpec((None, None) + shape, lambda i: (layer, f, 0, 0), pipeline_mode=pl.Buffered(1))

    in_specs = [_x_spec(tm, off), _mods_spec(layer, tm, off), _resident((1, D)),
                stacked(D, D_FF), stacked(D, D_FF), stacked(D_FF, D)]
    args = [x, mods, g.reshape(1, D), w1, w3, w2]
    if final:
        in_specs.append(_resident((1, D)))
        args.append(final_g.reshape(1, D))
        out_spec = pl.BlockSpec((tm, D), lambda i: (i, 0))
        out_shape = jax.ShapeDtypeStruct((n * tm, D), F32)
        aliases = {}
    else:
        out_spec = _x_spec(tm, off)
        out_shape = jax.ShapeDtypeStruct(x.shape, F32)
        aliases = {0: 0}
    return pl.pallas_call(
        functools.partial(_ffn_kernel, idx=idx, final=final),
        grid=(n,), in_specs=in_specs, out_specs=out_spec, out_shape=out_shape,
        input_output_aliases=aliases, compiler_params=_params(1), name="ffn",
    )(*args)


def _proj_kernel(x_ref, a_ref, mods_ref, w_ref, o_ref):
    y = _dot(a_ref[...].astype(BF16), w_ref[...])
    o_ref[...] = x_ref[...] + _gate(mods_ref, 1) * y


def _proj_residual(x, a, mods, w, layer, off, n, tm=512):
    return pl.pallas_call(
        _proj_kernel,
        grid=(n,),
        in_specs=[_x_spec(tm, off), pl.BlockSpec((tm, D), lambda i: (i, 0)),
                  _mods_spec(layer, tm, off), _resident((D, D))],
        out_specs=_x_spec(tm, off),
        out_shape=jax.ShapeDtypeStruct(x.shape, F32),
        input_output_aliases={0: 0}, compiler_params=_params(1), name="proj_residual",
    )(x, a, mods, w)


def _ml_pre_kernel(x_ref, mods_ref, g_ref, wqkv_ref, wog_ref, wif_ref, bif_ref,
                   q_ref, k_ref, kt_ref, v_ref, og_ref, gates_ref):
    hb = _mod_norm(x_ref[...], g_ref, mods_ref, 1).astype(BF16)
    qkv = _dot(hb, wqkv_ref[...])
    q_ref[...] = qkv[:, :D].astype(BF16)
    k = qkv[:, D:2 * D] * (ML_DK ** -0.5)
    k_ref[...] = k.astype(BF16)
    kt_ref[...] = k.T.astype(BF16)
    v_ref[...] = qkv[:, 2 * D:].astype(BF16)
    og_ref[...] = jax.nn.sigmoid(_dot(hb, wog_ref[...]))
    gates_ref[...] = _dot(hb, wif_ref[...]) + bif_ref[...]


def _ml_pre(x, mods, g, wqkv, wog, wif, bif, layer, off, n, tm=512):
    tok = pl.BlockSpec((tm, D), lambda i: (i, 0))
    return pl.pallas_call(
        _ml_pre_kernel,
        grid=(n,),
        in_specs=[_x_spec(tm, off), _mods_spec(layer, tm, off), _resident((1, D)),
                  _resident((D, 3 * D)), _resident((D, D)), _resident((D, LANE)), _resident((1, LANE))],
        out_specs=[tok, tok, pl.BlockSpec((D, tm), lambda i: (0, i)), tok, tok,
                   pl.BlockSpec((tm, LANE), lambda i: (i, 0))],
        out_shape=[jax.ShapeDtypeStruct((n * tm, D), BF16)] * 2
        + [jax.ShapeDtypeStruct((D, n * tm), BF16), jax.ShapeDtypeStruct((n * tm, D), BF16),
           jax.ShapeDtypeStruct((n * tm, D), F32), jax.ShapeDtypeStruct((n * tm, LANE), F32)],
        compiler_params=_params(1), name="mlstm_pre",
    )(x, mods, g.reshape(1, D), wqkv, wog, wif, bif)


def _log_sigmoid(x):
    return jnp.minimum(x, 0.0) - jnp.log1p(jnp.exp(-jnp.abs(x)))


def _cumsum_rows(x, reverse):
    n = x.shape[0]
    row = lax.broadcasted_iota(jnp.int32, x.shape, 0)
    sh = 1
    while sh < n:
        if reverse:
            x = x + jnp.where(row < n - sh, pltpu.roll(x, n - sh, 0), 0.0)
        else:
            x = x + jnp.where(row >= sh, pltpu.roll(x, sh, 0), 0.0)
        sh *= 2
    return x


def _ml_scan_kernel(*refs, with_init, emit_state):
    fwd_refs, bwd_refs, refs = refs[:5], refs[5:10], refs[10:]
    if with_init:
        (c0_ref, n0_ref, m0_ref), refs = refs[:3], refs[3:]
    (hf_ref, hb_ref), refs = refs[:2], refs[2:]
    if emit_state:
        (co_ref, no_ref, mo_ref), refs = refs[:3], refs[3:]
    c_sc, n_sc, m_sc = refs
    c = pl.program_id(1)
    L = ML_CHUNK

    @pl.when(c == 0)
    def _():
        if with_init:
            c_sc[...] = c0_ref[...]
            n_sc[...] = n0_ref[...]
            m_sc[...] = m0_ref[...]
        else:
            c_sc[...] = jnp.zeros_like(c_sc)
            n_sc[...] = jnp.zeros_like(n_sc)
            m_sc[...] = jnp.zeros_like(m_sc)

    t_idx = lax.broadcasted_iota(jnp.int32, (L, L), 0)
    s_idx = lax.broadcasted_iota(jnp.int32, (L, L), 1)
    for d, ((q_ref, k_ref, kt_ref, v_ref, g_ref), h_ref) in enumerate(
            ((fwd_refs, hf_ref), (bwd_refs, hb_ref))):
        reverse = d == 1
        gts = g_ref[...]
        bsum = _cumsum_rows(_log_sigmoid(gts), reverse)
        gts_t = gts.T
        bsum_t = bsum.T
        causal = (s_idx >= t_idx) if reverse else (s_idx <= t_idx)
        for hd in range(ML_HEADS):
            r = d * ML_HEADS + hd
            ci, cf = d * 8 + hd, d * 8 + 4 + hd
            lo, hi = hd * ML_DK, (hd + 1) * ML_DK
            bcol = bsum[:, cf:cf + 1]
            brow, irow = bsum_t[cf:cf + 1, :], gts_t[ci:ci + 1, :]
            m_old = m_sc[r:r + 1, 0:1]
            n_old = n_sc[r:r + 1, :]
            c_old = c_sc[r]
            qh, kh, vh = q_ref[:, lo:hi], k_ref[:, lo:hi], v_ref[:, lo:hi]
            kth = kt_ref[lo:hi, :]

            dmat = jnp.where(causal, bcol - brow + irow, NEG_INF)
            inter = bcol + m_old
            m_t = jnp.maximum(inter, jnp.max(dmat, axis=1, keepdims=True))
            a = jnp.exp(dmat - m_t) * _dot(qh, kth)
            w_inter = jnp.exp(inter - m_t)
            num = _dot(a.astype(BF16), vh) + w_inter * _dot(qh, c_old.astype(BF16))
            den = jnp.sum(a, axis=1, keepdims=True) + w_inter * jnp.sum(
                qh.astype(F32) * n_old, axis=1, keepdims=True)
            h_ref[:, lo:hi] = num / jnp.maximum(jnp.abs(den), jnp.exp(-m_t))

            b_last = brow[:, 0:1] if reverse else brow[:, L - 1:L]
            g_row = b_last - brow + irow
            m_new = jnp.maximum(b_last + m_old, jnp.max(g_row, axis=1, keepdims=True))
            decay = jnp.exp(b_last + m_old - m_new)
            w_row = jnp.exp(g_row - m_new)
            c_sc[r] = decay * c_old + _dot((kth.astype(F32) * w_row).astype(BF16), vh)
            w_rows = jnp.broadcast_to(w_row, (BF16_ROWS, L)).astype(BF16)
            n_sc[r:r + 1, :] = decay * n_old + _dot(w_rows, kh)[0:1, :]
            m_sc[r:r + 1, :] = jnp.broadcast_to(m_new, (1, LANE))

    if emit_state:
        @pl.when(c == pl.num_programs(1) - 1)
        def _():
            co_ref[...] = c_sc[...]
            no_ref[...] = n_sc[...]
            mo_ref[...] = m_sc[...]


def _ml_scan(q, k, kt, v, gates, nb, seq, init=None, emit_state=False):
    nc = seq // ML_CHUNK
    fwd = lambda b, c: (b * nc + c, 0)
    bwd = lambda b, c: (b * nc + (nc - 1 - c), 0)
    tok = lambda im: pl.BlockSpec((ML_CHUNK, D), im)
    ktsp = lambda im: pl.BlockSpec((D, ML_CHUNK), lambda b, c: im(b, c)[::-1])
    gsp = lambda im: pl.BlockSpec((ML_CHUNK, LANE), im)
    st_c = pl.BlockSpec((None, N_STATE, ML_DK, ML_DK), lambda b, c: (b, 0, 0, 0))
    st_n = pl.BlockSpec((None, N_STATE, ML_DK), lambda b, c: (b, 0, 0))
    st_m = pl.BlockSpec((None, N_STATE, LANE), lambda b, c: (b, 0, 0))
    in_specs = [tok(fwd), tok(fwd), ktsp(fwd), tok(fwd), gsp(fwd),
                tok(bwd), tok(bwd), ktsp(bwd), tok(bwd), gsp(bwd)]
    args = [q, k, kt, v, gates] * 2
    if init is not None:
        in_specs += [st_c, st_n, st_m]
        args += list(init)
    out_specs = [tok(fwd), tok(bwd)]
    out_shape = [jax.ShapeDtypeStruct((nb * seq, D), F32)] * 2
    if emit_state:
        out_specs += [st_c, st_n, st_m]
        out_shape += [jax.ShapeDtypeStruct((nb, N_STATE, ML_DK, ML_DK), F32),
                      jax.ShapeDtypeStruct((nb, N_STATE, ML_DK), F32),
                      jax.ShapeDtypeStruct((nb, N_STATE, LANE), F32)]
    return pl.pallas_call(
        functools.partial(_ml_scan_kernel, with_init=init is not None, emit_state=emit_state),
        grid=(nb, nc), in_specs=in_specs, out_specs=out_specs, out_shape=out_shape,
        scratch_shapes=[pltpu.VMEM((N_STATE, ML_DK, ML_DK), F32), pltpu.VMEM((N_STATE, ML_DK), F32),
                        pltpu.VMEM((N_STATE, LANE), F32)],
        compiler_params=_params(2, last_arbitrary=True), name="mlstm_scan",
    )(*args)


def _ml_post_kernel(x_ref, hf_ref, hb_ref, og_ref, mods_ref, hg_ref, w_ref, o_ref):
    hsum = hf_ref[...] + hb_ref[...]
    hn = jnp.concatenate(
        [_rms(hsum[:, hd * ML_DK:(hd + 1) * ML_DK]) for hd in range(ML_HEADS)], axis=1)
    y = _dot((og_ref[...] * (hn * hg_ref[...])).astype(BF16), w_ref[...])
    o_ref[...] = x_ref[...] + _gate(mods_ref, 1) * y


def _ml_post(x, hf, hb, og, mods, head_g, w_out, layer, off, n, tm=512):
    tok = pl.BlockSpec((tm, D), lambda i: (i, 0))
    return pl.pallas_call(
        _ml_post_kernel,
        grid=(n,),
        in_specs=[_x_spec(tm, off), tok, tok, tok, _mods_spec(layer, tm, off),
                  _resident((1, D)), _resident((D, D))],
        out_specs=_x_spec(tm, off),
        out_shape=jax.ShapeDtypeStruct(x.shape, F32),
        input_output_aliases={0: 0}, compiler_params=_params(1), name="mlstm_post",
    )(x, hf, hb, og, mods, head_g.reshape(1, D), w_out)


def _fn_chan_kernel(x_ref, mods_ref, g_ref, cc_ref, sc_ref, a_ref, b_ref):
    hb = _mod_norm(x_ref[...], g_ref, mods_ref, 1).astype(BF16)
    for gi in range(FOURIER_GROUPS):
        sl = slice(gi * FG, (gi + 1) * FG)
        a_ref[:, sl] = _dot(hb[:, sl], cc_ref[...]).astype(BF16)
        b_ref[:, sl] = _dot(hb[:, sl], sc_ref[...]).astype(BF16)


def _fn_chan(x, mods, g, cc, sc, layer, off, n, tm=512):
    tok = pl.BlockSpec((tm, D), lambda i: (i, 0))
    return pl.pallas_call(
        _fn_chan_kernel,
        grid=(n,),
        in_specs=[_x_spec(tm, off), _mods_spec(layer, tm, off), _resident((1, D)),
                  _resident((FG, FG)), _resident((FG, FG))],
        out_specs=[tok, tok],
        out_shape=[jax.ShapeDtypeStruct((n * tm, D), BF16)] * 2,
        compiler_params=_params(1), name="fourier_chan",
    )(x, mods, g.reshape(1, D), cc, sc)


def _fn_seq_kernel(x_ref, a_ref, b_ref, cs_ref, ss_ref, mods_ref, w_ref, bias_ref, o_ref, *, scale):
    f = (_dot(cs_ref[...], a_ref[...]) - _dot(ss_ref[...], b_ref[...])) * scale
    y = _dot(f.astype(BF16), w_ref[...]) + bias_ref[...]
    o_ref[...] = x_ref[...] + _gate(mods_ref, 1) * y


def _fn_seq(x, a, b, cs, ss, mods, w_out, b_out, layer, off_tok, nb, seq, tr):
    nt = seq // tr
    off = off_tok // tr
    xs = pl.BlockSpec((tr, D), lambda bi, t: (off + bi * nt + t, 0))
    ab = pl.BlockSpec((seq, D), lambda bi, t: (bi, 0))
    tbl = pl.BlockSpec((tr, seq), lambda bi, t: (t, 0))
    mods_spec = pl.BlockSpec(
        (None, 9, D), lambda bi, t: (layer * N_COND + _tile_cond(off + bi * nt + t, tr), 0, 0))
    return pl.pallas_call(
        functools.partial(_fn_seq_kernel, scale=float((seq * FG) ** -0.5)),
        grid=(nb, nt),
        in_specs=[xs, ab, ab, tbl, tbl, mods_spec, _resident((D, D)), _resident((1, D))],
        out_specs=xs,
        out_shape=jax.ShapeDtypeStruct(x.shape, F32),
        input_output_aliases={0: 0}, compiler_params=_params(2), name="fourier_seq",
    )(x, a, b, cs, ss, mods, w_out, b_out.reshape(1, D))


def _dft_tables(n):
    ang = 2.0 * np.pi * ((np.arange(n)[:, None] * np.arange(n)[None, :]) % n) / n
    return (jnp.asarray(np.cos(ang), F32).astype(BF16), jnp.asarray(np.sin(ang), F32).astype(BF16))


def _gm_kernel(x_ref, mods_ref, g_ref, win_ref, bin_ref, vg_ref, ws_ref, bs_ref, wout_ref, o_ref, sv_sc):
    x = x_ref[...]
    hb = _mod_norm(x, g_ref, mods_ref, 1).astype(BF16)
    z = _dot(hb, win_ref[...]) + bin_ref[...]
    z = z * (0.5 * (1.0 + jnp.tanh(np.sqrt(2.0 / np.pi) * (z + 0.044715 * (z * z * z)))))
    u = z[:, :GM_W]
    v = (_rms(z[:, GM_W:]) * vg_ref[...]).astype(BF16)
    gw = GM_W // GM_GROUPS
    for ch in range(x.shape[0] // GM_CHUNK):
        rows = slice(ch * GM_CHUNK, (ch + 1) * GM_CHUNK)
        for gi in range(GM_GROUPS):
            cols = slice(gi * gw, (gi + 1) * gw)
            sv_sc[rows, cols] = _dot(ws_ref[gi], v[rows, cols]) + bs_ref[:, gi:gi + 1]
    y = _dot((u * sv_sc[...]).astype(BF16), wout_ref[...])
    o_ref[...] = x + _gate(mods_ref, 1) * y


def _gmlp(x, mods, g, w_in, b_in, v_g, w_s, b_s_t, w_out, layer, tm=512):
    return pl.pallas_call(
        _gm_kernel,
        grid=(N_TOK // tm,),
        in_specs=[_x_spec(tm, 0), _mods_spec(layer, tm, 0), _resident((1, D)),
                  _resident((D, 2 * GM_W)), _resident((1, 2 * GM_W)), _resident((1, GM_W)),
                  _resident((GM_GROUPS, GM_CHUNK, GM_CHUNK)), _resident((GM_CHUNK, GM_GROUPS)),
                  _resident((GM_W, D))],
        out_specs=_x_spec(tm, 0),
        out_shape=jax.ShapeDtypeStruct(x.shape, F32),
        scratch_shapes=[pltpu.VMEM((tm, GM_W), F32)],
        input_output_aliases={0: 0}, compiler_params=_params(1), name="gmlp",
    )(x, mods, g.reshape(1, D), w_in, b_in.reshape(1, 2 * GM_W), v_g.reshape(1, GM_W), w_s, b_s_t, w_out)


def _na_pre_kernel(x_ref, mods_ref, g_ref, w_ref, q_ref, k_ref, v_ref):
    hb = _mod_norm(x_ref[...], g_ref, mods_ref, 1).astype(BF16)
    qkv = _dot(hb, w_ref[...])
    q_ref[...] = (qkv[:, :D] * (NA_HD ** -0.5)).astype(BF16)
    k_ref[...] = qkv[:, D:2 * D].astype(k_ref.dtype)
    v_ref[...] = qkv[:, 2 * D:].astype(v_ref.dtype)


def _na_pre(x, mods, g, w_qkv, layer, off, n, kv_dtype, tm=512):
    tok = pl.BlockSpec((tm, D), lambda i: (i, 0))
    return pl.pallas_call(
        _na_pre_kernel,
        grid=(n,),
        in_specs=[_x_spec(tm, off), _mods_spec(layer, tm, off), _resident((1, D)), _resident((D, 3 * D))],
        out_specs=[tok, tok, tok],
        out_shape=[jax.ShapeDtypeStruct((n * tm, D), BF16),
                   jax.ShapeDtypeStruct((n * tm, D), kv_dtype),
                   jax.ShapeDtypeStruct((n * tm, D), kv_dtype)],
        compiler_params=_params(1), name="na_qkv",
    )(x, mods, g.reshape(1, D), w_qkv)


def _half_mask(shape, e):
    lane = lax.broadcasted_iota(jnp.int32, shape, len(shape) - 1)
    return (lane < NA_HD) if e == 0 else (lane >= NA_HD)


def _ctx_attn_kernel(x_ref, q_ref, k_ref, v_ref, mods_ref, w_ref, o_ref, att_sc):
    for j in range(NA_PAIRS):
        sl = slice(j * LANE, (j + 1) * LANE)
        s = _dot_nt(_pair_rows(q_ref[:, sl]), k_ref[:, sl].astype(BF16))
        p = jnp.exp(s - jnp.max(s, axis=1, keepdims=True))
        o = _dot(p.astype(BF16), v_ref[:, sl].astype(BF16)) / jnp.sum(p, axis=1, keepdims=True)
        att_sc[:, sl] = jnp.where(_half_mask((SEQ, LANE), 0), o[:SEQ], o[SEQ:]).astype(BF16)
    o_ref[...] = x_ref[...] + _gate(mods_ref, 1) * _dot(att_sc[...], w_ref[...])


def _ctx_attn(x, q, k, v, mods, w_out, layer):
    tok = pl.BlockSpec((SEQ, D), lambda b: (b, 0))
    return pl.pallas_call(
        _ctx_attn_kernel,
        grid=(BATCH,),
        in_specs=[tok, tok, tok, tok, _mods_spec(layer, SEQ, 0), _resident((D, D))],
        out_specs=tok,
        out_shape=jax.ShapeDtypeStruct(x.shape, F32),
        scratch_shapes=[pltpu.VMEM((SEQ, D), BF16)],
        input_output_aliases={0: 0}, compiler_params=_params(1), name="ctx_attn",
    )(x, q, k, v, mods, w_out)


def _na_bias_kernel(rpb_ref, o_ref, pair_sc):
    h = pl.program_id(0)
    shape = (GRID_W, LANE)
    q = lax.broadcasted_iota(jnp.int32, shape, 0)
    lane = lax.broadcasted_iota(jnp.int32, shape, 1)
    x = lane & (GRID_W - 1)
    first = lane < GRID_W
    dc = x - q + (NA_KW - 1)
    q_start = jnp.clip(q - NA_KW // 2, 0, GRID_W - NA_KW)
    in_window = (x >= q_start) & (x < q_start + NA_KW)
    n_dc = 2 * NA_KW - 1
    for dr in range(2 * NA_KH - 2):
        acc = jnp.zeros(shape, F32)
        for j in range(n_dc):
            val = jnp.where(first, rpb_ref[h, dr * n_dc + j], rpb_ref[h, (dr + 1) * n_dc + j])
            acc = jnp.where(dc == j, val, acc)
        pair_sc[dr] = jnp.where(in_window, acc, NEG_INF)
    for o in range(NA_KH):
        for t in range(NA_KH // 2):
            o_ref[o, :, t * LANE:(t + 1) * LANE] = pair_sc[NA_KH - 1 - o + 2 * t]


def _na_bias_tables(rpb):
    n_rel = (2 * NA_KH - 1) * (2 * NA_KW - 1)
    t = pl.pallas_call(
        _na_bias_kernel,
        grid=(NA_HEADS,),
        in_specs=[pl.BlockSpec(memory_space=pltpu.SMEM)],
        out_specs=pl.BlockSpec((NA_KH, None, GRID_W, NA_WIN), lambda h: (0, h, 0, 0)),
        out_shape=jax.ShapeDtypeStruct((NA_KH, NA_HEADS, GRID_W, NA_WIN), F32),
        scratch_shapes=[pltpu.VMEM((2 * NA_KH - 2, GRID_W, LANE), F32)],
        compiler_params=_params(1), name="na_bias",
    )(rpb.reshape(NA_HEADS, n_rel))
    return t.reshape(NA_KH, NA_HEADS * GRID_W, NA_WIN)


def _pair_rows(qp):
    zero = jnp.zeros_like(qp)
    return jnp.concatenate([jnp.where(_half_mask(qp.shape, 0), qp, zero),
                            jnp.where(_half_mask(qp.shape, 1), qp, zero)], axis=0)


def _na_lat_kernel(q_ref, k_ref, v_ref, kc_ref, vc_ref, bias_ref, o_ref, s_sc, p_sc):
    r = pl.program_id(1)
    start = pl.multiple_of(jnp.clip(r - NA_KH // 2, 0, NA_ROWS - NA_KH) * GRID_W, GRID_W)
    for j in range(NA_PAIRS):
        sl = slice(j * LANE, (j + 1) * LANE)
        qb = _pair_rows(q_ref[:, sl])
        s_sc[sl, :NA_WIN] = _dot_nt(qb, k_ref[pl.ds(start, NA_WIN), sl]) + bias_ref[sl, :]
        s_sc[sl, NA_WIN:] = _dot_nt(qb, kc_ref[:, sl])
    s = s_sc[...]
    p = jnp.exp(s - jnp.max(s, axis=1, keepdims=True))
    inv = 1.0 / jnp.sum(p, axis=1, keepdims=True)
    p_sc[...] = p.astype(BF16)
    for j in range(NA_PAIRS):
        sl = slice(j * LANE, (j + 1) * LANE)
        o = (_dot(p_sc[sl, :NA_WIN], v_ref[pl.ds(start, NA_WIN), sl])
             + _dot(p_sc[sl, NA_WIN:], vc_ref[:, sl])) * inv[sl]
        o_ref[:, sl] = jnp.where(_half_mask((GRID_W, LANE), 0), o[:GRID_W], o[GRID_W:])


def _na_latent(q, k, v, kc, vc, bias):
    row_class = lambda r: r - jnp.clip(r - NA_KH // 2, 0, NA_ROWS - NA_KH)
    seq_kv = pl.BlockSpec((None, DEC_SEQ, D), lambda b, r: (b, 0, 0))
    ctx_kv = pl.BlockSpec((None, PAST_LEN, D), lambda b, r: (b, 0, 0))
    n_rows = NA_HEADS * GRID_W
    return pl.pallas_call(
        _na_lat_kernel,
        grid=(DEC_BATCH, NA_ROWS),
        in_specs=[pl.BlockSpec((GRID_W, D), lambda b, r: (b * NA_ROWS + r, 0)),
                  seq_kv, seq_kv, ctx_kv, ctx_kv,
                  pl.BlockSpec((None, n_rows, NA_WIN), lambda b, r: (row_class(r), 0, 0))],
        out_specs=pl.BlockSpec((GRID_W, D), lambda b, r: (b * NA_ROWS + r, 0)),
        out_shape=jax.ShapeDtypeStruct((NS_TOK, D), F32),
        scratch_shapes=[pltpu.VMEM((n_rows, NA_WIN + PAST_LEN), F32),
                        pltpu.VMEM((n_rows, NA_WIN + PAST_LEN), BF16)],
        compiler_params=_params(2), name="na_latent",
    )(q, k.reshape(DEC_BATCH, DEC_SEQ, D), v.reshape(DEC_BATCH, DEC_SEQ, D), kc, vc, bias)


def kernel(x_prompt, x_sample, state_mlstm_C, state_mlstm_n, state_mlstm_m, cache_na_k, cache_na_v, c, c_ctx, w_ada, b_ada, norm_g, final_g, ffn_w1, ffn_w3, ffn_w2, ml_w_qkv, ml_w_if, ml_b_if, ml_w_og, ml_head_g, ml_w_out, fn_w_out, fn_b_out, gm_w_in, gm_b_in, gm_v_g, gm_w_s, gm_b_s, gm_w_out, na_w_qkv, na_w_out, na_rpb):
    tm = 512
    n_p, n_s, n_all = NP_TOK // tm, NS_TOK // tm, N_TOK // tm
    x = jnp.concatenate([x_prompt.reshape(NP_TOK, D), x_sample.reshape(NS_TOK, D)], axis=0)

    cond = jnp.zeros((N_COND, D), F32).at[0].set(c_ctx).at[1:1 + DEC_BATCH].set(c)
    mods = _adaln(cond, w_ada, b_ada).reshape(DEPTH * N_COND, 9, D)

    w1, w3, w2 = ffn_w1.astype(BF16), ffn_w3.astype(BF16), ffn_w2.astype(BF16)
    outs = {}
    for l in range(DEPTH):
        kind, j = l % 4, l // 4
        x = _ffn(x, mods, norm_g[l, 0], w1, w3, w2, l, 0, 0, n_all)
        g = norm_g[l, 1]
        if kind == 0:
            wif = jnp.pad(jnp.transpose(ml_w_if[j], (1, 0, 2)).reshape(D, 16), ((0, 0), (0, LANE - 16)))
            bif = jnp.pad(ml_b_if[j].reshape(1, 16), ((0, 0), (0, LANE - 16)))
            wqkv, wog, wout = ml_w_qkv[j].astype(BF16), ml_w_og[j].astype(BF16), ml_w_out[j].astype(BF16)
            for off, n, nb, seq in ((0, n_p, BATCH, SEQ), (n_p, n_s, DEC_BATCH, DEC_SEQ)):
                q, k, kt, v, og, gates = _ml_pre(x, mods, g, wqkv, wog, wif.astype(BF16), bif, l, off, n)
                if off == 0:
                    hf, hb, c_new, n_new, m_new = _ml_scan(q, k, kt, v, gates, nb, seq, emit_state=True)
                    outs["C"] = c_new.reshape(BATCH, 1, 2, ML_HEADS, ML_DK, ML_DK)
                    outs["n"] = n_new.reshape(BATCH, 1, 2, ML_HEADS, ML_DK)
                    outs["m"] = m_new[:, :, 0].reshape(BATCH, 1, 2, ML_HEADS)
                else:
                    init = (state_mlstm_C[:, j].reshape(DEC_BATCH, N_STATE, ML_DK, ML_DK),
                            state_mlstm_n[:, j].reshape(DEC_BATCH, N_STATE, ML_DK),
                            jnp.broadcast_to(state_mlstm_m[:, j].reshape(DEC_BATCH, N_STATE, 1),
                                             (DEC_BATCH, N_STATE, LANE)))
                    hf, hb = _ml_scan(q, k, kt, v, gates, nb, seq, init=init)
                x = _ml_post(x, hf, hb, og, mods, ml_head_g[j], wout, l, off, n)
        elif kind == 1:
            cc, sc = _dft_tables(FG)
            wout = fn_w_out[j].astype(BF16)
            for off, n, nb, seq, tr in ((0, n_p, BATCH, SEQ, SEQ), (n_p, n_s, DEC_BATCH, DEC_SEQ, 512)):
                a, b = _fn_chan(x, mods, g, cc, sc, l, off, n)
                cs, ss = _dft_tables(seq)
                x = _fn_seq(x, a, b, cs, ss, mods, wout, fn_b_out[j], l, off * tm, nb, seq, tr)
        elif kind == 2:
            x = _gmlp(x, mods, g, gm_w_in[j].astype(BF16), gm_b_in[j], gm_v_g[j],
                      gm_w_s[j].astype(BF16), gm_b_s[j].T, gm_w_out[j].astype(BF16), l)
        else:
            wqkv, wout = na_w_qkv[j].astype(BF16), na_w_out[j].astype(BF16)
            q, k, v = _na_pre(x, mods, g, wqkv, l, 0, n_p, F32)
            outs["k"] = k.reshape(BATCH, 1, SEQ, NA_HEADS, NA_HD)
            outs["v"] = v.reshape(BATCH, 1, SEQ, NA_HEADS, NA_HD)
            x = _ctx_attn(x, q, k, v, mods, wout, l)
            q, k, v = _na_pre(x, mods, g, wqkv, l, n_p, n_s, BF16)
            att = _na_latent(q, k, v,
                             cache_na_k[:, j].reshape(DEC_BATCH, PAST_LEN, D).astype(BF16),
                             cache_na_v[:, j].reshape(DEC_BATCH, PAST_LEN, D).astype(BF16),
                             _na_bias_tables(na_rpb[j]))
            x = _proj_residual(x, att, mods, wout, l, n_p, n_s)
        if l < DEPTH - 1:
            x = _ffn(x, mods, norm_g[l, 2], w1, w3, w2, l, 1, 0, n_all)
        else:
            y_p = _ffn(x, mods, norm_g[l, 2], w1, w3, w2, l, 1, 0, n_p, final_g=final_g)
            y_s = _ffn(x, mods, norm_g[l, 2], w1, w3, w2, l, 1, n_p, n_s, final_g=final_g)
    return (y_p.reshape(BATCH, SEQ, D), y_s.reshape(DEC_BATCH, DEC_SEQ, D),
            outs["C"], outs["n"], outs["m"], outs["k"], outs["v"])
```

```python
import functools

import numpy as np
import jax
import jax.numpy as jnp
from jax import lax
from jax.experimental import pallas as pl
from jax.experimental.pallas import tpu as pltpu

D = 1024
BATCH = 32
SEQ = 256
DEPTH = 4
DEC_BATCH = 2
DEC_SEQ = 2048
PAST_LEN = 512
GRID_W = 64
D_FF = 2816
EPS = 1e-6

NP_TOK = BATCH * SEQ
NS_TOK = DEC_BATCH * DEC_SEQ
N_TOK = NP_TOK + NS_TOK

ML_HEADS = 4
ML_DK = D // ML_HEADS
ML_CHUNK = 128
N_STATE = 2 * ML_HEADS

FOURIER_GROUPS = 4
FG = D // FOURIER_GROUPS

GM_W = D
GM_GROUPS = 4
GM_CHUNK = 128

NA_HEADS = 16
NA_HD = D // NA_HEADS
NA_KH = 8
NA_KW = 16
NA_ROWS = DEC_SEQ // GRID_W
NA_WIN = NA_KH * GRID_W
NA_PAIRS = NA_HEADS // 2
N_COND = 8

LANE = 128
BF16_ROWS = 16
FFN_ROWS = 256
FFN_TILE = 1024
VMEM_LIMIT = 56 * 1024 * 1024

F32 = jnp.float32
BF16 = jnp.bfloat16
NEG_INF = float("-inf")


def _params(n_axes, last_arbitrary=False):
    sem = ["parallel"] * n_axes
    if last_arbitrary:
        sem[-1] = "arbitrary"
    return pltpu.CompilerParams(dimension_semantics=tuple(sem), vmem_limit_bytes=VMEM_LIMIT)


def _resident(shape):
    zeros = (0,) * len(shape)
    return pl.BlockSpec(shape, lambda *_: zeros, pipeline_mode=pl.Buffered(1))


def _tile_cond(i, tm):
    row0 = i * tm
    return jnp.where(row0 < NP_TOK, 0, 1 + (row0 - NP_TOK) // DEC_SEQ)


def _x_spec(tm, off):
    return pl.BlockSpec((tm, D), lambda i: (i + off, 0))


def _mods_spec(layer, tm, off):
    return pl.BlockSpec((None, 9, D), lambda i: (layer * N_COND + _tile_cond(i + off, tm), 0, 0))


def _dot(a, b):
    return jnp.dot(a, b, preferred_element_type=F32)


def _dot_nt(a, b):
    return lax.dot_general(a, b, (((1,), (1,)), ((), ())), preferred_element_type=F32)


def _rms(x):
    return x * lax.rsqrt(jnp.mean(x * x, axis=-1, keepdims=True) + EPS)


def _mod_norm(x, g_ref, mods_ref, idx):
    h = _rms(x) * g_ref[...]
    return h * (1.0 + mods_ref[3 * idx + 1:3 * idx + 2, :]) + mods_ref[3 * idx:3 * idx + 1, :]


def _gate(mods_ref, idx):
    return mods_ref[3 * idx + 2:3 * idx + 3, :]


def _row_groups(n_rows, group=256):
    return [slice(r0, r0 + group) for r0 in range(0, n_rows, group)]


def _adaln_kernel(c_ref, w_ref, b_ref, o_ref):
    c = c_ref[...]
    s = (c * jax.nn.sigmoid(c)).astype(BF16)
    o_ref[...] = _dot(s, w_ref[...].astype(BF16)) + b_ref[...]


def _adaln(cond, w_ada, b_ada):
    tn = 1152
    nj = 9 * D // tn
    return pl.pallas_call(
        _adaln_kernel,
        grid=(DEPTH, nj),
        in_specs=[
            pl.BlockSpec((N_COND, D), lambda l, j: (0, 0)),
            pl.BlockSpec((None, D, tn), lambda l, j: (l, 0, j)),
            pl.BlockSpec((None, 1, tn), lambda l, j: (l, 0, j)),
        ],
        out_specs=pl.BlockSpec((None, N_COND, tn), lambda l, j: (l, 0, j)),
        out_shape=jax.ShapeDtypeStruct((DEPTH, N_COND, 9 * D), F32),
        compiler_params=_params(2),
        name="adaln",
    )(cond, w_ada, b_ada.reshape(DEPTH, 1, 9 * D))


def _ffn_kernel(x_ref, mods_ref, g_ref, w1_ref, w3_ref, w2_ref, *rest, idx, final):
    o_ref = rest[-1]
    for r0 in range(0, x_ref.shape[0], FFN_ROWS):
        rows = slice(r0, r0 + FFN_ROWS)
        x = x_ref[rows, :]
        hb = _mod_norm(x, g_ref, mods_ref, idx).astype(BF16)
        a = _dot(hb, w1_ref[...])
        b = _dot(hb, w3_ref[...])
        act = (a * jax.nn.sigmoid(a) * b).astype(BF16)
        y = x + (0.5 * _gate(mods_ref, idx)) * _dot(act, w2_ref[...])
        if final:
            y = _rms(y) * rest[0][...]
        o_ref[rows, :] = y


def _ffn(x, mods, g, w1, w3, w2, layer, f, off, n, final_g=None, tm=FFN_TILE):
    final = final_g is not None
    idx = 2 * f

    def stacked(*shape):
        return pl.BlockSpec((None, None) + shape, lambda i: (layer, f, 0, 0), pipeline_mode=pl.Buffered(1))

    in_specs = [_x_spec(tm, off), _mods_spec(layer, tm, off), _resident((1, D)),
                stacked(D, D_FF), stacked(D, D_FF), stacked(D_FF, D)]
    args = [x, mods, g.reshape(1, D), w1, w3, w2]
    if final:
        in_specs.append(_resident((1, D)))
        args.append(final_g.reshape(1, D))
        out_spec = pl.BlockSpec((tm, D), lambda i: (i, 0))
        out_shape = jax.ShapeDtypeStruct((n * tm, D), F32)
        aliases = {}
    else:
        out_spec = _x_spec(tm, off)
        out_shape = jax.ShapeDtypeStruct(x.shape, F32)
        aliases = {0: 0}
    return pl.pallas_call(
        functools.partial(_ffn_kernel, idx=idx, final=final),
        grid=(n,), in_specs=in_specs, out_specs=out_spec, out_shape=out_shape,
        input_output_aliases=aliases, compiler_params=_params(1), name="ffn",
    )(*args)


def _proj_kernel(x_ref, a_ref, mods_ref, w_ref, o_ref):
    y = _dot(a_ref[...].astype(BF16), w_ref[...])
    o_ref[...] = x_ref[...] + _gate(mods_ref, 1) * y


def _proj_residual(x, a, mods, w, layer, off, n, tm=512):
    return pl.pallas_call(
        _proj_kernel,
        grid=(n,),
        in_specs=[_x_spec(tm, off), pl.BlockSpec((tm, D), lambda i: (i, 0)),
                  _mods_spec(layer, tm, off), _resident((D, D))],
        out_specs=_x_spec(tm, off),
        out_shape=jax.ShapeDtypeStruct(x.shape, F32),
        input_output_aliases={0: 0}, compiler_params=_params(1), name="proj_residual",
    )(x, a, mods, w)


def _ml_pre_kernel(x_ref, mods_ref, g_ref, wqkv_ref, wog_ref, wif_ref, bif_ref,
                   q_ref, k_ref, kt_ref, v_ref, og_ref, gates_ref):
    for rows in _row_groups(x_ref.shape[0]):
        hb = _mod_norm(x_ref[rows, :], g_ref, mods_ref, 1).astype(BF16)
        qkv = _dot(hb, wqkv_ref[...])
        q_ref[rows, :] = qkv[:, :D].astype(BF16)
        k = qkv[:, D:2 * D] * (ML_DK ** -0.5)
        k_ref[rows, :] = k.astype(BF16)
        kt_ref[:, rows] = k.T.astype(BF16)
        v_ref[rows, :] = qkv[:, 2 * D:].astype(BF16)
        og_ref[rows, :] = jax.nn.sigmoid(_dot(hb, wog_ref[...]))
        gates_ref[rows, :] = _dot(hb, wif_ref[...]) + bif_ref[...]


def _ml_pre(x, mods, g, wqkv, wog, wif, bif, layer, off, n, tm=512):
    tok = pl.BlockSpec((tm, D), lambda i: (i, 0))
    return pl.pallas_call(
        _ml_pre_kernel,
        grid=(n,),
        in_specs=[_x_spec(tm, off), _mods_spec(layer, tm, off), _resident((1, D)),
                  _resident((D, 3 * D)), _resident((D, D)), _resident((D, 2 * LANE)), _resident((1, 2 * LANE))],
        out_specs=[tok, tok, pl.BlockSpec((D, tm), lambda i: (0, i)), tok, tok,
                   pl.BlockSpec((tm, 2 * LANE), lambda i: (i, 0))],
        out_shape=[jax.ShapeDtypeStruct((n * tm, D), BF16)] * 2
        + [jax.ShapeDtypeStruct((D, n * tm), BF16), jax.ShapeDtypeStruct((n * tm, D), BF16),
           jax.ShapeDtypeStruct((n * tm, D), F32), jax.ShapeDtypeStruct((n * tm, 2 * LANE), F32)],
        compiler_params=_params(1), name="mlstm_pre",
    )(x, mods, g.reshape(1, D), wqkv, wog, wif, bif)


def _log_sigmoid(x):
    return jnp.minimum(x, 0.0) - jnp.log1p(jnp.exp(-jnp.abs(x)))


def _scan_rows(x, fwd_lanes, op, identity):
    n = x.shape[0]
    row = lax.broadcasted_iota(jnp.int32, x.shape, 0)
    sh = 1
    while sh < n:
        prev = jnp.where(row >= sh, pltpu.roll(x, sh, 0), identity)
        nxt = jnp.where(row < n - sh, pltpu.roll(x, n - sh, 0), identity)
        x = op(x, jnp.where(fwd_lanes, prev, nxt))
        sh *= 2
    return x


def _ml_scan_kernel(*refs, with_init, emit_state):
    fwd_refs, bwd_refs, refs = refs[:5], refs[5:10], refs[10:]
    if with_init:
        (c0_ref, n0_ref, m0_ref), refs = refs[:3], refs[3:]
    (hf_ref, hb_ref), refs = refs[:2], refs[2:]
    if emit_state:
        (co_ref, no_ref, mo_ref), refs = refs[:3], refs[3:]
    c_sc, n_sc, m_sc = refs
    c = pl.program_id(1)
    L = ML_CHUNK

    @pl.when(c == 0)
    def _():
        if with_init:
            c_sc[...] = c0_ref[...]
            n_sc[...] = n0_ref[...]
            m_sc[...] = m0_ref[...]
        else:
            c_sc[...] = jnp.zeros_like(c_sc)
            n_sc[...] = jnp.zeros_like(n_sc)
            m_sc[...] = jnp.zeros_like(m_sc)

    t_idx = lax.broadcasted_iota(jnp.int32, (L, L), 0)
    s_idx = lax.broadcasted_iota(jnp.int32, (L, L), 1)
    lane = lax.broadcasted_iota(jnp.int32, (L, LANE), 1)
    fwd_lanes = lane < ML_HEADS
    gf_ref, gb_ref = fwd_refs[4], bwd_refs[4]
    i_pre = jnp.where(fwd_lanes, gf_ref[:, :LANE], gb_ref[:, :LANE])
    f_pre = jnp.where(fwd_lanes, gf_ref[:, LANE:], gb_ref[:, LANE:])
    log_f = jnp.where(lane < N_STATE, _log_sigmoid(f_pre), 0.0)
    bsum = _scan_rows(log_f, fwd_lanes, jnp.add, 0.0)
    rel = i_pre - bsum
    m_old = m_sc[...]
    mm = jnp.maximum(m_old, _scan_rows(rel, fwd_lanes, jnp.maximum, NEG_INF))
    w_inter = jnp.exp(m_old - mm)
    floor = jnp.exp(-(bsum + mm))
    mx = jnp.maximum(m_old, jnp.max(rel, axis=0, keepdims=True))
    b_last = jnp.where(fwd_lanes[0:1], bsum[L - 1:L, :], bsum[0:1, :])
    decay = jnp.exp(m_old - mx)
    m_sc[...] = b_last + mx
    rel_t = rel.T

    t_idx = lax.broadcasted_iota(jnp.int32, (L, L), 0)
    s_idx = lax.broadcasted_iota(jnp.int32, (L, L), 1)
    for d, ((q_ref, k_ref, kt_ref, v_ref, _), h_ref) in enumerate(((fwd_refs, hf_ref), (bwd_refs, hb_ref))):
        visible = (s_idx >= t_idx) if d == 1 else (s_idx <= t_idx)
        for hd in range(ML_HEADS):
            r = d * ML_HEADS + hd
            lo, hi = hd * ML_DK, (hd + 1) * ML_DK
            rel_row = rel_t[r:r + 1, :]
            w_col = w_inter[:, r:r + 1]
            n_old = n_sc[r:r + 1, :]
            c_old = c_sc[r]
            qh, kh, vh = q_ref[:, lo:hi], k_ref[:, lo:hi], v_ref[:, lo:hi]
            kth = kt_ref[lo:hi, :]

            a = jnp.exp(jnp.where(visible, rel_row - mm[:, r:r + 1], NEG_INF)) * _dot(qh, kth)
            num = _dot(a.astype(BF16), vh) + w_col * _dot(qh, c_old.astype(BF16))
            den = jnp.sum(a, axis=1, keepdims=True) + w_col * jnp.sum(
                qh.astype(F32) * n_old, axis=1, keepdims=True)
            h_ref[:, lo:hi] = num * (1.0 / jnp.maximum(jnp.abs(den), floor[:, r:r + 1]))

            w_row = jnp.exp(rel_row - mx[:, r:r + 1])
            dec = decay[:, r:r + 1]
            c_sc[r] = dec * c_old + _dot((kth.astype(F32) * w_row).astype(BF16), vh)
            w_rows = jnp.broadcast_to(w_row, (BF16_ROWS, L)).astype(BF16)
            n_sc[r:r + 1, :] = dec * n_old + _dot(w_rows, kh)[0:1, :]

    if emit_state:
        @pl.when(c == pl.num_programs(1) - 1)
        def _():
            co_ref[...] = c_sc[...]
            no_ref[...] = n_sc[...]
            mo_ref[...] = m_sc[...]


def _ml_scan(q, k, kt, v, gates, nb, seq, init=None, emit_state=False):
    nc = seq // ML_CHUNK
    fwd = lambda b, c: (b * nc + c, 0)
    bwd = lambda b, c: (b * nc + (nc - 1 - c), 0)
    tok = lambda im: pl.BlockSpec((ML_CHUNK, D), im)
    ktsp = lambda im: pl.BlockSpec((D, ML_CHUNK), lambda b, c: im(b, c)[::-1])
    gsp = lambda im: pl.BlockSpec((ML_CHUNK, 2 * LANE), im)
    st_c = pl.BlockSpec((None, N_STATE, ML_DK, ML_DK), lambda b, c: (b, 0, 0, 0))
    st_n = pl.BlockSpec((None, N_STATE, ML_DK), lambda b, c: (b, 0, 0))
    st_m = pl.BlockSpec((None, 1, LANE), lambda b, c: (b, 0, 0))
    in_specs = [tok(fwd), tok(fwd), ktsp(fwd), tok(fwd), gsp(fwd),
                tok(bwd), tok(bwd), ktsp(bwd), tok(bwd), gsp(bwd)]
    args = [q, k, kt, v, gates] * 2
    if init is not None:
        in_specs += [st_c, st_n, st_m]
        args += list(init)
    out_specs = [tok(fwd), tok(bwd)]
    out_shape = [jax.ShapeDtypeStruct((nb * seq, D), F32)] * 2
    if emit_state:
        out_specs += [st_c, st_n, st_m]
        out_shape += [jax.ShapeDtypeStruct((nb, N_STATE, ML_DK, ML_DK), F32),
                      jax.ShapeDtypeStruct((nb, N_STATE, ML_DK), F32),
                      jax.ShapeDtypeStruct((nb, 1, LANE), F32)]
    return pl.pallas_call(
        functools.partial(_ml_scan_kernel, with_init=init is not None, emit_state=emit_state),
        grid=(nb, nc), in_specs=in_specs, out_specs=out_specs, out_shape=out_shape,
        scratch_shapes=[pltpu.VMEM((N_STATE, ML_DK, ML_DK), F32), pltpu.VMEM((N_STATE, ML_DK), F32),
                        pltpu.VMEM((1, LANE), F32)],
        compiler_params=_params(2, last_arbitrary=True), name="mlstm_scan",
    )(*args)


def _ml_post_kernel(x_ref, hf_ref, hb_ref, og_ref, mods_ref, hg_ref, w_ref, o_ref):
    for rows in _row_groups(x_ref.shape[0]):
        hsum = hf_ref[rows, :] + hb_ref[rows, :]
        hn = jnp.concatenate(
            [_rms(hsum[:, hd * ML_DK:(hd + 1) * ML_DK]) for hd in range(ML_HEADS)], axis=1)
        y = _dot((og_ref[rows, :] * (hn * hg_ref[...])).astype(BF16), w_ref[...])
        o_ref[rows, :] = x_ref[rows, :] + _gate(mods_ref, 1) * y


def _ml_post(x, hf, hb, og, mods, head_g, w_out, layer, off, n, tm=512):
    tok = pl.BlockSpec((tm, D), lambda i: (i, 0))
    return pl.pallas_call(
        _ml_post_kernel,
        grid=(n,),
        in_specs=[_x_spec(tm, off), tok, tok, tok, _mods_spec(layer, tm, off),
                  _resident((1, D)), _resident((D, D))],
        out_specs=_x_spec(tm, off),
        out_shape=jax.ShapeDtypeStruct(x.shape, F32),
        input_output_aliases={0: 0}, compiler_params=_params(1), name="mlstm_post",
    )(x, hf, hb, og, mods, head_g.reshape(1, D), w_out)


def _fn_chan_kernel(x_ref, mods_ref, g_ref, cc_ref, sc_ref, a_ref, b_ref):
    hb = _mod_norm(x_ref[...], g_ref, mods_ref, 1).astype(BF16)
    for gi in range(FOURIER_GROUPS):
        sl = slice(gi * FG, (gi + 1) * FG)
        a_ref[:, sl] = _dot(hb[:, sl], cc_ref[...]).astype(BF16)
        b_ref[:, sl] = _dot(hb[:, sl], sc_ref[...]).astype(BF16)


def _fn_chan(x, mods, g, cc, sc, layer, off, n, tm=512):
    tok = pl.BlockSpec((tm, D), lambda i: (i, 0))
    return pl.pallas_call(
        _fn_chan_kernel,
        grid=(n,),
        in_specs=[_x_spec(tm, off), _mods_spec(layer, tm, off), _resident((1, D)),
                  _resident((FG, FG)), _resident((FG, FG))],
        out_specs=[tok, tok],
        out_shape=[jax.ShapeDtypeStruct((n * tm, D), BF16)] * 2,
        compiler_params=_params(1), name="fourier_chan",
    )(x, mods, g.reshape(1, D), cc, sc)


def _fn_seq_kernel(x_ref, a_ref, b_ref, cs_ref, ss_ref, mods_ref, w_ref, bias_ref, o_ref, *, scale):
    f = (_dot(cs_ref[...], a_ref[...]) - _dot(ss_ref[...], b_ref[...])) * scale
    y = _dot(f.astype(BF16), w_ref[...]) + bias_ref[...]
    o_ref[...] = x_ref[...] + _gate(mods_ref, 1) * y


def _fn_seq(x, a, b, cs, ss, mods, w_out, b_out, layer, off_tok, nb, seq, tr):
    nt = seq // tr
    off = off_tok // tr
    xs = pl.BlockSpec((tr, D), lambda bi, t: (off + bi * nt + t, 0))
    ab = pl.BlockSpec((seq, D), lambda bi, t: (bi, 0))
    tbl = pl.BlockSpec((tr, seq), lambda bi, t: (t, 0))
    mods_spec = pl.BlockSpec(
        (None, 9, D), lambda bi, t: (layer * N_COND + _tile_cond(off + bi * nt + t, tr), 0, 0))
    return pl.pallas_call(
        functools.partial(_fn_seq_kernel, scale=float((seq * FG) ** -0.5)),
        grid=(nb, nt),
        in_specs=[xs, ab, ab, tbl, tbl, mods_spec, _resident((D, D)), _resident((1, D))],
        out_specs=xs,
        out_shape=jax.ShapeDtypeStruct(x.shape, F32),
        input_output_aliases={0: 0}, compiler_params=_params(2), name="fourier_seq",
    )(x, a, b, cs, ss, mods, w_out, b_out.reshape(1, D))


def _dft_tables(n):
    ang = 2.0 * np.pi * ((np.arange(n)[:, None] * np.arange(n)[None, :]) % n) / n
    return (jnp.asarray(np.cos(ang), F32).astype(BF16), jnp.asarray(np.sin(ang), F32).astype(BF16))


def _gm_kernel(x_ref, mods_ref, g_ref, win_ref, bin_ref, vg_ref, ws_ref, bs_ref, wout_ref, o_ref, sv_sc):
    gw = GM_W // GM_GROUPS
    for r0 in range(0, x_ref.shape[0], 2 * GM_CHUNK):
        x = x_ref[r0:r0 + 2 * GM_CHUNK, :]
        hb = _mod_norm(x, g_ref, mods_ref, 1).astype(BF16)
        z = _dot(hb, win_ref[...]) + bin_ref[...]
        z = z * (0.5 * (1.0 + jnp.tanh(np.sqrt(2.0 / np.pi) * (z + 0.044715 * (z * z * z)))))
        u = z[:, :GM_W]
        v = (_rms(z[:, GM_W:]) * vg_ref[...]).astype(BF16)
        for ch in range(2):
            rows = slice(ch * GM_CHUNK, (ch + 1) * GM_CHUNK)
            for gi in range(GM_GROUPS):
                cols = slice(gi * gw, (gi + 1) * gw)
                sv_sc[r0 + ch * GM_CHUNK:r0 + (ch + 1) * GM_CHUNK, cols] = (
                    _dot(ws_ref[gi], v[rows, cols]) + bs_ref[:, gi:gi + 1])
        y = _dot((u * sv_sc[r0:r0 + 2 * GM_CHUNK, :]).astype(BF16), wout_ref[...])
        o_ref[r0:r0 + 2 * GM_CHUNK, :] = x + _gate(mods_ref, 1) * y


def _gmlp(x, mods, g, w_in, b_in, v_g, w_s, b_s_t, w_out, layer, tm=512):
    return pl.pallas_call(
        _gm_kernel,
        grid=(N_TOK // tm,),
        in_specs=[_x_spec(tm, 0), _mods_spec(layer, tm, 0), _resident((1, D)),
                  _resident((D, 2 * GM_W)), _resident((1, 2 * GM_W)), _resident((1, GM_W)),
                  _resident((GM_GROUPS, GM_CHUNK, GM_CHUNK)), _resident((GM_CHUNK, GM_GROUPS)),
                  _resident((GM_W, D))],
        out_specs=_x_spec(tm, 0),
        out_shape=jax.ShapeDtypeStruct(x.shape, F32),
        scratch_shapes=[pltpu.VMEM((tm, GM_W), F32)],
        input_output_aliases={0: 0}, compiler_params=_params(1), name="gmlp",
    )(x, mods, g.reshape(1, D), w_in, b_in.reshape(1, 2 * GM_W), v_g.reshape(1, GM_W), w_s, b_s_t, w_out)


def _na_pre_kernel(x_ref, mods_ref, g_ref, w_ref, q_ref, k_ref, v_ref):
    for rows in _row_groups(x_ref.shape[0]):
        hb = _mod_norm(x_ref[rows, :], g_ref, mods_ref, 1).astype(BF16)
        qkv = _dot(hb, w_ref[...])
        q_ref[rows, :] = (qkv[:, :D] * (NA_HD ** -0.5)).astype(BF16)
        k_ref[rows, :] = qkv[:, D:2 * D].astype(k_ref.dtype)
        v_ref[rows, :] = qkv[:, 2 * D:].astype(v_ref.dtype)


def _na_pre(x, mods, g, w_qkv, layer, off, n, kv_dtype, tm=512):
    tok = pl.BlockSpec((tm, D), lambda i: (i, 0))
    return pl.pallas_call(
        _na_pre_kernel,
        grid=(n,),
        in_specs=[_x_spec(tm, off), _mods_spec(layer, tm, off), _resident((1, D)), _resident((D, 3 * D))],
        out_specs=[tok, tok, tok],
        out_shape=[jax.ShapeDtypeStruct((n * tm, D), BF16),
                   jax.ShapeDtypeStruct((n * tm, D), kv_dtype),
                   jax.ShapeDtypeStruct((n * tm, D), kv_dtype)],
        compiler_params=_params(1), name="na_qkv",
    )(x, mods, g.reshape(1, D), w_qkv)


def _half_mask(shape, e):
    lane = lax.broadcasted_iota(jnp.int32, shape, len(shape) - 1)
    return (lane < NA_HD) if e == 0 else (lane >= NA_HD)


def _ctx_attn_kernel(x_ref, q_ref, k_ref, v_ref, mods_ref, w_ref, o_ref, att_sc):
    for j in range(NA_PAIRS):
        sl = slice(j * LANE, (j + 1) * LANE)
        s = _dot_nt(_pair_rows(q_ref[:, sl]), k_ref[:, sl].astype(BF16))
        p = jnp.exp(s - jnp.max(s, axis=1, keepdims=True))
        o = _dot(p.astype(BF16), v_ref[:, sl].astype(BF16)) / jnp.sum(p, axis=1, keepdims=True)
        att_sc[:, sl] = jnp.where(_half_mask((SEQ, LANE), 0), o[:SEQ], o[SEQ:]).astype(BF16)
    o_ref[...] = x_ref[...] + _gate(mods_ref, 1) * _dot(att_sc[...], w_ref[...])


def _ctx_attn(x, q, k, v, mods, w_out, layer):
    tok = pl.BlockSpec((SEQ, D), lambda b: (b, 0))
    return pl.pallas_call(
        _ctx_attn_kernel,
        grid=(BATCH,),
        in_specs=[tok, tok, tok, tok, _mods_spec(layer, SEQ, 0), _resident((D, D))],
        out_specs=tok,
        out_shape=jax.ShapeDtypeStruct(x.shape, F32),
        scratch_shapes=[pltpu.VMEM((SEQ, D), BF16)],
        input_output_aliases={0: 0}, compiler_params=_params(1), name="ctx_attn",
    )(x, q, k, v, mods, w_out)


def _na_bias_kernel(rpb_ref, o_ref, pair_sc):
    h = pl.program_id(0)
    shape = (GRID_W, LANE)
    q = lax.broadcasted_iota(jnp.int32, shape, 0)
    lane = lax.broadcasted_iota(jnp.int32, shape, 1)
    x = lane & (GRID_W - 1)
    first = lane < GRID_W
    dc = x - q + (NA_KW - 1)
    q_start = jnp.clip(q - NA_KW // 2, 0, GRID_W - NA_KW)
    in_window = (x >= q_start) & (x < q_start + NA_KW)
    n_dc = 2 * NA_KW - 1
    for dr in range(2 * NA_KH - 2):
        acc = jnp.zeros(shape, F32)
        for j in range(n_dc):
            val = jnp.where(first, rpb_ref[h, dr * n_dc + j], rpb_ref[h, (dr + 1) * n_dc + j])
            acc = jnp.where(dc == j, val, acc)
        pair_sc[dr] = jnp.where(in_window, acc, NEG_INF)
    for o in range(NA_KH):
        for t in range(NA_KH // 2):
            o_ref[o, :, t * LANE:(t + 1) * LANE] = pair_sc[NA_KH - 1 - o + 2 * t]


def _na_bias_tables(rpb):
    n_rel = (2 * NA_KH - 1) * (2 * NA_KW - 1)
    t = pl.pallas_call(
        _na_bias_kernel,
        grid=(NA_HEADS,),
        in_specs=[pl.BlockSpec(memory_space=pltpu.SMEM)],
        out_specs=pl.BlockSpec((NA_KH, None, GRID_W, NA_WIN), lambda h: (0, h, 0, 0)),
        out_shape=jax.ShapeDtypeStruct((NA_KH, NA_HEADS, GRID_W, NA_WIN), F32),
        scratch_shapes=[pltpu.VMEM((2 * NA_KH - 2, GRID_W, LANE), F32)],
        compiler_params=_params(1), name="na_bias",
    )(rpb.reshape(NA_HEADS, n_rel))
    return t.reshape(NA_KH, NA_HEADS * GRID_W, NA_WIN)


def _pair_rows(qp):
    zero = jnp.zeros_like(qp)
    return jnp.concatenate([jnp.where(_half_mask(qp.shape, 0), qp, zero),
                            jnp.where(_half_mask(qp.shape, 1), qp, zero)], axis=0)


def _na_lat_kernel(q_ref, k_ref, v_ref, kc_ref, vc_ref, bias_ref, o_ref, s_sc, p_sc):
    r = pl.program_id(1)
    start = pl.multiple_of(jnp.clip(r - NA_KH // 2, 0, NA_ROWS - NA_KH) * GRID_W, GRID_W)
    for j in range(NA_PAIRS):
        sl = slice(j * LANE, (j + 1) * LANE)
        qb = _pair_rows(q_ref[:, sl])
        s_sc[sl, :NA_WIN] = _dot_nt(qb, k_ref[pl.ds(start, NA_WIN), sl]) + bias_ref[sl, :]
        s_sc[sl, NA_WIN:] = _dot_nt(qb, kc_ref[:, sl])
    s = s_sc[...]
    p = jnp.exp(s - jnp.max(s, axis=1, keepdims=True))
    inv = 1.0 / jnp.sum(p, axis=1, keepdims=True)
    p_sc[...] = p.astype(BF16)
    for j in range(NA_PAIRS):
        sl = slice(j * LANE, (j + 1) * LANE)
        o = (_dot(p_sc[sl, :NA_WIN], v_ref[pl.ds(start, NA_WIN), sl])
             + _dot(p_sc[sl, NA_WIN:], vc_ref[:, sl])) * inv[sl]
        o_ref[:, sl] = jnp.where(_half_mask((GRID_W, LANE), 0), o[:GRID_W], o[GRID_W:])


def _na_latent(q, k, v, kc, vc, bias):
    row_class = lambda r: r - jnp.clip(r - NA_KH // 2, 0, NA_ROWS - NA_KH)
    seq_kv = pl.BlockSpec((None, DEC_SEQ, D), lambda b, r: (b, 0, 0))
    ctx_kv = pl.BlockSpec((None, PAST_LEN, D), lambda b, r: (b, 0, 0))
    n_rows = NA_HEADS * GRID_W
    return pl.pallas_call(
        _na_lat_kernel,
        grid=(DEC_BATCH, NA_ROWS),
        in_specs=[pl.BlockSpec((GRID_W, D), lambda b, r: (b * NA_ROWS + r, 0)),
                  seq_kv, seq_kv, ctx_kv, ctx_kv,
                  pl.BlockSpec((None, n_rows, NA_WIN), lambda b, r: (row_class(r), 0, 0))],
        out_specs=pl.BlockSpec((GRID_W, D), lambda b, r: (b * NA_ROWS + r, 0)),
        out_shape=jax.ShapeDtypeStruct((NS_TOK, D), F32),
        scratch_shapes=[pltpu.VMEM((n_rows, NA_WIN + PAST_LEN), F32),
                        pltpu.VMEM((n_rows, NA_WIN + PAST_LEN), BF16)],
        compiler_params=_params(2), name="na_latent",
    )(q, k.reshape(DEC_BATCH, DEC_SEQ, D), v.reshape(DEC_BATCH, DEC_SEQ, D), kc, vc, bias)


def kernel(x_prompt, x_sample, state_mlstm_C, state_mlstm_n, state_mlstm_m, cache_na_k, cache_na_v, c, c_ctx, w_ada, b_ada, norm_g, final_g, ffn_w1, ffn_w3, ffn_w2, ml_w_qkv, ml_w_if, ml_b_if, ml_w_og, ml_head_g, ml_w_out, fn_w_out, fn_b_out, gm_w_in, gm_b_in, gm_v_g, gm_w_s, gm_b_s, gm_w_out, na_w_qkv, na_w_out, na_rpb):
    tm = 512
    n_p, n_s, n_all = NP_TOK // tm, NS_TOK // tm, N_TOK // tm
    x = jnp.concatenate([x_prompt.reshape(NP_TOK, D), x_sample.reshape(NS_TOK, D)], axis=0)

    cond = jnp.zeros((N_COND, D), F32).at[0].set(c_ctx).at[1:1 + DEC_BATCH].set(c)
    mods = _adaln(cond, w_ada, b_ada).reshape(DEPTH * N_COND, 9, D)

    w1, w3, w2 = ffn_w1.astype(BF16), ffn_w3.astype(BF16), ffn_w2.astype(BF16)
    outs = {}
    for l in range(DEPTH):
        kind, j = l % 4, l // 4
        x = _ffn(x, mods, norm_g[l, 0], w1, w3, w2, l, 0, 0, N_TOK // FFN_TILE)
        g = norm_g[l, 1]
        if kind == 0:
            w_dir = jnp.transpose(ml_w_if[j], (1, 0, 2))
            lane_pad = ((0, 0), (0, LANE - N_STATE))
            wif = jnp.concatenate(
                [jnp.pad(w_dir[:, :, :ML_HEADS].reshape(D, N_STATE), lane_pad),
                 jnp.pad(w_dir[:, :, ML_HEADS:].reshape(D, N_STATE), lane_pad)], axis=1)
            bif = jnp.concatenate(
                [jnp.pad(ml_b_if[j][:, :ML_HEADS].reshape(1, N_STATE), lane_pad),
                 jnp.pad(ml_b_if[j][:, ML_HEADS:].reshape(1, N_STATE), lane_pad)], axis=1)
            wqkv, wog, wout = ml_w_qkv[j].astype(BF16), ml_w_og[j].astype(BF16), ml_w_out[j].astype(BF16)
            for off, n, nb, seq in ((0, n_p, BATCH, SEQ), (n_p, n_s, DEC_BATCH, DEC_SEQ)):
                q, k, kt, v, og, gates = _ml_pre(x, mods, g, wqkv, wog, wif.astype(BF16), bif, l, off, n)
                if off == 0:
                    hf, hb, c_new, n_new, m_new = _ml_scan(q, k, kt, v, gates, nb, seq, emit_state=True)
                    outs["C"] = c_new.reshape(BATCH, 1, 2, ML_HEADS, ML_DK, ML_DK)
                    outs["n"] = n_new.reshape(BATCH, 1, 2, ML_HEADS, ML_DK)
                    outs["m"] = m_new[:, 0, :N_STATE].reshape(BATCH, 1, 2, ML_HEADS)
                else:
                    init = (state_mlstm_C[:, j].reshape(DEC_BATCH, N_STATE, ML_DK, ML_DK),
                            state_mlstm_n[:, j].reshape(DEC_BATCH, N_STATE, ML_DK),
                            jnp.pad(state_mlstm_m[:, j].reshape(DEC_BATCH, 1, N_STATE),
                                    ((0, 0), (0, 0), (0, LANE - N_STATE))))
                    hf, hb = _ml_scan(q, k, kt, v, gates, nb, seq, init=init)
                x = _ml_post(x, hf, hb, og, mods, ml_head_g[j], wout, l, off, n)
        elif kind == 1:
            cc, sc = _dft_tables(FG)
            wout = fn_w_out[j].astype(BF16)
            for off, n, nb, seq, tr in ((0, n_p, BATCH, SEQ, SEQ), (n_p, n_s, DEC_BATCH, DEC_SEQ, 512)):
                a, b = _fn_chan(x, mods, g, cc, sc, l, off, n)
                cs, ss = _dft_tables(seq)
                x = _fn_seq(x, a, b, cs, ss, mods, wout, fn_b_out[j], l, off * tm, nb, seq, tr)
        elif kind == 2:
            x = _gmlp(x, mods, g, gm_w_in[j].astype(BF16), gm_b_in[j], gm_v_g[j],
                      gm_w_s[j].astype(BF16), gm_b_s[j].T, gm_w_out[j].astype(BF16), l)
        else:
            wqkv, wout = na_w_qkv[j].astype(BF16), na_w_out[j].astype(BF16)
            q, k, v = _na_pre(x, mods, g, wqkv, l, 0, n_p, F32)
            outs["k"] = k.reshape(BATCH, 1, SEQ, NA_HEADS, NA_HD)
            outs["v"] = v.reshape(BATCH, 1, SEQ, NA_HEADS, NA_HD)
            x = _ctx_attn(x, q, k, v, mods, wout, l)
            q, k, v = _na_pre(x, mods, g, wqkv, l, n_p, n_s, BF16)
            att = _na_latent(q, k, v,
                             cache_na_k[:, j].reshape(DEC_BATCH, PAST_LEN, D).astype(BF16),
                             cache_na_v[:, j].reshape(DEC_BATCH, PAST_LEN, D).astype(BF16),
                             _na_bias_tables(na_rpb[j]))
            x = _proj_residual(x, att, mods, wout, l, n_p, n_s)
        if l < DEPTH - 1:
            x = _ffn(x, mods, norm_g[l, 2], w1, w3, w2, l, 1, 0, N_TOK // FFN_TILE)
        else:
            y_p = _ffn(x, mods, norm_g[l, 2], w1, w3, w2, l, 1, 0, NP_TOK // FFN_TILE, final_g=final_g)
            y_s = _ffn(x, mods, norm_g[l, 2], w1, w3, w2, l, 1, NP_TOK // FFN_TILE, NS_TOK // FFN_TILE, final_g=final_g)
    return (y_p.reshape(BATCH, SEQ, D), y_s.reshape(DEC_BATCH, DEC_SEQ, D),
            outs["C"], outs["n"], outs["m"], outs["k"], outs["v"])
```

```python
import functools

import numpy as np
import jax
import jax.numpy as jnp
from jax import lax
from jax.experimental import pallas as pl
from jax.experimental.pallas import tpu as pltpu

D = 1024
BATCH = 32
SEQ = 256
DEPTH = 4
DEC_BATCH = 2
DEC_SEQ = 2048
PAST_LEN = 512
GRID_W = 64
D_FF = 2816
EPS = 1e-6

NP_TOK = BATCH * SEQ
NS_TOK = DEC_BATCH * DEC_SEQ
N_TOK = NP_TOK + NS_TOK

ML_HEADS = 4
ML_DK = D // ML_HEADS
ML_CHUNK = 128
N_STATE = 2 * ML_HEADS

FOURIER_GROUPS = 4
FG = D // FOURIER_GROUPS

GM_W = D
GM_GROUPS = 4
GM_CHUNK = 128

NA_HEADS = 16
NA_HD = D // NA_HEADS
NA_KH = 8
NA_KW = 16
NA_ROWS = DEC_SEQ // GRID_W
NA_WIN = NA_KH * GRID_W
NA_PAIRS = NA_HEADS // 2
N_COND = 8

LANE = 128
BF16_ROWS = 16
FFN_ROWS = 256
FFN_TILE = 1024
VMEM_LIMIT = 56 * 1024 * 1024

F32 = jnp.float32
BF16 = jnp.bfloat16
NEG_INF = float("-inf")


def _params(n_axes, last_arbitrary=False):
    sem = ["parallel"] * n_axes
    if last_arbitrary:
        sem[-1] = "arbitrary"
    return pltpu.CompilerParams(dimension_semantics=tuple(sem), vmem_limit_bytes=VMEM_LIMIT)


def _resident(shape):
    zeros = (0,) * len(shape)
    return pl.BlockSpec(shape, lambda *_: zeros, pipeline_mode=pl.Buffered(1))


def _tile_cond(i, tm):
    row0 = i * tm
    return jnp.where(row0 < NP_TOK, 0, 1 + (row0 - NP_TOK) // DEC_SEQ)


def _x_spec(tm, off):
    return pl.BlockSpec((tm, D), lambda i: (i + off, 0))


def _mods_spec(layer, tm, off):
    return pl.BlockSpec((None, 9, D), lambda i: (layer * N_COND + _tile_cond(i + off, tm), 0, 0))


def _dot(a, b):
    return jnp.dot(a, b, preferred_element_type=F32)


def _dot_nt(a, b):
    return lax.dot_general(a, b, (((1,), (1,)), ((), ())), preferred_element_type=F32)


def _rms(x):
    return x * lax.rsqrt(jnp.mean(x * x, axis=-1, keepdims=True) + EPS)


def _mod_norm(x, g_ref, mods_ref, idx):
    h = _rms(x) * g_ref[...]
    return h * (1.0 + mods_ref[3 * idx + 1:3 * idx + 2, :]) + mods_ref[3 * idx:3 * idx + 1, :]


def _gate(mods_ref, idx):
    return mods_ref[3 * idx + 2:3 * idx + 3, :]


def _row_groups(n_rows, group=256):
    return [slice(r0, r0 + group) for r0 in range(0, n_rows, group)]


def _adaln_kernel(c_ref, w_ref, b_ref, o_ref):
    c = c_ref[...]
    s = (c * jax.nn.sigmoid(c)).astype(BF16)
    o_ref[...] = _dot(s, w_ref[...].astype(BF16)) + b_ref[...]


def _adaln(cond, w_ada, b_ada):
    tn = 1152
    nj = 9 * D // tn
    return pl.pallas_call(
        _adaln_kernel,
        grid=(DEPTH, nj),
        in_specs=[
            pl.BlockSpec((N_COND, D), lambda l, j: (0, 0)),
            pl.BlockSpec((None, D, tn), lambda l, j: (l, 0, j)),
            pl.BlockSpec((None, 1, tn), lambda l, j: (l, 0, j)),
        ],
        out_specs=pl.BlockSpec((None, N_COND, tn), lambda l, j: (l, 0, j)),
        out_shape=jax.ShapeDtypeStruct((DEPTH, N_COND, 9 * D), F32),
        compiler_params=_params(2),
        name="adaln",
    )(cond, w_ada, b_ada.reshape(DEPTH, 1, 9 * D))


def _ffn_kernel(x_ref, mods_ref, g_ref, w1_ref, w3_ref, w2_ref, *rest, idx, final):
    o_ref = rest[-1]
    for r0 in range(0, x_ref.shape[0], FFN_ROWS):
        rows = slice(r0, r0 + FFN_ROWS)
        x = x_ref[rows, :]
        hb = _mod_norm(x, g_ref, mods_ref, idx).astype(BF16)
        a = _dot(hb, w1_ref[...])
        b = _dot(hb, w3_ref[...])
        act = (a * jax.nn.sigmoid(a) * b).astype(BF16)
        y = x + (0.5 * _gate(mods_ref, idx)) * _dot(act, w2_ref[...])
        if final:
            y = _rms(y) * rest[0][...]
        o_ref[rows, :] = y


def _ffn(x, mods, g, w1, w3, w2, layer, f, off, n, final_g=None, tm=FFN_TILE):
    final = final_g is not None
    idx = 2 * f

    def stacked(*shape):
        return pl.BlockSpec((None, None) + shape, lambda i: (layer, f, 0, 0), pipeline_mode=pl.Buffered(1))

    in_specs = [_x_spec(tm, off), _mods_spec(layer, tm, off), _resident((1, D)),
                stacked(D, D_FF), stacked(D, D_FF), stacked(D_FF, D)]
    args = [x, mods, g.reshape(1, D), w1, w3, w2]
    if final:
        in_specs.append(_resident((1, D)))
        args.append(final_g.reshape(1, D))
        out_spec = pl.BlockSpec((tm, D), lambda i: (i, 0))
        out_shape = jax.ShapeDtypeStruct((n * tm, D), F32)
        aliases = {}
    else:
        out_spec = _x_spec(tm, off)
        out_shape = jax.ShapeDtypeStruct(x.shape, F32)
        aliases = {0: 0}
    return pl.pallas_call(
        functools.partial(_ffn_kernel, idx=idx, final=final),
        grid=(n,), in_specs=in_specs, out_specs=out_spec, out_shape=out_shape,
        input_output_aliases=aliases, compiler_params=_params(1), name="ffn",
    )(*args)


def _proj_kernel(x_ref, a_ref, mods_ref, w_ref, o_ref):
    y = _dot(a_ref[...].astype(BF16), w_ref[...])
    o_ref[...] = x_ref[...] + _gate(mods_ref, 1) * y


def _proj_residual(x, a, mods, w, layer, off, n, tm=512):
    return pl.pallas_call(
        _proj_kernel,
        grid=(n,),
        in_specs=[_x_spec(tm, off), pl.BlockSpec((tm, D), lambda i: (i, 0)),
                  _mods_spec(layer, tm, off), _resident((D, D))],
        out_specs=_x_spec(tm, off),
        out_shape=jax.ShapeDtypeStruct(x.shape, F32),
        input_output_aliases={0: 0}, compiler_params=_params(1), name="proj_residual",
    )(x, a, mods, w)


def _ml_pre_kernel(x_ref, mods_ref, g_ref, wqkv_ref, wog_ref, wif_ref, bif_ref,
                   q_ref, k_ref, kt_ref, v_ref, og_ref, gates_ref):
    for rows in _row_groups(x_ref.shape[0]):
        hb = _mod_norm(x_ref[rows, :], g_ref, mods_ref, 1).astype(BF16)
        qkv = _dot(hb, wqkv_ref[...])
        q_ref[rows, :] = qkv[:, :D].astype(BF16)
        k = qkv[:, D:2 * D] * (ML_DK ** -0.5)
        k_ref[rows, :] = k.astype(BF16)
        kt_ref[:, rows] = k.T.astype(BF16)
        v_ref[rows, :] = qkv[:, 2 * D:].astype(BF16)
        og_ref[rows, :] = jax.nn.sigmoid(_dot(hb, wog_ref[...])).astype(BF16)
        gates_ref[rows, :] = _dot(hb, wif_ref[...]) + bif_ref[...]


def _ml_pre(x, mods, g, wqkv, wog, wif, bif, layer, off, n, tm=512):
    tok = pl.BlockSpec((tm, D), lambda i: (i, 0))
    return pl.pallas_call(
        _ml_pre_kernel,
        grid=(n,),
        in_specs=[_x_spec(tm, off), _mods_spec(layer, tm, off), _resident((1, D)),
                  _resident((D, 3 * D)), _resident((D, D)), _resident((D, 2 * LANE)), _resident((1, 2 * LANE))],
        out_specs=[tok, tok, pl.BlockSpec((D, tm), lambda i: (0, i)), tok, tok,
                   pl.BlockSpec((tm, 2 * LANE), lambda i: (i, 0))],
        out_shape=[jax.ShapeDtypeStruct((n * tm, D), BF16)] * 2
        + [jax.ShapeDtypeStruct((D, n * tm), BF16), jax.ShapeDtypeStruct((n * tm, D), BF16),
           jax.ShapeDtypeStruct((n * tm, D), BF16), jax.ShapeDtypeStruct((n * tm, 2 * LANE), F32)],
        compiler_params=_params(1), name="mlstm_pre",
    )(x, mods, g.reshape(1, D), wqkv, wog, wif, bif)


def _log_sigmoid(x):
    return jnp.minimum(x, 0.0) - jnp.log1p(jnp.exp(-jnp.abs(x)))


def _scan_rows(x, fwd_lanes, op, identity):
    n = x.shape[0]
    row = lax.broadcasted_iota(jnp.int32, x.shape, 0)
    sh = 1
    while sh < n:
        prev = jnp.where(row >= sh, pltpu.roll(x, sh, 0), identity)
        nxt = jnp.where(row < n - sh, pltpu.roll(x, n - sh, 0), identity)
        x = op(x, jnp.where(fwd_lanes, prev, nxt))
        sh *= 2
    return x


def _ml_scan_kernel(*refs, with_init, emit_state):
    fwd_refs, bwd_refs, refs = refs[:5], refs[5:10], refs[10:]
    if with_init:
        (c0_ref, n0_ref, m0_ref), refs = refs[:3], refs[3:]
    (hf_ref, hb_ref), refs = refs[:2], refs[2:]
    if emit_state:
        (co_ref, no_ref, mo_ref), refs = refs[:3], refs[3:]
    c_sc, n_sc, m_sc = refs
    c = pl.program_id(1)
    L = ML_CHUNK

    @pl.when(c == 0)
    def _():
        if with_init:
            c_sc[...] = c0_ref[...]
            n_sc[...] = n0_ref[...]
            m_sc[...] = m0_ref[...]
        else:
            c_sc[...] = jnp.zeros_like(c_sc)
            n_sc[...] = jnp.zeros_like(n_sc)
            m_sc[...] = jnp.zeros_like(m_sc)

    t_idx = lax.broadcasted_iota(jnp.int32, (L, L), 0)
    s_idx = lax.broadcasted_iota(jnp.int32, (L, L), 1)
    lane = lax.broadcasted_iota(jnp.int32, (L, LANE), 1)
    fwd_lanes = lane < ML_HEADS
    gf_ref, gb_ref = fwd_refs[4], bwd_refs[4]
    i_pre = jnp.where(fwd_lanes, gf_ref[:, :LANE], gb_ref[:, :LANE])
    f_pre = jnp.where(fwd_lanes, gf_ref[:, LANE:], gb_ref[:, LANE:])
    log_f = jnp.where(lane < N_STATE, _log_sigmoid(f_pre), 0.0)
    bsum = _scan_rows(log_f, fwd_lanes, jnp.add, 0.0)
    rel = i_pre - bsum
    m_old = m_sc[...]
    mm = jnp.maximum(m_old, _scan_rows(rel, fwd_lanes, jnp.maximum, NEG_INF))
    w_inter = jnp.exp(m_old - mm)
    floor = jnp.exp(-(bsum + mm))
    mx = jnp.maximum(m_old, jnp.max(rel, axis=0, keepdims=True))
    b_last = jnp.where(fwd_lanes[0:1], bsum[L - 1:L, :], bsum[0:1, :])
    decay = jnp.exp(m_old - mx)
    m_sc[...] = b_last + mx
    rel_t = rel.T

    t_idx = lax.broadcasted_iota(jnp.int32, (L, L), 0)
    s_idx = lax.broadcasted_iota(jnp.int32, (L, L), 1)
    for d, ((q_ref, k_ref, kt_ref, v_ref, _), h_ref) in enumerate(((fwd_refs, hf_ref), (bwd_refs, hb_ref))):
        visible = (s_idx >= t_idx) if d == 1 else (s_idx <= t_idx)
        for hd in range(ML_HEADS):
            r = d * ML_HEADS + hd
            lo, hi = hd * ML_DK, (hd + 1) * ML_DK
            rel_row = rel_t[r:r + 1, :]
            w_col = w_inter[:, r:r + 1]
            n_old = n_sc[r:r + 1, :]
            c_old = c_sc[r]
            qh, kh, vh = q_ref[:, lo:hi], k_ref[:, lo:hi], v_ref[:, lo:hi]
            kth = kt_ref[lo:hi, :]

            a = jnp.exp(jnp.where(visible, rel_row - mm[:, r:r + 1], NEG_INF)) * _dot(qh, kth)
            num = _dot(a.astype(BF16), vh) + w_col * _dot(qh, c_old.astype(BF16))
            den = jnp.sum(a, axis=1, keepdims=True) + w_col * jnp.sum(
                qh.astype(F32) * n_old, axis=1, keepdims=True)
            h_ref[:, lo:hi] = (num * (1.0 / jnp.maximum(jnp.abs(den), floor[:, r:r + 1]))).astype(BF16)

            w_row = jnp.exp(rel_row - mx[:, r:r + 1])
            dec = decay[:, r:r + 1]
            c_sc[r] = dec * c_old + _dot((kth.astype(F32) * w_row).astype(BF16), vh)
            w_rows = jnp.broadcast_to(w_row, (BF16_ROWS, L)).astype(BF16)
            n_sc[r:r + 1, :] = dec * n_old + _dot(w_rows, kh)[0:1, :]

    if emit_state:
        @pl.when(c == pl.num_programs(1) - 1)
        def _():
            co_ref[...] = c_sc[...]
            no_ref[...] = n_sc[...]
            mo_ref[...] = m_sc[...]


def _ml_scan(q, k, kt, v, gates, nb, seq, init=None, emit_state=False):
    nc = seq // ML_CHUNK
    fwd = lambda b, c: (b * nc + c, 0)
    bwd = lambda b, c: (b * nc + (nc - 1 - c), 0)
    tok = lambda im: pl.BlockSpec((ML_CHUNK, D), im)
    ktsp = lambda im: pl.BlockSpec((D, ML_CHUNK), lambda b, c: im(b, c)[::-1])
    gsp = lambda im: pl.BlockSpec((ML_CHUNK, 2 * LANE), im)
    st_c = pl.BlockSpec((None, N_STATE, ML_DK, ML_DK), lambda b, c: (b, 0, 0, 0))
    st_n = pl.BlockSpec((None, N_STATE, ML_DK), lambda b, c: (b, 0, 0))
    st_m = pl.BlockSpec((None, 1, LANE), lambda b, c: (b, 0, 0))
    in_specs = [tok(fwd), tok(fwd), ktsp(fwd), tok(fwd), gsp(fwd),
                tok(bwd), tok(bwd), ktsp(bwd), tok(bwd), gsp(bwd)]
    args = [q, k, kt, v, gates] * 2
    if init is not None:
        in_specs += [st_c, st_n, st_m]
        args += list(init)
    out_specs = [tok(fwd), tok(bwd)]
    out_shape = [jax.ShapeDtypeStruct((nb * seq, D), BF16)] * 2
    if emit_state:
        out_specs += [st_c, st_n, st_m]
        out_shape += [jax.ShapeDtypeStruct((nb, N_STATE, ML_DK, ML_DK), F32),
                      jax.ShapeDtypeStruct((nb, N_STATE, ML_DK), F32),
                      jax.ShapeDtypeStruct((nb, 1, LANE), F32)]
    return pl.pallas_call(
        functools.partial(_ml_scan_kernel, with_init=init is not None, emit_state=emit_state),
        grid=(nb, nc), in_specs=in_specs, out_specs=out_specs, out_shape=out_shape,
        scratch_shapes=[pltpu.VMEM((N_STATE, ML_DK, ML_DK), F32), pltpu.VMEM((N_STATE, ML_DK), F32),
                        pltpu.VMEM((1, LANE), F32)],
        compiler_params=_params(2, last_arbitrary=True), name="mlstm_scan",
    )(*args)


def _ml_post_kernel(x_ref, hf_ref, hb_ref, og_ref, mods_ref, hg_ref, w_ref, o_ref):
    for rows in _row_groups(x_ref.shape[0]):
        hsum = hf_ref[rows, :].astype(F32) + hb_ref[rows, :].astype(F32)
        hn = jnp.concatenate(
            [_rms(hsum[:, hd * ML_DK:(hd + 1) * ML_DK]) for hd in range(ML_HEADS)], axis=1)
        y = _dot((og_ref[rows, :].astype(F32) * (hn * hg_ref[...])).astype(BF16), w_ref[...])
        o_ref[rows, :] = x_ref[rows, :] + _gate(mods_ref, 1) * y


def _ml_post(x, hf, hb, og, mods, head_g, w_out, layer, off, n, tm=512):
    tok = pl.BlockSpec((tm, D), lambda i: (i, 0))
    return pl.pallas_call(
        _ml_post_kernel,
        grid=(n,),
        in_specs=[_x_spec(tm, off), tok, tok, tok, _mods_spec(layer, tm, off),
                  _resident((1, D)), _resident((D, D))],
        out_specs=_x_spec(tm, off),
        out_shape=jax.ShapeDtypeStruct(x.shape, F32),
        input_output_aliases={0: 0}, compiler_params=_params(1), name="mlstm_post",
    )(x, hf, hb, og, mods, head_g.reshape(1, D), w_out)


def _fn_chan_kernel(x_ref, mods_ref, g_ref, cc_ref, sc_ref, a_ref, b_ref):
    hb = _mod_norm(x_ref[...], g_ref, mods_ref, 1).astype(BF16)
    for gi in range(FOURIER_GROUPS):
        sl = slice(gi * FG, (gi + 1) * FG)
        a_ref[:, sl] = _dot(hb[:, sl], cc_ref[...]).astype(BF16)
        b_ref[:, sl] = _dot(hb[:, sl], sc_ref[...]).astype(BF16)


def _fn_chan(x, mods, g, cc, sc, layer, off, n, tm=512):
    tok = pl.BlockSpec((tm, D), lambda i: (i, 0))
    return pl.pallas_call(
        _fn_chan_kernel,
        grid=(n,),
        in_specs=[_x_spec(tm, off), _mods_spec(layer, tm, off), _resident((1, D)),
                  _resident((FG, FG)), _resident((FG, FG))],
        out_specs=[tok, tok],
        out_shape=[jax.ShapeDtypeStruct((n * tm, D), BF16)] * 2,
        compiler_params=_params(1), name="fourier_chan",
    )(x, mods, g.reshape(1, D), cc, sc)


def _fn_seq_kernel(x_ref, a_ref, b_ref, cs_ref, ss_ref, mods_ref, w_ref, bias_ref, o_ref, *, scale):
    f = (_dot(cs_ref[...], a_ref[...]) - _dot(ss_ref[...], b_ref[...])) * scale
    y = _dot(f.astype(BF16), w_ref[...]) + bias_ref[...]
    o_ref[...] = x_ref[...] + _gate(mods_ref, 1) * y


def _fn_seq(x, a, b, cs, ss, mods, w_out, b_out, layer, off_tok, nb, seq, tr):
    nt = seq // tr
    off = off_tok // tr
    xs = pl.BlockSpec((tr, D), lambda bi, t: (off + bi * nt + t, 0))
    ab = pl.BlockSpec((seq, D), lambda bi, t: (bi, 0))
    tbl = pl.BlockSpec((tr, seq), lambda bi, t: (t, 0))
    mods_spec = pl.BlockSpec(
        (None, 9, D), lambda bi, t: (layer * N_COND + _tile_cond(off + bi * nt + t, tr), 0, 0))
    return pl.pallas_call(
        functools.partial(_fn_seq_kernel, scale=float((seq * FG) ** -0.5)),
        grid=(nb, nt),
        in_specs=[xs, ab, ab, tbl, tbl, mods_spec, _resident((D, D)), _resident((1, D))],
        out_specs=xs,
        out_shape=jax.ShapeDtypeStruct(x.shape, F32),
        input_output_aliases={0: 0}, compiler_params=_params(2), name="fourier_seq",
    )(x, a, b, cs, ss, mods, w_out, b_out.reshape(1, D))


def _fn_fused_kernel(x_ref, mods_ref, g_ref, cc_ref, sc_ref, cs_ref, ss_ref, w_ref, bias_ref, o_ref,
                     a_sc, b_sc, *, seq, scale):
    hb = _mod_norm(x_ref[...], g_ref, mods_ref, 1).astype(BF16)
    for gi in range(FOURIER_GROUPS):
        sl = slice(gi * FG, (gi + 1) * FG)
        a_sc[:, sl] = _dot(hb[:, sl], cc_ref[...]).astype(BF16)
        b_sc[:, sl] = _dot(hb[:, sl], sc_ref[...]).astype(BF16)
    for r0 in range(0, x_ref.shape[0], seq):
        rows = slice(r0, r0 + seq)
        f = (_dot(cs_ref[...], a_sc[rows, :]) - _dot(ss_ref[...], b_sc[rows, :])) * scale
        y = _dot(f.astype(BF16), w_ref[...]) + bias_ref[...]
        o_ref[rows, :] = x_ref[rows, :] + _gate(mods_ref, 1) * y


def _fn_fused(x, mods, g, cc, sc, cs, ss, w_out, b_out, layer, off, n, seq, tm=512):
    return pl.pallas_call(
        functools.partial(_fn_fused_kernel, seq=seq, scale=float((seq * FG) ** -0.5)),
        grid=(n,),
        in_specs=[_x_spec(tm, off), _mods_spec(layer, tm, off), _resident((1, D)),
                  _resident((FG, FG)), _resident((FG, FG)), _resident((seq, seq)), _resident((seq, seq)),
                  _resident((D, D)), _resident((1, D))],
        out_specs=_x_spec(tm, off),
        out_shape=jax.ShapeDtypeStruct(x.shape, F32),
        scratch_shapes=[pltpu.VMEM((tm, D), BF16), pltpu.VMEM((tm, D), BF16)],
        input_output_aliases={0: 0}, compiler_params=_params(1), name="fourier_fused",
    )(x, mods, g.reshape(1, D), cc, sc, cs, ss, w_out, b_out.reshape(1, D))


def _dft_tables(n):
    ang = 2.0 * np.pi * ((np.arange(n)[:, None] * np.arange(n)[None, :]) % n) / n
    return (jnp.asarray(np.cos(ang), F32).astype(BF16), jnp.asarray(np.sin(ang), F32).astype(BF16))


def _gm_kernel(x_ref, mods_ref, g_ref, win_ref, bin_ref, vg_ref, ws_ref, bs_ref, wout_ref, o_ref, sv_sc):
    gw = GM_W // GM_GROUPS
    for r0 in range(0, x_ref.shape[0], 2 * GM_CHUNK):
        x = x_ref[r0:r0 + 2 * GM_CHUNK, :]
        hb = _mod_norm(x, g_ref, mods_ref, 1).astype(BF16)
        z = _dot(hb, win_ref[...]) + bin_ref[...]
        z = z * (0.5 * (1.0 + jnp.tanh(np.sqrt(2.0 / np.pi) * (z + 0.044715 * (z * z * z)))))
        u = z[:, :GM_W]
        v = (_rms(z[:, GM_W:]) * vg_ref[...]).astype(BF16)
        for ch in range(2):
            rows = slice(ch * GM_CHUNK, (ch + 1) * GM_CHUNK)
            for gi in range(GM_GROUPS):
                cols = slice(gi * gw, (gi + 1) * gw)
                sv_sc[r0 + ch * GM_CHUNK:r0 + (ch + 1) * GM_CHUNK, cols] = (
                    _dot(ws_ref[gi], v[rows, cols]) + bs_ref[:, gi:gi + 1])
        y = _dot((u * sv_sc[r0:r0 + 2 * GM_CHUNK, :]).astype(BF16), wout_ref[...])
        o_ref[r0:r0 + 2 * GM_CHUNK, :] = x + _gate(mods_ref, 1) * y


def _gmlp(x, mods, g, w_in, b_in, v_g, w_s, b_s_t, w_out, layer, tm=512):
    return pl.pallas_call(
        _gm_kernel,
        grid=(N_TOK // tm,),
        in_specs=[_x_spec(tm, 0), _mods_spec(layer, tm, 0), _resident((1, D)),
                  _resident((D, 2 * GM_W)), _resident((1, 2 * GM_W)), _resident((1, GM_W)),
                  _resident((GM_GROUPS, GM_CHUNK, GM_CHUNK)), _resident((GM_CHUNK, GM_GROUPS)),
                  _resident((GM_W, D))],
        out_specs=_x_spec(tm, 0),
        out_shape=jax.ShapeDtypeStruct(x.shape, F32),
        scratch_shapes=[pltpu.VMEM((tm, GM_W), F32)],
        input_output_aliases={0: 0}, compiler_params=_params(1), name="gmlp",
    )(x, mods, g.reshape(1, D), w_in, b_in.reshape(1, 2 * GM_W), v_g.reshape(1, GM_W), w_s, b_s_t, w_out)


def _na_pre_kernel(x_ref, mods_ref, g_ref, w_ref, q_ref, k_ref, v_ref, *f32_refs):
    for rows in _row_groups(x_ref.shape[0]):
        hb = _mod_norm(x_ref[rows, :], g_ref, mods_ref, 1).astype(BF16)
        qkv = _dot(hb, w_ref[...])
        q_ref[rows, :] = (qkv[:, :D] * (NA_HD ** -0.5)).astype(BF16)
        k, v = qkv[:, D:2 * D], qkv[:, 2 * D:]
        k_ref[rows, :] = k.astype(BF16)
        v_ref[rows, :] = v.astype(BF16)
        if f32_refs:
            f32_refs[0][rows, :] = k
            f32_refs[1][rows, :] = v


def _na_pre(x, mods, g, w_qkv, layer, off, n, emit_f32, tm=512):
    tok = pl.BlockSpec((tm, D), lambda i: (i, 0))
    out_specs = [tok, tok, tok]
    out_shape = [jax.ShapeDtypeStruct((n * tm, D), BF16)] * 3
    if emit_f32:
        out_specs += [tok, tok]
        out_shape += [jax.ShapeDtypeStruct((n * tm, D), F32)] * 2
    return pl.pallas_call(
        _na_pre_kernel,
        grid=(n,),
        in_specs=[_x_spec(tm, off), _mods_spec(layer, tm, off), _resident((1, D)), _resident((D, 3 * D))],
        out_specs=out_specs, out_shape=out_shape,
        compiler_params=_params(1), name="na_qkv",
    )(x, mods, g.reshape(1, D), w_qkv)


def _half_mask(shape, e):
    lane = lax.broadcasted_iota(jnp.int32, shape, len(shape) - 1)
    return (lane < NA_HD) if e == 0 else (lane >= NA_HD)


def _ctx_attn_kernel(x_ref, q_ref, k_ref, v_ref, mods_ref, w_ref, o_ref, att_sc):
    for j in range(NA_PAIRS):
        sl = slice(j * LANE, (j + 1) * LANE)
        s = _dot_nt(_pair_rows(q_ref[:, sl]), k_ref[:, sl])
        p = jnp.exp(s - jnp.max(s, axis=1, keepdims=True))
        o = _dot(p.astype(BF16), v_ref[:, sl]) / jnp.sum(p, axis=1, keepdims=True)
        att_sc[:, sl] = jnp.where(_half_mask((SEQ, LANE), 0), o[:SEQ], o[SEQ:]).astype(BF16)
    o_ref[...] = x_ref[...] + _gate(mods_ref, 1) * _dot(att_sc[...], w_ref[...])


def _ctx_attn(x, q, k, v, mods, w_out, layer):
    tok = pl.BlockSpec((SEQ, D), lambda b: (b, 0))
    return pl.pallas_call(
        _ctx_attn_kernel,
        grid=(BATCH,),
        in_specs=[tok, tok, tok, tok, _mods_spec(layer, SEQ, 0), _resident((D, D))],
        out_specs=tok,
        out_shape=jax.ShapeDtypeStruct(x.shape, F32),
        scratch_shapes=[pltpu.VMEM((SEQ, D), BF16)],
        input_output_aliases={0: 0}, compiler_params=_params(1), name="ctx_attn",
    )(x, q, k, v, mods, w_out)


def _na_bias_kernel(rpb_ref, o_ref, pair_sc):
    h = pl.program_id(0)
    shape = (GRID_W, LANE)
    q = lax.broadcasted_iota(jnp.int32, shape, 0)
    lane = lax.broadcasted_iota(jnp.int32, shape, 1)
    x = lane & (GRID_W - 1)
    first = lane < GRID_W
    dc = x - q + (NA_KW - 1)
    q_start = jnp.clip(q - NA_KW // 2, 0, GRID_W - NA_KW)
    in_window = (x >= q_start) & (x < q_start + NA_KW)
    n_dc = 2 * NA_KW - 1
    for dr in range(2 * NA_KH - 2):
        acc = jnp.zeros(shape, F32)
        for j in range(n_dc):
            val = jnp.where(first, rpb_ref[h, dr * n_dc + j], rpb_ref[h, (dr + 1) * n_dc + j])
            acc = jnp.where(dc == j, val, acc)
        pair_sc[dr] = jnp.where(in_window, acc, NEG_INF)
    for o in range(NA_KH):
        for t in range(NA_KH // 2):
            o_ref[o, :, t * LANE:(t + 1) * LANE] = pair_sc[NA_KH - 1 - o + 2 * t]


def _na_bias_tables(rpb):
    n_rel = (2 * NA_KH - 1) * (2 * NA_KW - 1)
    t = pl.pallas_call(
        _na_bias_kernel,
        grid=(NA_HEADS,),
        in_specs=[pl.BlockSpec(memory_space=pltpu.SMEM)],
        out_specs=pl.BlockSpec((NA_KH, None, GRID_W, NA_WIN), lambda h: (0, h, 0, 0)),
        out_shape=jax.ShapeDtypeStruct((NA_KH, NA_HEADS, GRID_W, NA_WIN), F32),
        scratch_shapes=[pltpu.VMEM((2 * NA_KH - 2, GRID_W, LANE), F32)],
        compiler_params=_params(1), name="na_bias",
    )(rpb.reshape(NA_HEADS, n_rel))
    return t.reshape(NA_KH, NA_HEADS * GRID_W, NA_WIN)


def _pair_rows(qp):
    zero = jnp.zeros_like(qp)
    return jnp.concatenate([jnp.where(_half_mask(qp.shape, 0), qp, zero),
                            jnp.where(_half_mask(qp.shape, 1), qp, zero)], axis=0)


def _na_lat_kernel(q_ref, k_ref, v_ref, kc_ref, vc_ref, bias_ref, o_ref, p_sc):
    r = pl.program_id(1)
    start = pl.multiple_of(jnp.clip(r - NA_KH // 2, 0, NA_ROWS - NA_KH) * GRID_W, GRID_W)
    inv = []
    for j in range(NA_PAIRS):
        sl = slice(j * LANE, (j + 1) * LANE)
        qb = _pair_rows(q_ref[:, sl])
        s_w = _dot_nt(qb, k_ref[pl.ds(start, NA_WIN), sl]) + bias_ref[sl, :]
        s_c = _dot_nt(qb, kc_ref[:, sl])
        m = jnp.maximum(jnp.max(s_w, axis=1, keepdims=True), jnp.max(s_c, axis=1, keepdims=True))
        p_w = jnp.exp(s_w - m)
        p_c = jnp.exp(s_c - m)
        inv.append(1.0 / (jnp.sum(p_w, axis=1, keepdims=True) + jnp.sum(p_c, axis=1, keepdims=True)))
        p_sc[sl, :NA_WIN] = p_w.astype(BF16)
        p_sc[sl, NA_WIN:] = p_c.astype(BF16)
    for j in range(NA_PAIRS):
        sl = slice(j * LANE, (j + 1) * LANE)
        o = (_dot(p_sc[sl, :NA_WIN], v_ref[pl.ds(start, NA_WIN), sl])
             + _dot(p_sc[sl, NA_WIN:], vc_ref[:, sl])) * inv[j]
        o_ref[:, sl] = jnp.where(_half_mask((GRID_W, LANE), 0), o[:GRID_W], o[GRID_W:]).astype(BF16)


def _na_latent(q, k, v, kc, vc, bias):
    row_class = lambda r: r - jnp.clip(r - NA_KH // 2, 0, NA_ROWS - NA_KH)
    seq_kv = pl.BlockSpec((None, DEC_SEQ, D), lambda b, r: (b, 0, 0))
    ctx_kv = pl.BlockSpec((None, PAST_LEN, D), lambda b, r: (b, 0, 0))
    n_rows = NA_HEADS * GRID_W
    return pl.pallas_call(
        _na_lat_kernel,
        grid=(DEC_BATCH, NA_ROWS),
        in_specs=[pl.BlockSpec((GRID_W, D), lambda b, r: (b * NA_ROWS + r, 0)),
                  seq_kv, seq_kv, ctx_kv, ctx_kv,
                  pl.BlockSpec((None, n_rows, NA_WIN), lambda b, r: (row_class(r), 0, 0))],
        out_specs=pl.BlockSpec((GRID_W, D), lambda b, r: (b * NA_ROWS + r, 0)),
        out_shape=jax.ShapeDtypeStruct((NS_TOK, D), BF16),
        scratch_shapes=[pltpu.VMEM((n_rows, NA_WIN + PAST_LEN), BF16)],
        compiler_params=_params(2), name="na_latent",
    )(q, k.reshape(DEC_BATCH, DEC_SEQ, D), v.reshape(DEC_BATCH, DEC_SEQ, D), kc, vc, bias)


def kernel(x_prompt, x_sample, state_mlstm_C, state_mlstm_n, state_mlstm_m, cache_na_k, cache_na_v, c, c_ctx, w_ada, b_ada, norm_g, final_g, ffn_w1, ffn_w3, ffn_w2, ml_w_qkv, ml_w_if, ml_b_if, ml_w_og, ml_head_g, ml_w_out, fn_w_out, fn_b_out, gm_w_in, gm_b_in, gm_v_g, gm_w_s, gm_b_s, gm_w_out, na_w_qkv, na_w_out, na_rpb):
    tm = 512
    n_p, n_s, n_all = NP_TOK // tm, NS_TOK // tm, N_TOK // tm
    x = jnp.concatenate([x_prompt.reshape(NP_TOK, D), x_sample.reshape(NS_TOK, D)], axis=0)

    cond = jnp.zeros((N_COND, D), F32).at[0].set(c_ctx).at[1:1 + DEC_BATCH].set(c)
    mods = _adaln(cond, w_ada, b_ada).reshape(DEPTH * N_COND, 9, D)

    w1, w3, w2 = ffn_w1.astype(BF16), ffn_w3.astype(BF16), ffn_w2.astype(BF16)
    outs = {}
    for l in range(DEPTH):
        kind, j = l % 4, l // 4
        x = _ffn(x, mods, norm_g[l, 0], w1, w3, w2, l, 0, 0, N_TOK // FFN_TILE)
        g = norm_g[l, 1]
        if kind == 0:
            w_dir = jnp.transpose(ml_w_if[j], (1, 0, 2))
            lane_pad = ((0, 0), (0, LANE - N_STATE))
            wif = jnp.concatenate(
                [jnp.pad(w_dir[:, :, :ML_HEADS].reshape(D, N_STATE), lane_pad),
                 jnp.pad(w_dir[:, :, ML_HEADS:].reshape(D, N_STATE), lane_pad)], axis=1)
            bif = jnp.concatenate(
                [jnp.pad(ml_b_if[j][:, :ML_HEADS].reshape(1, N_STATE), lane_pad),
                 jnp.pad(ml_b_if[j][:, ML_HEADS:].reshape(1, N_STATE), lane_pad)], axis=1)
            wqkv, wog, wout = ml_w_qkv[j].astype(BF16), ml_w_og[j].astype(BF16), ml_w_out[j].astype(BF16)
            for off, n, nb, seq in ((0, n_p, BATCH, SEQ), (n_p, n_s, DEC_BATCH, DEC_SEQ)):
                q, k, kt, v, og, gates = _ml_pre(x, mods, g, wqkv, wog, wif.astype(BF16), bif, l, off, n)
                if off == 0:
                    hf, hb, c_new, n_new, m_new = _ml_scan(q, k, kt, v, gates, nb, seq, emit_state=True)
                    outs["C"] = c_new.reshape(BATCH, 1, 2, ML_HEADS, ML_DK, ML_DK)
                    outs["n"] = n_new.reshape(BATCH, 1, 2, ML_HEADS, ML_DK)
                    outs["m"] = m_new[:, 0, :N_STATE].reshape(BATCH, 1, 2, ML_HEADS)
                else:
                    init = (state_mlstm_C[:, j].reshape(DEC_BATCH, N_STATE, ML_DK, ML_DK),
                            state_mlstm_n[:, j].reshape(DEC_BATCH, N_STATE, ML_DK),
                            jnp.pad(state_mlstm_m[:, j].reshape(DEC_BATCH, 1, N_STATE),
                                    ((0, 0), (0, 0), (0, LANE - N_STATE))))
                    hf, hb = _ml_scan(q, k, kt, v, gates, nb, seq, init=init)
                x = _ml_post(x, hf, hb, og, mods, ml_head_g[j], wout, l, off, n)
        elif kind == 1:
            cc, sc = _dft_tables(FG)
            wout = fn_w_out[j].astype(BF16)
            cs, ss = _dft_tables(SEQ)
            x = _fn_fused(x, mods, g, cc, sc, cs, ss, wout, fn_b_out[j], l, 0, n_p, SEQ)
            a, b = _fn_chan(x, mods, g, cc, sc, l, n_p, n_s)
            cs, ss = _dft_tables(DEC_SEQ)
            x = _fn_seq(x, a, b, cs, ss, mods, wout, fn_b_out[j], l, NP_TOK, DEC_BATCH, DEC_SEQ, 512)
        elif kind == 2:
            x = _gmlp(x, mods, g, gm_w_in[j].astype(BF16), gm_b_in[j], gm_v_g[j],
                      gm_w_s[j].astype(BF16), gm_b_s[j].T, gm_w_out[j].astype(BF16), l)
        else:
            wqkv, wout = na_w_qkv[j].astype(BF16), na_w_out[j].astype(BF16)
            q, k, v, k_heads, v_heads = _na_pre(x, mods, g, wqkv, l, 0, n_p, True)
            outs["k"] = k_heads.reshape(BATCH, 1, SEQ, NA_HEADS, NA_HD)
            outs["v"] = v_heads.reshape(BATCH, 1, SEQ, NA_HEADS, NA_HD)
            x = _ctx_attn(x, q, k, v, mods, wout, l)
            q, k, v = _na_pre(x, mods, g, wqkv, l, n_p, n_s, False)
            att = _na_latent(q, k, v,
                             cache_na_k[:, j].reshape(DEC_BATCH, PAST_LEN, D).astype(BF16),
                             cache_na_v[:, j].reshape(DEC_BATCH, PAST_LEN, D).astype(BF16),
                             _na_bias_tables(na_rpb[j]))
            x = _proj_residual(x, att, mods, wout, l, n_p, n_s)
        if l < DEPTH - 1:
            x = _ffn(x, mods, norm_g[l, 2], w1, w3, w2, l, 1, 0, N_TOK // FFN_TILE)
        else:
            y_p = _ffn(x, mods, norm_g[l, 2], w1, w3, w2, l, 1, 0, NP_TOK // FFN_TILE, final_g=final_g)
            y_s = _ffn(x, mods, norm_g[l, 2], w1, w3, w2, l, 1, NP_TOK // FFN_TILE, NS_TOK // FFN_TILE, final_g=final_g)
    return (y_p.reshape(BATCH, SEQ, D), y_s.reshape(DEC_BATCH, DEC_SEQ, D),
            outs["C"], outs["n"], outs["m"], outs["k"], outs["v"])
```

```python
import functools

import numpy as np
import jax
import jax.numpy as jnp
from jax import lax
from jax.experimental import pallas as pl
from jax.experimental.pallas import tpu as pltpu

D = 1024
BATCH = 32
SEQ = 256
DEPTH = 4
DEC_BATCH = 2
DEC_SEQ = 2048
PAST_LEN = 512
GRID_W = 64
D_FF = 2816
EPS = 1e-6

NP_TOK = BATCH * SEQ
NS_TOK = DEC_BATCH * DEC_SEQ
N_TOK = NP_TOK + NS_TOK

ML_HEADS = 4
ML_DK = D // ML_HEADS
ML_CHUNK = 128
N_STATE = 2 * ML_HEADS

FOURIER_GROUPS = 4
FG = D // FOURIER_GROUPS

GM_W = D
GM_GROUPS = 4
GM_CHUNK = 128

NA_HEADS = 16
NA_HD = D // NA_HEADS
NA_KH = 8
NA_KW = 16
NA_ROWS = DEC_SEQ // GRID_W
NA_WIN = NA_KH * GRID_W
NA_PAIRS = NA_HEADS // 2
N_COND = 8

LANE = 128
BF16_ROWS = 16
FFN_ROWS = 256
FFN_TILE = 1024
FFN_CAST_CHUNKS = 8
VMEM_LIMIT = 60 * 1024 * 1024

F32 = jnp.float32
BF16 = jnp.bfloat16
NEG_INF = float("-inf")


def _params(n_axes, last_arbitrary=False):
    sem = ["parallel"] * n_axes
    if last_arbitrary:
        sem[-1] = "arbitrary"
    return pltpu.CompilerParams(dimension_semantics=tuple(sem), vmem_limit_bytes=VMEM_LIMIT)


def _resident(shape):
    zeros = (0,) * len(shape)
    return pl.BlockSpec(shape, lambda *_: zeros, pipeline_mode=pl.Buffered(1))


def _tile_cond(i, tm):
    row0 = i * tm
    return jnp.where(row0 < NP_TOK, 0, 1 + (row0 - NP_TOK) // DEC_SEQ)


def _x_spec(tm, off):
    return pl.BlockSpec((tm, D), lambda i: (i + off, 0))


def _mods_spec(layer, tm, off):
    return pl.BlockSpec((None, 9, D), lambda i: (layer * N_COND + _tile_cond(i + off, tm), 0, 0))


def _dot(a, b):
    return jnp.dot(a, b, preferred_element_type=F32)


def _dot_nt(a, b):
    return lax.dot_general(a, b, (((1,), (1,)), ((), ())), preferred_element_type=F32)


def _rms(x):
    return x * lax.rsqrt(jnp.mean(x * x, axis=-1, keepdims=True) + EPS)


def _mod_norm(x, g_ref, mods_ref, idx):
    h = _rms(x) * g_ref[...]
    return h * (1.0 + mods_ref[3 * idx + 1:3 * idx + 2, :]) + mods_ref[3 * idx:3 * idx + 1, :]


def _gate(mods_ref, idx):
    return mods_ref[3 * idx + 2:3 * idx + 3, :]


def _row_groups(n_rows, group=256):
    return [slice(r0, r0 + group) for r0 in range(0, n_rows, group)]


def _adaln_kernel(c_ref, w_ref, b_ref, o_ref):
    c = c_ref[...]
    s = (c * jax.nn.sigmoid(c)).astype(BF16)
    o_ref[...] = _dot(s, w_ref[...].astype(BF16)) + b_ref[...]


def _adaln(cond, w_ada, b_ada):
    tn = 1152
    nj = 9 * D // tn
    return pl.pallas_call(
        _adaln_kernel,
        grid=(DEPTH, nj),
        in_specs=[
            pl.BlockSpec((N_COND, D), lambda l, j: (0, 0)),
            pl.BlockSpec((None, D, tn), lambda l, j: (l, 0, j)),
            pl.BlockSpec((None, 1, tn), lambda l, j: (l, 0, j)),
        ],
        out_specs=pl.BlockSpec((None, N_COND, tn), lambda l, j: (l, 0, j)),
        out_shape=jax.ShapeDtypeStruct((DEPTH, N_COND, 9 * D), F32),
        compiler_params=_params(2),
        name="adaln",
    )(cond, w_ada, b_ada.reshape(DEPTH, 1, 9 * D))


def _ffn_kernel(*refs, idx, final, convert_next, split_at):
    if split_at is None:
        x_ref, refs = refs[0], refs[1:]
        load_x = lambda rows: x_ref[rows, :]
    else:
        (xa_ref, xb_ref), refs = refs[:2], refs[2:]
        first = pl.program_id(0) < split_at
        load_x = lambda rows: jnp.where(first, xa_ref[rows, :], xb_ref[rows, :])
    mods_ref, g_ref, w1_ref, w3_ref, w2_ref = refs[:5]
    rest = refs[5:]
    if convert_next:
        (nw1_ref, nw3_ref, nw2_ref), rest = rest[:3], rest[3:]
        cast_out, rest = rest[-3:], rest[:-3]
    o_ref = rest[-1]
    for rows in _row_groups(o_ref.shape[0], FFN_ROWS):
        x = load_x(rows)
        hb = _mod_norm(x, g_ref, mods_ref, idx).astype(BF16)
        a = _dot(hb, w1_ref[...])
        b = _dot(hb, w3_ref[...])
        act = (a * jax.nn.sigmoid(a) * b).astype(BF16)
        y = x + (0.5 * _gate(mods_ref, idx)) * _dot(act, w2_ref[...])
        if final:
            y = _rms(y) * rest[0][...]
        o_ref[rows, :] = y
    if convert_next:
        for src, dst in zip((nw1_ref, nw3_ref, nw2_ref), cast_out):
            dst[...] = src[...].astype(BF16)


def _ffn(x, mods, g, wb, layer, f, off, n, final_g=None, next_f32=None, tm=FFN_TILE):
    final = final_g is not None
    convert_next = next_f32 is not None
    idx = 2 * f
    split_at = None
    if isinstance(x, tuple):
        xa, xb = x
        split_at = xa.shape[0] // tm
        assert not final and off == 0 and xa.shape[0] % tm == 0 and n * tm == xa.shape[0] + xb.shape[0]
        x_specs = [pl.BlockSpec((tm, D), lambda i: (jnp.minimum(i, split_at - 1), 0)),
                   pl.BlockSpec((tm, D), lambda i: (jnp.maximum(i - split_at, 0), 0))]
        x_args = [xa, xb]
    else:
        x_specs, x_args = [_x_spec(tm, off)], [x]
    in_specs = x_specs + [_mods_spec(layer, tm, off), _resident((1, D)),
                          _resident((D, D_FF)), _resident((D, D_FF)), _resident((D_FF, D))]
    args = x_args + [mods, g.reshape(1, D), *wb]
    if convert_next:
        nw1, nw3, nw2, nl, nf = next_f32
        chunk = lambda i: jnp.minimum(i, FFN_CAST_CHUNKS - 1)
        for w in (nw1, nw3, nw2):
            rows, cols = w.shape[2] // FFN_CAST_CHUNKS, w.shape[3]
            in_specs.append(pl.BlockSpec((None, None, rows, cols), lambda i: (nl, nf, chunk(i), 0)))
            args.append(w)
    if final:
        in_specs.append(_resident((1, D)))
        args.append(final_g.reshape(1, D))
        out_specs = [pl.BlockSpec((tm, D), lambda i: (i, 0))]
        out_shape = [jax.ShapeDtypeStruct((n * tm, D), F32)]
        aliases = {}
    elif split_at is not None:
        out_specs = [_x_spec(tm, 0)]
        out_shape = [jax.ShapeDtypeStruct((n * tm, D), F32)]
        aliases = {}
    else:
        out_specs = [_x_spec(tm, off)]
        out_shape = [jax.ShapeDtypeStruct(x.shape, F32)]
        aliases = {0: 0}
    if convert_next:
        assert n >= FFN_CAST_CHUNKS
        for w in next_f32[:3]:
            rows, cols = w.shape[2] // FFN_CAST_CHUNKS, w.shape[3]
            out_specs.append(pl.BlockSpec((rows, cols), lambda i: (chunk(i), 0)))
            out_shape.append(jax.ShapeDtypeStruct(w.shape[2:], BF16))
    outs = pl.pallas_call(
        functools.partial(_ffn_kernel, idx=idx, final=final, convert_next=convert_next, split_at=split_at),
        grid=(n,), in_specs=in_specs, out_specs=out_specs, out_shape=out_shape,
        input_output_aliases=aliases, compiler_params=_params(1, last_arbitrary=True), name="ffn",
    )(*args)
    return (outs[0], tuple(outs[1:])) if convert_next else outs[0]


def _proj_kernel(x_ref, a_ref, mods_ref, w_ref, o_ref):
    y = _dot(a_ref[...].astype(BF16), w_ref[...])
    o_ref[...] = x_ref[...] + _gate(mods_ref, 1) * y


def _proj_residual(x, a, mods, w, layer, off, n, tm=512):
    return pl.pallas_call(
        _proj_kernel,
        grid=(n,),
        in_specs=[_x_spec(tm, off), pl.BlockSpec((tm, D), lambda i: (i, 0)),
                  _mods_spec(layer, tm, off), _resident((D, D))],
        out_specs=_x_spec(tm, off),
        out_shape=jax.ShapeDtypeStruct(x.shape, F32),
        input_output_aliases={0: 0}, compiler_params=_params(1), name="proj_residual",
    )(x, a, mods, w)


def _ml_pre_kernel(x_ref, mods_ref, g_ref, wqkv_ref, wog_ref, wif_ref, bif_ref,
                   q_ref, k_ref, kt_ref, v_ref, og_ref, gates_ref):
    for rows in _row_groups(x_ref.shape[0]):
        hb = _mod_norm(x_ref[rows, :], g_ref, mods_ref, 1).astype(BF16)
        qkv = _dot(hb, wqkv_ref[...])
        q_ref[rows, :] = qkv[:, :D].astype(BF16)
        k = qkv[:, D:2 * D] * (ML_DK ** -0.5)
        k_ref[rows, :] = k.astype(BF16)
        kt_ref[:, rows] = k.T.astype(BF16)
        v_ref[rows, :] = qkv[:, 2 * D:].astype(BF16)
        og_ref[rows, :] = jax.nn.sigmoid(_dot(hb, wog_ref[...])).astype(BF16)
        gates_ref[rows, :] = _dot(hb, wif_ref[...]) + bif_ref[...]


def _ml_pre(x, mods, g, wqkv, wog, wif, bif, layer, off, n, tm=512):
    tok = pl.BlockSpec((tm, D), lambda i: (i, 0))
    return pl.pallas_call(
        _ml_pre_kernel,
        grid=(n,),
        in_specs=[_x_spec(tm, off), _mods_spec(layer, tm, off), _resident((1, D)),
                  _resident((D, 3 * D)), _resident((D, D)), _resident((D, 2 * LANE)), _resident((1, 2 * LANE))],
        out_specs=[tok, tok, pl.BlockSpec((D, tm), lambda i: (0, i)), tok, tok,
                   pl.BlockSpec((tm, 2 * LANE), lambda i: (i, 0))],
        out_shape=[jax.ShapeDtypeStruct((n * tm, D), BF16)] * 2
        + [jax.ShapeDtypeStruct((D, n * tm), BF16), jax.ShapeDtypeStruct((n * tm, D), BF16),
           jax.ShapeDtypeStruct((n * tm, D), BF16), jax.ShapeDtypeStruct((n * tm, 2 * LANE), F32)],
        compiler_params=_params(1), name="mlstm_pre",
    )(x, mods, g.reshape(1, D), wqkv, wog, wif, bif)


def _log_sigmoid(x):
    return jnp.minimum(x, 0.0) - jnp.log1p(jnp.exp(-jnp.abs(x)))


def _scan_rows(x, fwd_lanes, op, identity):
    n = x.shape[0]
    row = lax.broadcasted_iota(jnp.int32, x.shape, 0)
    sh = 1
    while sh < n:
        prev = jnp.where(row >= sh, pltpu.roll(x, sh, 0), identity)
        nxt = jnp.where(row < n - sh, pltpu.roll(x, n - sh, 0), identity)
        x = op(x, jnp.where(fwd_lanes, prev, nxt))
        sh *= 2
    return x


def _ml_scan_kernel(*refs, with_init, emit_state):
    fwd_refs, bwd_refs, refs = refs[:5], refs[5:10], refs[10:]
    if with_init:
        (c0_ref, n0_ref, m0_ref), refs = refs[:3], refs[3:]
    (hf_ref, hb_ref), refs = refs[:2], refs[2:]
    if emit_state:
        (co_ref, no_ref, mo_ref), refs = refs[:3], refs[3:]
    c_sc, n_sc, m_sc = refs
    c = pl.program_id(1)
    L = ML_CHUNK

    @pl.when(c == 0)
    def _():
        if with_init:
            c_sc[...] = c0_ref[...]
            n_sc[...] = n0_ref[...]
            m_sc[...] = m0_ref[...]
        else:
            c_sc[...] = jnp.zeros_like(c_sc)
            n_sc[...] = jnp.zeros_like(n_sc)
            m_sc[...] = jnp.zeros_like(m_sc)

    t_idx = lax.broadcasted_iota(jnp.int32, (L, L), 0)
    s_idx = lax.broadcasted_iota(jnp.int32, (L, L), 1)
    lane = lax.broadcasted_iota(jnp.int32, (L, LANE), 1)
    fwd_lanes = lane < ML_HEADS
    gf_ref, gb_ref = fwd_refs[4], bwd_refs[4]
    i_pre = jnp.where(fwd_lanes, gf_ref[:, :LANE], gb_ref[:, :LANE])
    f_pre = jnp.where(fwd_lanes, gf_ref[:, LANE:], gb_ref[:, LANE:])
    log_f = jnp.where(lane < N_STATE, _log_sigmoid(f_pre), 0.0)
    bsum = _scan_rows(log_f, fwd_lanes, jnp.add, 0.0)
    rel = i_pre - bsum
    m_old = m_sc[...]
    mm = jnp.maximum(m_old, _scan_rows(rel, fwd_lanes, jnp.maximum, NEG_INF))
    w_inter = jnp.exp(m_old - mm)
    floor = jnp.exp(-(bsum + mm))
    mx = jnp.maximum(m_old, jnp.max(rel, axis=0, keepdims=True))
    b_last = jnp.where(fwd_lanes[0:1], bsum[L - 1:L, :], bsum[0:1, :])
    decay = jnp.exp(m_old - mx)
    m_sc[...] = b_last + mx
    rel_t = rel.T

    t_idx = lax.broadcasted_iota(jnp.int32, (L, L), 0)
    s_idx = lax.broadcasted_iota(jnp.int32, (L, L), 1)
    for d, ((q_ref, k_ref, kt_ref, v_ref, _), h_ref) in enumerate(((fwd_refs, hf_ref), (bwd_refs, hb_ref))):
        visible = (s_idx >= t_idx) if d == 1 else (s_idx <= t_idx)
        for hd in range(ML_HEADS):
            r = d * ML_HEADS + hd
            lo, hi = hd * ML_DK, (hd + 1) * ML_DK
            rel_row = rel_t[r:r + 1, :]
            w_col = w_inter[:, r:r + 1]
            n_old = n_sc[r:r + 1, :]
            c_old = c_sc[r]
            qh, kh, vh = q_ref[:, lo:hi], k_ref[:, lo:hi], v_ref[:, lo:hi]
            kth = kt_ref[lo:hi, :]

            a = jnp.exp(jnp.where(visible, rel_row - mm[:, r:r + 1], NEG_INF)) * _dot(qh, kth)
            num = _dot(a.astype(BF16), vh) + w_col * _dot(qh, c_old.astype(BF16))
            den = jnp.sum(a, axis=1, keepdims=True) + w_col * jnp.sum(
                qh.astype(F32) * n_old, axis=1, keepdims=True)
            h_ref[:, lo:hi] = (num * (1.0 / jnp.maximum(jnp.abs(den), floor[:, r:r + 1]))).astype(BF16)

            w_row = jnp.exp(rel_row - mx[:, r:r + 1])
            dec = decay[:, r:r + 1]
            c_sc[r] = dec * c_old + _dot((kth.astype(F32) * w_row).astype(BF16), vh)
            w_rows = jnp.broadcast_to(w_row, (BF16_ROWS, L)).astype(BF16)
            n_sc[r:r + 1, :] = dec * n_old + _dot(w_rows, kh)[0:1, :]

    if emit_state:
        @pl.when(c == pl.num_programs(1) - 1)
        def _():
            co_ref[...] = c_sc[...]
            no_ref[...] = n_sc[...]
            mo_ref[...] = m_sc[...]


def _ml_scan(q, k, kt, v, gates, nb, seq, init=None, emit_state=False):
    nc = seq // ML_CHUNK
    fwd = lambda b, c: (b * nc + c, 0)
    bwd = lambda b, c: (b * nc + (nc - 1 - c), 0)
    tok = lambda im: pl.BlockSpec((ML_CHUNK, D), im)
    ktsp = lambda im: pl.BlockSpec((D, ML_CHUNK), lambda b, c: im(b, c)[::-1])
    gsp = lambda im: pl.BlockSpec((ML_CHUNK, 2 * LANE), im)
    st_c = pl.BlockSpec((None, N_STATE, ML_DK, ML_DK), lambda b, c: (b, 0, 0, 0))
    st_n = pl.BlockSpec((None, N_STATE, ML_DK), lambda b, c: (b, 0, 0))
    st_m = pl.BlockSpec((None, 1, LANE), lambda b, c: (b, 0, 0))
    in_specs = [tok(fwd), tok(fwd), ktsp(fwd), tok(fwd), gsp(fwd),
                tok(bwd), tok(bwd), ktsp(bwd), tok(bwd), gsp(bwd)]
    args = [q, k, kt, v, gates] * 2
    if init is not None:
        in_specs += [st_c, st_n, st_m]
        args += list(init)
    out_specs = [tok(fwd), tok(bwd)]
    out_shape = [jax.ShapeDtypeStruct((nb * seq, D), BF16)] * 2
    if emit_state:
        out_specs += [st_c, st_n, st_m]
        out_shape += [jax.ShapeDtypeStruct((nb, N_STATE, ML_DK, ML_DK), F32),
                      jax.ShapeDtypeStruct((nb, N_STATE, ML_DK), F32),
                      jax.ShapeDtypeStruct((nb, 1, LANE), F32)]
    return pl.pallas_call(
        functools.partial(_ml_scan_kernel, with_init=init is not None, emit_state=emit_state),
        grid=(nb, nc), in_specs=in_specs, out_specs=out_specs, out_shape=out_shape,
        scratch_shapes=[pltpu.VMEM((N_STATE, ML_DK, ML_DK), F32), pltpu.VMEM((N_STATE, ML_DK), F32),
                        pltpu.VMEM((1, LANE), F32)],
        compiler_params=_params(2, last_arbitrary=True), name="mlstm_scan",
    )(*args)


def _ml_post_kernel(x_ref, hf_ref, hb_ref, og_ref, mods_ref, hg_ref, w_ref, o_ref):
    for rows in _row_groups(x_ref.shape[0]):
        hsum = hf_ref[rows, :].astype(F32) + hb_ref[rows, :].astype(F32)
        hn = jnp.concatenate(
            [_rms(hsum[:, hd * ML_DK:(hd + 1) * ML_DK]) for hd in range(ML_HEADS)], axis=1)
        y = _dot((og_ref[rows, :].astype(F32) * (hn * hg_ref[...])).astype(BF16), w_ref[...])
        o_ref[rows, :] = x_ref[rows, :] + _gate(mods_ref, 1) * y


def _ml_post(x, hf, hb, og, mods, head_g, w_out, layer, off, n, tm=512):
    tok = pl.BlockSpec((tm, D), lambda i: (i, 0))
    return pl.pallas_call(
        _ml_post_kernel,
        grid=(n,),
        in_specs=[_x_spec(tm, off), tok, tok, tok, _mods_spec(layer, tm, off),
                  _resident((1, D)), _resident((D, D))],
        out_specs=_x_spec(tm, off),
        out_shape=jax.ShapeDtypeStruct(x.shape, F32),
        input_output_aliases={0: 0}, compiler_params=_params(1), name="mlstm_post",
    )(x, hf, hb, og, mods, head_g.reshape(1, D), w_out)


def _fn_chan_kernel(x_ref, mods_ref, g_ref, cc_ref, sc_ref, a_ref, b_ref):
    hb = _mod_norm(x_ref[...], g_ref, mods_ref, 1).astype(BF16)
    for gi in range(FOURIER_GROUPS):
        sl = slice(gi * FG, (gi + 1) * FG)
        a_ref[:, sl] = _dot(hb[:, sl], cc_ref[...]).astype(BF16)
        b_ref[:, sl] = _dot(hb[:, sl], sc_ref[...]).astype(BF16)


def _fn_chan(x, mods, g, cc, sc, layer, off, n, tm=512):
    tok = pl.BlockSpec((tm, D), lambda i: (i, 0))
    return pl.pallas_call(
        _fn_chan_kernel,
        grid=(n,),
        in_specs=[_x_spec(tm, off), _mods_spec(layer, tm, off), _resident((1, D)),
                  _resident((FG, FG)), _resident((FG, FG))],
        out_specs=[tok, tok],
        out_shape=[jax.ShapeDtypeStruct((n * tm, D), BF16)] * 2,
        compiler_params=_params(1), name="fourier_chan",
    )(x, mods, g.reshape(1, D), cc, sc)


def _fn_seq_kernel(x_ref, a_ref, b_ref, cs_ref, ss_ref, mods_ref, w_ref, bias_ref, o_ref, *, scale):
    f = (_dot(cs_ref[...], a_ref[...]) - _dot(ss_ref[...], b_ref[...])) * scale
    y = _dot(f.astype(BF16), w_ref[...]) + bias_ref[...]
    o_ref[...] = x_ref[...] + _gate(mods_ref, 1) * y


def _fn_seq(x, a, b, cs, ss, mods, w_out, b_out, layer, off_tok, nb, seq, tr):
    nt = seq // tr
    off = off_tok // tr
    xs = pl.BlockSpec((tr, D), lambda bi, t: (off + bi * nt + t, 0))
    ab = pl.BlockSpec((seq, D), lambda bi, t: (bi, 0))
    tbl = pl.BlockSpec((tr, seq), lambda bi, t: (t, 0))
    mods_spec = pl.BlockSpec(
        (None, 9, D), lambda bi, t: (layer * N_COND + _tile_cond(off + bi * nt + t, tr), 0, 0))
    return pl.pallas_call(
        functools.partial(_fn_seq_kernel, scale=float((seq * FG) ** -0.5)),
        grid=(nb, nt),
        in_specs=[xs, ab, ab, tbl, tbl, mods_spec, _resident((D, D)), _resident((1, D))],
        out_specs=xs,
        out_shape=jax.ShapeDtypeStruct(x.shape, F32),
        input_output_aliases={0: 0}, compiler_params=_params(2), name="fourier_seq",
    )(x, a, b, cs, ss, mods, w_out, b_out.reshape(1, D))


def _fn_fused_kernel(x_ref, mods_ref, g_ref, cc_ref, sc_ref, cs_ref, ss_ref, w_ref, bias_ref, o_ref,
                     a_sc, b_sc, *, seq, scale):
    hb = _mod_norm(x_ref[...], g_ref, mods_ref, 1).astype(BF16)
    for gi in range(FOURIER_GROUPS):
        sl = slice(gi * FG, (gi + 1) * FG)
        a_sc[:, sl] = _dot(hb[:, sl], cc_ref[...]).astype(BF16)
        b_sc[:, sl] = _dot(hb[:, sl], sc_ref[...]).astype(BF16)
    for r0 in range(0, x_ref.shape[0], seq):
        rows = slice(r0, r0 + seq)
        f = (_dot(cs_ref[...], a_sc[rows, :]) - _dot(ss_ref[...], b_sc[rows, :])) * scale
        y = _dot(f.astype(BF16), w_ref[...]) + bias_ref[...]
        o_ref[rows, :] = x_ref[rows, :] + _gate(mods_ref, 1) * y


def _fn_fused(x, mods, g, cc, sc, cs, ss, w_out, b_out, layer, off, n, seq, tm=512):
    return pl.pallas_call(
        functools.partial(_fn_fused_kernel, seq=seq, scale=float((seq * FG) ** -0.5)),
        grid=(n,),
        in_specs=[_x_spec(tm, off), _mods_spec(layer, tm, off), _resident((1, D)),
                  _resident((FG, FG)), _resident((FG, FG)), _resident((seq, seq)), _resident((seq, seq)),
                  _resident((D, D)), _resident((1, D))],
        out_specs=_x_spec(tm, off),
        out_shape=jax.ShapeDtypeStruct(x.shape, F32),
        scratch_shapes=[pltpu.VMEM((tm, D), BF16), pltpu.VMEM((tm, D), BF16)],
        input_output_aliases={0: 0}, compiler_params=_params(1), name="fourier_fused",
    )(x, mods, g.reshape(1, D), cc, sc, cs, ss, w_out, b_out.reshape(1, D))


def _dft_tables(n):
    ang = 2.0 * np.pi * ((np.arange(n)[:, None] * np.arange(n)[None, :]) % n) / n
    return (jnp.asarray(np.cos(ang), F32).astype(BF16), jnp.asarray(np.sin(ang), F32).astype(BF16))


def _gm_kernel(x_ref, mods_ref, g_ref, win_ref, bin_ref, vg_ref, ws_ref, bs_ref, wout_ref, o_ref, sv_sc):
    gw = GM_W // GM_GROUPS
    for r0 in range(0, x_ref.shape[0], 2 * GM_CHUNK):
        x = x_ref[r0:r0 + 2 * GM_CHUNK, :]
        hb = _mod_norm(x, g_ref, mods_ref, 1).astype(BF16)
        z = _dot(hb, win_ref[...]) + bin_ref[...]
        z = z * (0.5 * (1.0 + jnp.tanh(np.sqrt(2.0 / np.pi) * (z + 0.044715 * (z * z * z)))))
        u = z[:, :GM_W]
        v = (_rms(z[:, GM_W:]) * vg_ref[...]).astype(BF16)
        for ch in range(2):
            rows = slice(ch * GM_CHUNK, (ch + 1) * GM_CHUNK)
            for gi in range(GM_GROUPS):
                cols = slice(gi * gw, (gi + 1) * gw)
                sv_sc[r0 + ch * GM_CHUNK:r0 + (ch + 1) * GM_CHUNK, cols] = (
                    _dot(ws_ref[gi], v[rows, cols]) + bs_ref[:, gi:gi + 1])
        y = _dot((u * sv_sc[r0:r0 + 2 * GM_CHUNK, :]).astype(BF16), wout_ref[...])
        o_ref[r0:r0 + 2 * GM_CHUNK, :] = x + _gate(mods_ref, 1) * y


def _gmlp(x, mods, g, w_in, b_in, v_g, w_s, b_s_t, w_out, layer, tm=512):
    return pl.pallas_call(
        _gm_kernel,
        grid=(N_TOK // tm,),
        in_specs=[_x_spec(tm, 0), _mods_spec(layer, tm, 0), _resident((1, D)),
                  _resident((D, 2 * GM_W)), _resident((1, 2 * GM_W)), _resident((1, GM_W)),
                  _resident((GM_GROUPS, GM_CHUNK, GM_CHUNK)), _resident((GM_CHUNK, GM_GROUPS)),
                  _resident((GM_W, D))],
        out_specs=_x_spec(tm, 0),
        out_shape=jax.ShapeDtypeStruct(x.shape, F32),
        scratch_shapes=[pltpu.VMEM((tm, GM_W), F32)],
        input_output_aliases={0: 0}, compiler_params=_params(1), name="gmlp",
    )(x, mods, g.reshape(1, D), w_in, b_in.reshape(1, 2 * GM_W), v_g.reshape(1, GM_W), w_s, b_s_t, w_out)


def _na_pre_kernel(x_ref, mods_ref, g_ref, w_ref, q_ref, k_ref, v_ref, *f32_refs):
    for rows in _row_groups(x_ref.shape[0]):
        hb = _mod_norm(x_ref[rows, :], g_ref, mods_ref, 1).astype(BF16)
        qkv = _dot(hb, w_ref[...])
        q_ref[rows, :] = (qkv[:, :D] * (NA_HD ** -0.5)).astype(BF16)
        k, v = qkv[:, D:2 * D], qkv[:, 2 * D:]
        k_ref[rows, :] = k.astype(BF16)
        v_ref[rows, :] = v.astype(BF16)
        if f32_refs:
            f32_refs[0][rows, :] = k
            f32_refs[1][rows, :] = v


def _na_pre(x, mods, g, w_qkv, layer, off, n, emit_f32, tm=512):
    tok = pl.BlockSpec((tm, D), lambda i: (i, 0))
    out_specs = [tok, tok, tok]
    out_shape = [jax.ShapeDtypeStruct((n * tm, D), BF16)] * 3
    if emit_f32:
        out_specs += [tok, tok]
        out_shape += [jax.ShapeDtypeStruct((n * tm, D), F32)] * 2
    return pl.pallas_call(
        _na_pre_kernel,
        grid=(n,),
        in_specs=[_x_spec(tm, off), _mods_spec(layer, tm, off), _resident((1, D)), _resident((D, 3 * D))],
        out_specs=out_specs, out_shape=out_shape,
        compiler_params=_params(1), name="na_qkv",
    )(x, mods, g.reshape(1, D), w_qkv)


def _half_mask(shape, e):
    lane = lax.broadcasted_iota(jnp.int32, shape, len(shape) - 1)
    return (lane < NA_HD) if e == 0 else (lane >= NA_HD)


def _ctx_attn_kernel(x_ref, q_ref, k_ref, v_ref, mods_ref, w_ref, o_ref, att_sc):
    for j in range(NA_PAIRS):
        sl = slice(j * LANE, (j + 1) * LANE)
        s = _dot_nt(_pair_rows(q_ref[:, sl]), k_ref[:, sl])
        p = jnp.exp(s - jnp.max(s, axis=1, keepdims=True))
        o = _dot(p.astype(BF16), v_ref[:, sl]) / jnp.sum(p, axis=1, keepdims=True)
        att_sc[:, sl] = jnp.where(_half_mask((SEQ, LANE), 0), o[:SEQ], o[SEQ:]).astype(BF16)
    o_ref[...] = x_ref[...] + _gate(mods_ref, 1) * _dot(att_sc[...], w_ref[...])


def _ctx_attn(x, q, k, v, mods, w_out, layer):
    tok = pl.BlockSpec((SEQ, D), lambda b: (b, 0))
    return pl.pallas_call(
        _ctx_attn_kernel,
        grid=(BATCH,),
        in_specs=[tok, tok, tok, tok, _mods_spec(layer, SEQ, 0), _resident((D, D))],
        out_specs=tok,
        out_shape=jax.ShapeDtypeStruct(x.shape, F32),
        scratch_shapes=[pltpu.VMEM((SEQ, D), BF16)],
        input_output_aliases={0: 0}, compiler_params=_params(1), name="ctx_attn",
    )(x, q, k, v, mods, w_out)


def _na_bias_kernel(rpb_ref, o_ref, pair_sc):
    h = pl.program_id(0)
    shape = (GRID_W, LANE)
    q = lax.broadcasted_iota(jnp.int32, shape, 0)
    lane = lax.broadcasted_iota(jnp.int32, shape, 1)
    x = lane & (GRID_W - 1)
    first = lane < GRID_W
    dc = x - q + (NA_KW - 1)
    q_start = jnp.clip(q - NA_KW // 2, 0, GRID_W - NA_KW)
    in_window = (x >= q_start) & (x < q_start + NA_KW)
    n_dc = 2 * NA_KW - 1
    for dr in range(2 * NA_KH - 2):
        acc = jnp.zeros(shape, F32)
        for j in range(n_dc):
            val = jnp.where(first, rpb_ref[h, dr * n_dc + j], rpb_ref[h, (dr + 1) * n_dc + j])
            acc = jnp.where(dc == j, val, acc)
        pair_sc[dr] = jnp.where(in_window, acc, NEG_INF)
    for o in range(NA_KH):
        for t in range(NA_KH // 2):
            o_ref[o, :, t * LANE:(t + 1) * LANE] = pair_sc[NA_KH - 1 - o + 2 * t]


def _na_bias_tables(rpb):
    n_rel = (2 * NA_KH - 1) * (2 * NA_KW - 1)
    t = pl.pallas_call(
        _na_bias_kernel,
        grid=(NA_HEADS,),
        in_specs=[pl.BlockSpec(memory_space=pltpu.SMEM)],
        out_specs=pl.BlockSpec((NA_KH, None, GRID_W, NA_WIN), lambda h: (0, h, 0, 0)),
        out_shape=jax.ShapeDtypeStruct((NA_KH, NA_HEADS, GRID_W, NA_WIN), F32),
        scratch_shapes=[pltpu.VMEM((2 * NA_KH - 2, GRID_W, LANE), F32)],
        compiler_params=_params(1), name="na_bias",
    )(rpb.reshape(NA_HEADS, n_rel))
    return t.reshape(NA_KH, NA_HEADS * GRID_W, NA_WIN)


def _pair_rows(qp):
    zero = jnp.zeros_like(qp)
    return jnp.concatenate([jnp.where(_half_mask(qp.shape, 0), qp, zero),
                            jnp.where(_half_mask(qp.shape, 1), qp, zero)], axis=0)


def _na_lat_kernel(q_ref, k_ref, v_ref, kc_ref, vc_ref, bias_ref, o_ref, p_sc):
    r = pl.program_id(1)
    start = pl.multiple_of(jnp.clip(r - NA_KH // 2, 0, NA_ROWS - NA_KH) * GRID_W, GRID_W)
    inv = []
    for j in range(NA_PAIRS):
        sl = slice(j * LANE, (j + 1) * LANE)
        qb = _pair_rows(q_ref[:, sl])
        s_w = _dot_nt(qb, k_ref[pl.ds(start, NA_WIN), sl]) + bias_ref[sl, :]
        s_c = _dot_nt(qb, kc_ref[:, sl])
        m = jnp.maximum(jnp.max(s_w, axis=1, keepdims=True), jnp.max(s_c, axis=1, keepdims=True))
        p_w = jnp.exp(s_w - m)
        p_c = jnp.exp(s_c - m)
        inv.append(1.0 / (jnp.sum(p_w, axis=1, keepdims=True) + jnp.sum(p_c, axis=1, keepdims=True)))
        p_sc[sl, :NA_WIN] = p_w.astype(BF16)
        p_sc[sl, NA_WIN:] = p_c.astype(BF16)
    for j in range(NA_PAIRS):
        sl = slice(j * LANE, (j + 1) * LANE)
        o = (_dot(p_sc[sl, :NA_WIN], v_ref[pl.ds(start, NA_WIN), sl])
             + _dot(p_sc[sl, NA_WIN:], vc_ref[:, sl])) * inv[j]
        o_ref[:, sl] = jnp.where(_half_mask((GRID_W, LANE), 0), o[:GRID_W], o[GRID_W:]).astype(BF16)


def _na_latent(q, k, v, kc, vc, bias):
    row_class = lambda r: r - jnp.clip(r - NA_KH // 2, 0, NA_ROWS - NA_KH)
    seq_kv = pl.BlockSpec((None, DEC_SEQ, D), lambda b, r: (b, 0, 0))
    ctx_kv = pl.BlockSpec((None, PAST_LEN, D), lambda b, r: (b, 0, 0))
    n_rows = NA_HEADS * GRID_W
    return pl.pallas_call(
        _na_lat_kernel,
        grid=(DEC_BATCH, NA_ROWS),
        in_specs=[pl.BlockSpec((GRID_W, D), lambda b, r: (b * NA_ROWS + r, 0)),
                  seq_kv, seq_kv, ctx_kv, ctx_kv,
                  pl.BlockSpec((None, n_rows, NA_WIN), lambda b, r: (row_class(r), 0, 0))],
        out_specs=pl.BlockSpec((GRID_W, D), lambda b, r: (b * NA_ROWS + r, 0)),
        out_shape=jax.ShapeDtypeStruct((NS_TOK, D), BF16),
        scratch_shapes=[pltpu.VMEM((n_rows, NA_WIN + PAST_LEN), BF16)],
        compiler_params=_params(2), name="na_latent",
    )(q, k.reshape(DEC_BATCH, DEC_SEQ, D), v.reshape(DEC_BATCH, DEC_SEQ, D), kc, vc, bias)


def kernel(x_prompt, x_sample, state_mlstm_C, state_mlstm_n, state_mlstm_m, cache_na_k, cache_na_v, c, c_ctx, w_ada, b_ada, norm_g, final_g, ffn_w1, ffn_w3, ffn_w2, ml_w_qkv, ml_w_if, ml_b_if, ml_w_og, ml_head_g, ml_w_out, fn_w_out, fn_b_out, gm_w_in, gm_b_in, gm_v_g, gm_w_s, gm_b_s, gm_w_out, na_w_qkv, na_w_out, na_rpb):
    tm = 512
    n_p, n_s, n_all = NP_TOK // tm, NS_TOK // tm, N_TOK // tm
    x = (x_prompt.reshape(NP_TOK, D), x_sample.reshape(NS_TOK, D))

    cond = jnp.zeros((N_COND, D), F32).at[0].set(c_ctx).at[1:1 + DEC_BATCH].set(c)
    mods = _adaln(cond, w_ada, b_ada).reshape(DEPTH * N_COND, 9, D)

    ffn_f32 = (ffn_w1, ffn_w3, ffn_w2)
    wb = tuple(w[0, 0].astype(BF16) for w in ffn_f32)
    outs = {}
    for l in range(DEPTH):
        kind, j = l % 4, l // 4
        x, wb = _ffn(x, mods, norm_g[l, 0], wb, l, 0, 0, N_TOK // FFN_TILE, next_f32=(*ffn_f32, l, 1))
        g = norm_g[l, 1]
        if kind == 0:
            w_dir = jnp.transpose(ml_w_if[j], (1, 0, 2))
            lane_pad = ((0, 0), (0, LANE - N_STATE))
            wif = jnp.concatenate(
                [jnp.pad(w_dir[:, :, :ML_HEADS].reshape(D, N_STATE), lane_pad),
                 jnp.pad(w_dir[:, :, ML_HEADS:].reshape(D, N_STATE), lane_pad)], axis=1)
            bif = jnp.concatenate(
                [jnp.pad(ml_b_if[j][:, :ML_HEADS].reshape(1, N_STATE), lane_pad),
                 jnp.pad(ml_b_if[j][:, ML_HEADS:].reshape(1, N_STATE), lane_pad)], axis=1)
            wqkv, wog, wout = ml_w_qkv[j].astype(BF16), ml_w_og[j].astype(BF16), ml_w_out[j].astype(BF16)
            for off, n, nb, seq in ((0, n_p, BATCH, SEQ), (n_p, n_s, DEC_BATCH, DEC_SEQ)):
                q, k, kt, v, og, gates = _ml_pre(x, mods, g, wqkv, wog, wif.astype(BF16), bif, l, off, n)
                if off == 0:
                    hf, hb, c_new, n_new, m_new = _ml_scan(q, k, kt, v, gates, nb, seq, emit_state=True)
                    outs["C"] = c_new.reshape(BATCH, 1, 2, ML_HEADS, ML_DK, ML_DK)
                    outs["n"] = n_new.reshape(BATCH, 1, 2, ML_HEADS, ML_DK)
                    outs["m"] = m_new[:, 0, :N_STATE].reshape(BATCH, 1, 2, ML_HEADS)
                else:
                    init = (state_mlstm_C[:, j].reshape(DEC_BATCH, N_STATE, ML_DK, ML_DK),
                            state_mlstm_n[:, j].reshape(DEC_BATCH, N_STATE, ML_DK),
                            jnp.pad(state_mlstm_m[:, j].reshape(DEC_BATCH, 1, N_STATE),
                                    ((0, 0), (0, 0), (0, LANE - N_STATE))))
                    hf, hb = _ml_scan(q, k, kt, v, gates, nb, seq, init=init)
                x = _ml_post(x, hf, hb, og, mods, ml_head_g[j], wout, l, off, n)
        elif kind == 1:
            cc, sc = _dft_tables(FG)
            wout = fn_w_out[j].astype(BF16)
            cs, ss = _dft_tables(SEQ)
            x = _fn_fused(x, mods, g, cc, sc, cs, ss, wout, fn_b_out[j], l, 0, n_p, SEQ)
            a, b = _fn_chan(x, mods, g, cc, sc, l, n_p, n_s)
            cs, ss = _dft_tables(DEC_SEQ)
            x = _fn_seq(x, a, b, cs, ss, mods, wout, fn_b_out[j], l, NP_TOK, DEC_BATCH, DEC_SEQ, 512)
        elif kind == 2:
            x = _gmlp(x, mods, g, gm_w_in[j].astype(BF16), gm_b_in[j], gm_v_g[j],
                      gm_w_s[j].astype(BF16), gm_b_s[j].T, gm_w_out[j].astype(BF16), l)
        else:
            wqkv, wout = na_w_qkv[j].astype(BF16), na_w_out[j].astype(BF16)
            q, k, v, k_heads, v_heads = _na_pre(x, mods, g, wqkv, l, 0, n_p, True)
            outs["k"] = k_heads.reshape(BATCH, 1, SEQ, NA_HEADS, NA_HD)
            outs["v"] = v_heads.reshape(BATCH, 1, SEQ, NA_HEADS, NA_HD)
            x = _ctx_attn(x, q, k, v, mods, wout, l)
            q, k, v = _na_pre(x, mods, g, wqkv, l, n_p, n_s, False)
            att = _na_latent(q, k, v,
                             cache_na_k[:, j].reshape(DEC_BATCH, PAST_LEN, D).astype(BF16),
                             cache_na_v[:, j].reshape(DEC_BATCH, PAST_LEN, D).astype(BF16),
                             _na_bias_tables(na_rpb[j]))
            x = _proj_residual(x, att, mods, wout, l, n_p, n_s)
        if l < DEPTH - 1:
            x, wb = _ffn(x, mods, norm_g[l, 2], wb, l, 1, 0, N_TOK // FFN_TILE, next_f32=(*ffn_f32, l + 1, 0))
        else:
            y_p = _ffn(x, mods, norm_g[l, 2], wb, l, 1, 0, NP_TOK // FFN_TILE, final_g=final_g)
            y_s = _ffn(x, mods, norm_g[l, 2], wb, l, 1, NP_TOK // FFN_TILE, NS_TOK // FFN_TILE, final_g=final_g)
    return (y_p.reshape(BATCH, SEQ, D), y_s.reshape(DEC_BATCH, DEC_SEQ, D),
            outs["C"], outs["n"], outs["m"], outs["k"], outs["v"])
```

```python
import functools

import numpy as np
import jax
import jax.numpy as jnp
from jax import lax
from jax.experimental import pallas as pl
from jax.experimental.pallas import tpu as pltpu

D = 1024
BATCH = 32
SEQ = 256
DEPTH = 4
DEC_BATCH = 2
DEC_SEQ = 2048
PAST_LEN = 512
GRID_W = 64
D_FF = 2816
EPS = 1e-6

NP_TOK = BATCH * SEQ
NS_TOK = DEC_BATCH * DEC_SEQ
N_TOK = NP_TOK + NS_TOK

ML_HEADS = 4
ML_DK = D // ML_HEADS
ML_CHUNK = 128
N_STATE = 2 * ML_HEADS
ML_SEQ_GROUP = 2

FOURIER_GROUPS = 4
FG = D // FOURIER_GROUPS

GM_W = D
GM_GROUPS = 4
GM_CHUNK = 128

NA_HEADS = 16
NA_HD = D // NA_HEADS
NA_KH = 8
NA_KW = 16
NA_ROWS = DEC_SEQ // GRID_W
NA_WIN = NA_KH * GRID_W
NA_PAIRS = NA_HEADS // 2
N_COND = 8

LANE = 128
BF16_ROWS = 16
FFN_ROWS = 128
FFN_TILE = 1024
FFN_CAST_CHUNKS = 8
VMEM_LIMIT = 60 * 1024 * 1024

F32 = jnp.float32
BF16 = jnp.bfloat16
NEG_INF = float("-inf")


def _params(n_axes, last_arbitrary=False):
    sem = ["parallel"] * n_axes
    if last_arbitrary:
        sem[-1] = "arbitrary"
    return pltpu.CompilerParams(dimension_semantics=tuple(sem), vmem_limit_bytes=VMEM_LIMIT)


def _resident(shape):
    zeros = (0,) * len(shape)
    return pl.BlockSpec(shape, lambda *_: zeros, pipeline_mode=pl.Buffered(1))


def _tile_cond(i, tm):
    row0 = i * tm
    return jnp.where(row0 < NP_TOK, 0, 1 + (row0 - NP_TOK) // DEC_SEQ)


def _x_spec(tm, off):
    return pl.BlockSpec((tm, D), lambda i: (i + off, 0))


def _mods_spec(layer, tm, off):
    return pl.BlockSpec((None, 9, D), lambda i: (layer * N_COND + _tile_cond(i + off, tm), 0, 0))


def _dot(a, b):
    return jnp.dot(a, b, preferred_element_type=F32)


def _dot_nt(a, b):
    return lax.dot_general(a, b, (((1,), (1,)), ((), ())), preferred_element_type=F32)


def _rms(x):
    return x * lax.rsqrt(jnp.mean(x * x, axis=-1, keepdims=True) + EPS)


def _mod_norm(x, g_ref, mods_ref, idx):
    h = _rms(x) * g_ref[...]
    return h * (1.0 + mods_ref[3 * idx + 1:3 * idx + 2, :]) + mods_ref[3 * idx:3 * idx + 1, :]


def _gate(mods_ref, idx):
    return mods_ref[3 * idx + 2:3 * idx + 3, :]


def _row_groups(n_rows, group=256):
    return [slice(r0, r0 + group) for r0 in range(0, n_rows, group)]


def _adaln_kernel(c_ref, w_ref, b_ref, o_ref):
    c = c_ref[...]
    s = (c * jax.nn.sigmoid(c)).astype(BF16)
    o_ref[...] = _dot(s, w_ref[...].astype(BF16)) + b_ref[...]


def _adaln(cond, w_ada, b_ada):
    tn = 1152
    nj = 9 * D // tn
    return pl.pallas_call(
        _adaln_kernel,
        grid=(DEPTH, nj),
        in_specs=[
            pl.BlockSpec((N_COND, D), lambda l, j: (0, 0)),
            pl.BlockSpec((None, D, tn), lambda l, j: (l, 0, j)),
            pl.BlockSpec((None, 1, tn), lambda l, j: (l, 0, j)),
        ],
        out_specs=pl.BlockSpec((None, N_COND, tn), lambda l, j: (l, 0, j)),
        out_shape=jax.ShapeDtypeStruct((DEPTH, N_COND, 9 * D), F32),
        compiler_params=_params(2),
        name="adaln",
    )(cond, w_ada, b_ada.reshape(DEPTH, 1, 9 * D))


def _ffn_kernel(*refs, idx, final, convert_next, split_at):
    if split_at is None:
        x_ref, refs = refs[0], refs[1:]
        load_x = lambda rows: x_ref[rows, :]
    else:
        (xa_ref, xb_ref), refs = refs[:2], refs[2:]
        first = pl.program_id(0) < split_at
        load_x = lambda rows: jnp.where(first, xa_ref[rows, :], xb_ref[rows, :])
    mods_ref, g_ref, w1_ref, w3_ref, w2_ref = refs[:5]
    rest = refs[5:]
    if convert_next:
        (nw1_ref, nw3_ref, nw2_ref), rest = rest[:3], rest[3:]
        cast_out, rest = rest[-3:], rest[:-3]
    o_ref = rest[-1]
    for rows in _row_groups(o_ref.shape[0], FFN_ROWS):
        x = load_x(rows)
        hb = _mod_norm(x, g_ref, mods_ref, idx).astype(BF16)
        a = _dot(hb, w1_ref[...])
        b = _dot(hb, w3_ref[...])
        act = (a * jax.nn.sigmoid(a) * b).astype(BF16)
        y = x + (0.5 * _gate(mods_ref, idx)) * _dot(act, w2_ref[...])
        if final:
            y = _rms(y) * rest[0][...]
        o_ref[rows, :] = y
    if convert_next:
        for src, dst in zip((nw1_ref, nw3_ref, nw2_ref), cast_out):
            dst[...] = src[...].astype(BF16)


def _ffn(x, mods, g, wb, layer, f, off, n, final_g=None, next_f32=None, tm=FFN_TILE):
    final = final_g is not None
    convert_next = next_f32 is not None
    idx = 2 * f
    split_at = None
    if isinstance(x, tuple):
        xa, xb = x
        split_at = xa.shape[0] // tm
        assert not final and off == 0 and xa.shape[0] % tm == 0 and n * tm == xa.shape[0] + xb.shape[0]
        x_specs = [pl.BlockSpec((tm, D), lambda i: (jnp.minimum(i, split_at - 1), 0)),
                   pl.BlockSpec((tm, D), lambda i: (jnp.maximum(i - split_at, 0), 0))]
        x_args = [xa, xb]
    else:
        x_specs, x_args = [_x_spec(tm, off)], [x]
    in_specs = x_specs + [_mods_spec(layer, tm, off), _resident((1, D)),
                          _resident((D, D_FF)), _resident((D, D_FF)), _resident((D_FF, D))]
    args = x_args + [mods, g.reshape(1, D), *wb]
    if convert_next:
        nw1, nw3, nw2, nl, nf = next_f32
        chunk = lambda i: jnp.minimum(i, FFN_CAST_CHUNKS - 1)
        for w in (nw1, nw3, nw2):
            rows, cols = w.shape[2] // FFN_CAST_CHUNKS, w.shape[3]
            in_specs.append(pl.BlockSpec((None, None, rows, cols), lambda i: (nl, nf, chunk(i), 0)))
            args.append(w)
    if final:
        in_specs.append(_resident((1, D)))
        args.append(final_g.reshape(1, D))
        out_specs = [pl.BlockSpec((tm, D), lambda i: (i, 0))]
        out_shape = [jax.ShapeDtypeStruct((n * tm, D), F32)]
        aliases = {}
    elif split_at is not None:
        out_specs = [_x_spec(tm, 0)]
        out_shape = [jax.ShapeDtypeStruct((n * tm, D), F32)]
        aliases = {}
    else:
        out_specs = [_x_spec(tm, off)]
        out_shape = [jax.ShapeDtypeStruct(x.shape, F32)]
        aliases = {0: 0}
    if convert_next:
        assert n >= FFN_CAST_CHUNKS
        for w in next_f32[:3]:
            rows, cols = w.shape[2] // FFN_CAST_CHUNKS, w.shape[3]
            out_specs.append(pl.BlockSpec((rows, cols), lambda i: (chunk(i), 0)))
            out_shape.append(jax.ShapeDtypeStruct(w.shape[2:], BF16))
    outs = pl.pallas_call(
        functools.partial(_ffn_kernel, idx=idx, final=final, convert_next=convert_next, split_at=split_at),
        grid=(n,), in_specs=in_specs, out_specs=out_specs, out_shape=out_shape,
        input_output_aliases=aliases, compiler_params=_params(1, last_arbitrary=True), name="ffn",
    )(*args)
    return (outs[0], tuple(outs[1:])) if convert_next else outs[0]


def _proj_kernel(x_ref, a_ref, mods_ref, w_ref, o_ref):
    y = _dot(a_ref[...].astype(BF16), w_ref[...])
    o_ref[...] = x_ref[...] + _gate(mods_ref, 1) * y


def _proj_residual(x, a, mods, w, layer, off, n, tm=512):
    return pl.pallas_call(
        _proj_kernel,
        grid=(n,),
        in_specs=[_x_spec(tm, off), pl.BlockSpec((tm, D), lambda i: (i, 0)),
                  _mods_spec(layer, tm, off), _resident((D, D))],
        out_specs=_x_spec(tm, off),
        out_shape=jax.ShapeDtypeStruct(x.shape, F32),
        input_output_aliases={0: 0}, compiler_params=_params(1), name="proj_residual",
    )(x, a, mods, w)


def _ml_pre_kernel(x_ref, mods_ref, g_ref, wqkv_ref, wog_ref, wif_ref, bif_ref,
                   q_ref, k_ref, kt_ref, v_ref, og_ref, gates_ref):
    for rows in _row_groups(x_ref.shape[0]):
        hb = _mod_norm(x_ref[rows, :], g_ref, mods_ref, 1).astype(BF16)
        qkv = _dot(hb, wqkv_ref[...])
        q_ref[rows, :] = qkv[:, :D].astype(BF16)
        k = qkv[:, D:2 * D] * (ML_DK ** -0.5)
        k_ref[rows, :] = k.astype(BF16)
        for ch in range(rows.start // ML_CHUNK, rows.stop // ML_CHUNK):
            kt_ref[ch] = k[ch * ML_CHUNK - rows.start:(ch + 1) * ML_CHUNK - rows.start, :].T.astype(BF16)
        v_ref[rows, :] = qkv[:, 2 * D:].astype(BF16)
        og_ref[rows, :] = jax.nn.sigmoid(_dot(hb, wog_ref[...])).astype(BF16)
        gates_ref[rows, :] = _dot(hb, wif_ref[...]) + bif_ref[...]


def _ml_pre(x, mods, g, wqkv, wog, wif, bif, layer, off, n, tm=512):
    tok = pl.BlockSpec((tm, D), lambda i: (i, 0))
    return pl.pallas_call(
        _ml_pre_kernel,
        grid=(n,),
        in_specs=[_x_spec(tm, off), _mods_spec(layer, tm, off), _resident((1, D)),
                  _resident((D, 3 * D)), _resident((D, D)), _resident((D, 2 * LANE)), _resident((1, 2 * LANE))],
        out_specs=[tok, tok, pl.BlockSpec((tm // ML_CHUNK, D, ML_CHUNK), lambda i: (i, 0, 0)), tok, tok,
                   pl.BlockSpec((tm, 2 * LANE), lambda i: (i, 0))],
        out_shape=[jax.ShapeDtypeStruct((n * tm, D), BF16)] * 2
        + [jax.ShapeDtypeStruct((n * tm // ML_CHUNK, D, ML_CHUNK), BF16), jax.ShapeDtypeStruct((n * tm, D), BF16),
           jax.ShapeDtypeStruct((n * tm, D), BF16), jax.ShapeDtypeStruct((n * tm, 2 * LANE), F32)],
        compiler_params=_params(1), name="mlstm_pre",
    )(x, mods, g.reshape(1, D), wqkv, wog, wif, bif)


def _log_sigmoid(x):
    return jnp.minimum(x, 0.0) - jnp.log1p(jnp.exp(-jnp.abs(x)))


def _scan_rows(x, fwd_lanes, op, identity):
    n = x.shape[0]
    row = lax.broadcasted_iota(jnp.int32, x.shape, 0)
    sh = 1
    while sh < n:
        prev = jnp.where(row >= sh, pltpu.roll(x, sh, 0), identity)
        nxt = jnp.where(row < n - sh, pltpu.roll(x, n - sh, 0), identity)
        x = op(x, jnp.where(fwd_lanes, prev, nxt))
        sh *= 2
    return x


def _ml_gates(fwd_refs, bwd_refs, m_sc, bi):
    L = ML_CHUNK
    lane = lax.broadcasted_iota(jnp.int32, (L, LANE), 1)
    fwd_lanes = lane < ML_HEADS
    gf_ref, gb_ref = fwd_refs[4], bwd_refs[4]
    i_pre = jnp.where(fwd_lanes, gf_ref[bi, :, :LANE], gb_ref[bi, :, :LANE])
    f_pre = jnp.where(fwd_lanes, gf_ref[bi, :, LANE:], gb_ref[bi, :, LANE:])
    log_f = jnp.where(lane < N_STATE, _log_sigmoid(f_pre), 0.0)
    bsum = _scan_rows(log_f, fwd_lanes, jnp.add, 0.0)
    rel = i_pre - bsum
    m_old = m_sc[bi]
    mm = jnp.maximum(m_old, _scan_rows(rel, fwd_lanes, jnp.maximum, NEG_INF))
    mx = jnp.maximum(m_old, jnp.max(rel, axis=0, keepdims=True))
    b_last = jnp.where(fwd_lanes[0:1], bsum[L - 1:L, :], bsum[0:1, :])
    m_sc[bi] = b_last + mx
    return dict(mm=mm, mx=mx,
                w_inter=jnp.exp(m_old - mm),
                floor=jnp.exp(-(bsum + mm)),
                decay=jnp.exp(m_old - mx),
                rel_t=rel.T)


def _ml_chain(refs, h_ref, c_sc, n_sc, gt, bi, d, hd):
    L = ML_CHUNK
    q_ref, k_ref, kt_ref, v_ref, _ = refs
    t_idx = lax.broadcasted_iota(jnp.int32, (L, L), 0)
    s_idx = lax.broadcasted_iota(jnp.int32, (L, L), 1)
    visible = (s_idx >= t_idx) if d == 1 else (s_idx <= t_idx)
    r = d * ML_HEADS + hd
    lo, hi = hd * ML_DK, (hd + 1) * ML_DK
    rel_row = gt["rel_t"][r:r + 1, :]
    w_col = gt["w_inter"][:, r:r + 1]
    n_old = n_sc[bi, r:r + 1, :]
    c_old = c_sc[bi, r]
    qh, kh, vh = q_ref[bi, :, lo:hi], k_ref[bi, :, lo:hi], v_ref[bi, :, lo:hi]
    kth = kt_ref[bi, lo:hi, :]

    a = jnp.exp(jnp.where(visible, rel_row - gt["mm"][:, r:r + 1], NEG_INF)) * _dot(qh, kth)
    num = _dot(a.astype(BF16), vh) + w_col * _dot(qh, c_old.astype(BF16))
    den = jnp.sum(a, axis=1, keepdims=True) + w_col * jnp.sum(qh.astype(F32) * n_old, axis=1, keepdims=True)
    h_ref[bi, :, lo:hi] = (num * (1.0 / jnp.maximum(jnp.abs(den), gt["floor"][:, r:r + 1]))).astype(BF16)

    w_row = jnp.exp(rel_row - gt["mx"][:, r:r + 1])
    dec = gt["decay"][:, r:r + 1]
    c_sc[bi, r] = dec * c_old + _dot((kth.astype(F32) * w_row).astype(BF16), vh)
    w_rows = jnp.broadcast_to(w_row, (BF16_ROWS, L)).astype(BF16)
    n_sc[bi, r:r + 1, :] = dec * n_old + _dot(w_rows, kh)[0:1, :]


def _ml_scan_kernel(*refs, with_init, emit_state):
    fwd_refs, bwd_refs, refs = refs[:5], refs[5:10], refs[10:]
    if with_init:
        (c0_ref, n0_ref, m0_ref), refs = refs[:3], refs[3:]
    (hf_ref, hb_ref), refs = refs[:2], refs[2:]
    if emit_state:
        (co_ref, no_ref, mo_ref), refs = refs[:3], refs[3:]
    c_sc, n_sc, m_sc = refs
    c = pl.program_id(1)

    @pl.when(c == 0)
    def _():
        if with_init:
            c_sc[...] = c0_ref[...]
            n_sc[...] = n0_ref[...]
            m_sc[...] = m0_ref[...]
        else:
            c_sc[...] = jnp.zeros_like(c_sc)
            n_sc[...] = jnp.zeros_like(n_sc)
            m_sc[...] = jnp.zeros_like(m_sc)

    seqs = range(c_sc.shape[0])
    gates = [_ml_gates(fwd_refs, bwd_refs, m_sc, bi) for bi in seqs]
    for d, (refs_d, h_ref) in enumerate(((fwd_refs, hf_ref), (bwd_refs, hb_ref))):
        for hd in range(ML_HEADS):
            for bi in seqs:
                _ml_chain(refs_d, h_ref, c_sc, n_sc, gates[bi], bi, d, hd)

    if emit_state:
        @pl.when(c == pl.num_programs(1) - 1)
        def _():
            co_ref[...] = c_sc[...]
            no_ref[...] = n_sc[...]
            mo_ref[...] = m_sc[...]


def _ml_scan(q, k, kt, v, gates, nb, seq, init=None, emit_state=False, group=ML_SEQ_GROUP):
    nc = seq // ML_CHUNK
    per_chunk = lambda a: a.reshape(nb, nc, *a.shape[-2:]) if a.ndim == 3 else a.reshape(nb, nc, ML_CHUNK, -1)
    fwd = lambda b, c: (b, c, 0, 0)
    bwd = lambda b, c: (b, nc - 1 - c, 0, 0)
    blk = lambda a, im: pl.BlockSpec((group, None) + a.shape[2:], im)
    st_c = pl.BlockSpec((group, N_STATE, ML_DK, ML_DK), lambda b, c: (b, 0, 0, 0))
    st_n = pl.BlockSpec((group, N_STATE, ML_DK), lambda b, c: (b, 0, 0))
    st_m = pl.BlockSpec((group, 1, LANE), lambda b, c: (b, 0, 0))
    arrays = [per_chunk(a) for a in (q, k, kt, v, gates)]
    in_specs = [blk(a, fwd) for a in arrays] + [blk(a, bwd) for a in arrays]
    args = arrays * 2
    if init is not None:
        in_specs += [st_c, st_n, st_m]
        args += list(init)
    h_shape = jax.ShapeDtypeStruct((nb, nc, ML_CHUNK, D), BF16)
    out_specs = [blk(h_shape, fwd), blk(h_shape, bwd)]
    out_shape = [h_shape] * 2
    if emit_state:
        out_specs += [st_c, st_n, st_m]
        out_shape += [jax.ShapeDtypeStruct((nb, N_STATE, ML_DK, ML_DK), F32),
                      jax.ShapeDtypeStruct((nb, N_STATE, ML_DK), F32),
                      jax.ShapeDtypeStruct((nb, 1, LANE), F32)]
    outs = pl.pallas_call(
        functools.partial(_ml_scan_kernel, with_init=init is not None, emit_state=emit_state),
        grid=(nb // group, nc), in_specs=in_specs, out_specs=out_specs, out_shape=out_shape,
        scratch_shapes=[pltpu.VMEM((group, N_STATE, ML_DK, ML_DK), F32),
                        pltpu.VMEM((group, N_STATE, ML_DK), F32), pltpu.VMEM((group, 1, LANE), F32)],
        compiler_params=_params(2, last_arbitrary=True), name="mlstm_scan",
    )(*args)
    return [outs[0].reshape(nb * seq, D), outs[1].reshape(nb * seq, D)] + list(outs[2:])


def _ml_post_kernel(x_ref, hf_ref, hb_ref, og_ref, mods_ref, hg_ref, w_ref, o_ref):
    for rows in _row_groups(x_ref.shape[0]):
        hsum = hf_ref[rows, :].astype(F32) + hb_ref[rows, :].astype(F32)
        hn = jnp.concatenate(
            [_rms(hsum[:, hd * ML_DK:(hd + 1) * ML_DK]) for hd in range(ML_HEADS)], axis=1)
        y = _dot((og_ref[rows, :].astype(F32) * (hn * hg_ref[...])).astype(BF16), w_ref[...])
        o_ref[rows, :] = x_ref[rows, :] + _gate(mods_ref, 1) * y


def _ml_post(x, hf, hb, og, mods, head_g, w_out, layer, off, n, tm=512):
    tok = pl.BlockSpec((tm, D), lambda i: (i, 0))
    return pl.pallas_call(
        _ml_post_kernel,
        grid=(n,),
        in_specs=[_x_spec(tm, off), tok, tok, tok, _mods_spec(layer, tm, off),
                  _resident((1, D)), _resident((D, D))],
        out_specs=_x_spec(tm, off),
        out_shape=jax.ShapeDtypeStruct(x.shape, F32),
        input_output_aliases={0: 0}, compiler_params=_params(1), name="mlstm_post",
    )(x, hf, hb, og, mods, head_g.reshape(1, D), w_out)


def _fn_chan_kernel(x_ref, mods_ref, g_ref, cc_ref, sc_ref, a_ref, b_ref):
    hb = _mod_norm(x_ref[...], g_ref, mods_ref, 1).astype(BF16)
    for gi in range(FOURIER_GROUPS):
        sl = slice(gi * FG, (gi + 1) * FG)
        a_ref[:, sl] = _dot(hb[:, sl], cc_ref[...]).astype(BF16)
        b_ref[:, sl] = _dot(hb[:, sl], sc_ref[...]).astype(BF16)


def _fn_chan(x, mods, g, cc, sc, layer, off, n, tm=512):
    tok = pl.BlockSpec((tm, D), lambda i: (i, 0))
    return pl.pallas_call(
        _fn_chan_kernel,
        grid=(n,),
        in_specs=[_x_spec(tm, off), _mods_spec(layer, tm, off), _resident((1, D)),
                  _resident((FG, FG)), _resident((FG, FG))],
        out_specs=[tok, tok],
        out_shape=[jax.ShapeDtypeStruct((n * tm, D), BF16)] * 2,
        compiler_params=_params(1), name="fourier_chan",
    )(x, mods, g.reshape(1, D), cc, sc)


def _fn_seq_kernel(x_ref, a_ref, b_ref, cs_ref, ss_ref, mods_ref, w_ref, bias_ref, o_ref, *, scale):
    f = (_dot(cs_ref[...], a_ref[...]) - _dot(ss_ref[...], b_ref[...])) * scale
    y = _dot(f.astype(BF16), w_ref[...]) + bias_ref[...]
    o_ref[...] = x_ref[...] + _gate(mods_ref, 1) * y


def _fn_seq(x, a, b, cs, ss, mods, w_out, b_out, layer, off_tok, nb, seq, tr):
    nt = seq // tr
    off = off_tok // tr
    xs = pl.BlockSpec((tr, D), lambda bi, t: (off + bi * nt + t, 0))
    ab = pl.BlockSpec((seq, D), lambda bi, t: (bi, 0))
    tbl = pl.BlockSpec((tr, seq), lambda bi, t: (t, 0))
    mods_spec = pl.BlockSpec(
        (None, 9, D), lambda bi, t: (layer * N_COND + _tile_cond(off + bi * nt + t, tr), 0, 0))
    return pl.pallas_call(
        functools.partial(_fn_seq_kernel, scale=float((seq * FG) ** -0.5)),
        grid=(nb, nt),
        in_specs=[xs, ab, ab, tbl, tbl, mods_spec, _resident((D, D)), _resident((1, D))],
        out_specs=xs,
        out_shape=jax.ShapeDtypeStruct(x.shape, F32),
        input_output_aliases={0: 0}, compiler_params=_params(2), name="fourier_seq",
    )(x, a, b, cs, ss, mods, w_out, b_out.reshape(1, D))


def _fn_fused_kernel(x_ref, mods_ref, g_ref, cc_ref, sc_ref, cs_ref, ss_ref, w_ref, bias_ref, o_ref,
                     a_sc, b_sc, *, seq, scale):
    hb = _mod_norm(x_ref[...], g_ref, mods_ref, 1).astype(BF16)
    for gi in range(FOURIER_GROUPS):
        sl = slice(gi * FG, (gi + 1) * FG)
        a_sc[:, sl] = _dot(hb[:, sl], cc_ref[...]).astype(BF16)
        b_sc[:, sl] = _dot(hb[:, sl], sc_ref[...]).astype(BF16)
    for r0 in range(0, x_ref.shape[0], seq):
        rows = slice(r0, r0 + seq)
        f = (_dot(cs_ref[...], a_sc[rows, :]) - _dot(ss_ref[...], b_sc[rows, :])) * scale
        y = _dot(f.astype(BF16), w_ref[...]) + bias_ref[...]
        o_ref[rows, :] = x_ref[rows, :] + _gate(mods_ref, 1) * y


def _fn_fused(x, mods, g, cc, sc, cs, ss, w_out, b_out, layer, off, n, seq, tm=512):
    return pl.pallas_call(
        functools.partial(_fn_fused_kernel, seq=seq, scale=float((seq * FG) ** -0.5)),
        grid=(n,),
        in_specs=[_x_spec(tm, off), _mods_spec(layer, tm, off), _resident((1, D)),
                  _resident((FG, FG)), _resident((FG, FG)), _resident((seq, seq)), _resident((seq, seq)),
                  _resident((D, D)), _resident((1, D))],
        out_specs=_x_spec(tm, off),
        out_shape=jax.ShapeDtypeStruct(x.shape, F32),
        scratch_shapes=[pltpu.VMEM((tm, D), BF16), pltpu.VMEM((tm, D), BF16)],
        input_output_aliases={0: 0}, compiler_params=_params(1), name="fourier_fused",
    )(x, mods, g.reshape(1, D), cc, sc, cs, ss, w_out, b_out.reshape(1, D))


def _dft_tables(n):
    ang = 2.0 * np.pi * ((np.arange(n)[:, None] * np.arange(n)[None, :]) % n) / n
    return (jnp.asarray(np.cos(ang), F32).astype(BF16), jnp.asarray(np.sin(ang), F32).astype(BF16))


def _gm_kernel(x_ref, mods_ref, g_ref, win_ref, bin_ref, vg_ref, ws_ref, bs_ref, wout_ref, o_ref, sv_sc):
    gw = GM_W // GM_GROUPS
    groups = _row_groups(x_ref.shape[0], 2 * GM_CHUNK)
    zs = [_dot(_mod_norm(x_ref[rows, :], g_ref, mods_ref, 1).astype(BF16), win_ref[...]) + bin_ref[...]
          for rows in groups]
    us = []
    for rows, z in zip(groups, zs):
        z = z * (0.5 * (1.0 + jnp.tanh(np.sqrt(2.0 / np.pi) * (z + 0.044715 * (z * z * z)))))
        us.append(z[:, :GM_W])
        v = (_rms(z[:, GM_W:]) * vg_ref[...]).astype(BF16)
        for ch in range(2):
            crows = slice(ch * GM_CHUNK, (ch + 1) * GM_CHUNK)
            for gi in range(GM_GROUPS):
                cols = slice(gi * gw, (gi + 1) * gw)
                sv_sc[rows.start + ch * GM_CHUNK:rows.start + (ch + 1) * GM_CHUNK, cols] = (
                    _dot(ws_ref[gi], v[crows, cols]) + bs_ref[:, gi:gi + 1])
    for rows, u in zip(groups, us):
        y = _dot((u * sv_sc[rows, :]).astype(BF16), wout_ref[...])
        o_ref[rows, :] = x_ref[rows, :] + _gate(mods_ref, 1) * y


def _gmlp(x, mods, g, w_in, b_in, v_g, w_s, b_s_t, w_out, layer, tm=512):
    return pl.pallas_call(
        _gm_kernel,
        grid=(N_TOK // tm,),
        in_specs=[_x_spec(tm, 0), _mods_spec(layer, tm, 0), _resident((1, D)),
                  _resident((D, 2 * GM_W)), _resident((1, 2 * GM_W)), _resident((1, GM_W)),
                  _resident((GM_GROUPS, GM_CHUNK, GM_CHUNK)), _resident((GM_CHUNK, GM_GROUPS)),
                  _resident((GM_W, D))],
        out_specs=_x_spec(tm, 0),
        out_shape=jax.ShapeDtypeStruct(x.shape, F32),
        scratch_shapes=[pltpu.VMEM((tm, GM_W), F32)],
        input_output_aliases={0: 0}, compiler_params=_params(1), name="gmlp",
    )(x, mods, g.reshape(1, D), w_in, b_in.reshape(1, 2 * GM_W), v_g.reshape(1, GM_W), w_s, b_s_t, w_out)


def _na_pre_kernel(x_ref, mods_ref, g_ref, w_ref, q_ref, k_ref, v_ref, *f32_refs):
    for rows in _row_groups(x_ref.shape[0]):
        hb = _mod_norm(x_ref[rows, :], g_ref, mods_ref, 1).astype(BF16)
        qkv = _dot(hb, w_ref[...])
        q_ref[rows, :] = (qkv[:, :D] * (NA_HD ** -0.5)).astype(BF16)
        k, v = qkv[:, D:2 * D], qkv[:, 2 * D:]
        k_ref[rows, :] = k.astype(BF16)
        v_ref[rows, :] = v.astype(BF16)
        if f32_refs:
            f32_refs[0][rows, :] = k
            f32_refs[1][rows, :] = v


def _na_pre(x, mods, g, w_qkv, layer, off, n, emit_f32, tm=512):
    tok = pl.BlockSpec((tm, D), lambda i: (i, 0))
    out_specs = [tok, tok, tok]
    out_shape = [jax.ShapeDtypeStruct((n * tm, D), BF16)] * 3
    if emit_f32:
        out_specs += [tok, tok]
        out_shape += [jax.ShapeDtypeStruct((n * tm, D), F32)] * 2
    return pl.pallas_call(
        _na_pre_kernel,
        grid=(n,),
        in_specs=[_x_spec(tm, off), _mods_spec(layer, tm, off), _resident((1, D)), _resident((D, 3 * D))],
        out_specs=out_specs, out_shape=out_shape,
        compiler_params=_params(1), name="na_qkv",
    )(x, mods, g.reshape(1, D), w_qkv)


def _half_mask(shape, e):
    lane = lax.broadcasted_iota(jnp.int32, shape, len(shape) - 1)
    return (lane < NA_HD) if e == 0 else (lane >= NA_HD)


def _ctx_attn_kernel(x_ref, q_ref, k_ref, v_ref, mods_ref, w_ref, o_ref, att_sc):
    for j in range(NA_PAIRS):
        sl = slice(j * LANE, (j + 1) * LANE)
        s = _dot_nt(_pair_rows(q_ref[:, sl]), k_ref[:, sl])
        p = jnp.exp(s - jnp.max(s, axis=1, keepdims=True))
        o = _dot(p.astype(BF16), v_ref[:, sl]) / jnp.sum(p, axis=1, keepdims=True)
        att_sc[:, sl] = jnp.where(_half_mask((SEQ, LANE), 0), o[:SEQ], o[SEQ:]).astype(BF16)
    o_ref[...] = x_ref[...] + _gate(mods_ref, 1) * _dot(att_sc[...], w_ref[...])


def _ctx_attn(x, q, k, v, mods, w_out, layer):
    tok = pl.BlockSpec((SEQ, D), lambda b: (b, 0))
    return pl.pallas_call(
        _ctx_attn_kernel,
        grid=(BATCH,),
        in_specs=[tok, tok, tok, tok, _mods_spec(layer, SEQ, 0), _resident((D, D))],
        out_specs=tok,
        out_shape=jax.ShapeDtypeStruct(x.shape, F32),
        scratch_shapes=[pltpu.VMEM((SEQ, D), BF16)],
        input_output_aliases={0: 0}, compiler_params=_params(1), name="ctx_attn",
    )(x, q, k, v, mods, w_out)


def _na_bias_kernel(rpb_ref, o_ref, pair_sc):
    h = pl.program_id(0)
    shape = (GRID_W, LANE)
    q = lax.broadcasted_iota(jnp.int32, shape, 0)
    lane = lax.broadcasted_iota(jnp.int32, shape, 1)
    x = lane & (GRID_W - 1)
    first = lane < GRID_W
    dc = x - q + (NA_KW - 1)
    q_start = jnp.clip(q - NA_KW // 2, 0, GRID_W - NA_KW)
    in_window = (x >= q_start) & (x < q_start + NA_KW)
    n_dc = 2 * NA_KW - 1
    for dr in range(2 * NA_KH - 2):
        acc = jnp.zeros(shape, F32)
        for j in range(n_dc):
            val = jnp.where(first, rpb_ref[h, dr * n_dc + j], rpb_ref[h, (dr + 1) * n_dc + j])
            acc = jnp.where(dc == j, val, acc)
        pair_sc[dr] = jnp.where(in_window, acc, NEG_INF)
    for o in range(NA_KH):
        for t in range(NA_KH // 2):
            o_ref[o, :, t * LANE:(t + 1) * LANE] = pair_sc[NA_KH - 1 - o + 2 * t]


def _na_bias_tables(rpb):
    n_rel = (2 * NA_KH - 1) * (2 * NA_KW - 1)
    t = pl.pallas_call(
        _na_bias_kernel,
        grid=(NA_HEADS,),
        in_specs=[pl.BlockSpec(memory_space=pltpu.SMEM)],
        out_specs=pl.BlockSpec((NA_KH, None, GRID_W, NA_WIN), lambda h: (0, h, 0, 0)),
        out_shape=jax.ShapeDtypeStruct((NA_KH, NA_HEADS, GRID_W, NA_WIN), F32),
        scratch_shapes=[pltpu.VMEM((2 * NA_KH - 2, GRID_W, LANE), F32)],
        compiler_params=_params(1), name="na_bias",
    )(rpb.reshape(NA_HEADS, n_rel))
    return t.reshape(NA_KH, NA_HEADS * GRID_W, NA_WIN)


def _pair_rows(qp):
    zero = jnp.zeros_like(qp)
    return jnp.concatenate([jnp.where(_half_mask(qp.shape, 0), qp, zero),
                            jnp.where(_half_mask(qp.shape, 1), qp, zero)], axis=0)


def _na_lat_kernel(q_ref, k_ref, v_ref, kc_ref, vc_ref, bias_ref, o_ref, p_sc):
    r = pl.program_id(1)
    start = pl.multiple_of(jnp.clip(r - NA_KH // 2, 0, NA_ROWS - NA_KH) * GRID_W, GRID_W)
    inv = []
    for j in range(NA_PAIRS):
        sl = slice(j * LANE, (j + 1) * LANE)
        qb = _pair_rows(q_ref[:, sl])
        s_w = _dot_nt(qb, k_ref[pl.ds(start, NA_WIN), sl]) + bias_ref[sl, :]
        s_c = _dot_nt(qb, kc_ref[:, sl])
        m = jnp.maximum(jnp.max(s_w, axis=1, keepdims=True), jnp.max(s_c, axis=1, keepdims=True))
        p_w = jnp.exp(s_w - m)
        p_c = jnp.exp(s_c - m)
        inv.append(1.0 / (jnp.sum(p_w, axis=1, keepdims=True) + jnp.sum(p_c, axis=1, keepdims=True)))
        p_sc[sl, :NA_WIN] = p_w.astype(BF16)
        p_sc[sl, NA_WIN:] = p_c.astype(BF16)
    for j in range(NA_PAIRS):
        sl = slice(j * LANE, (j + 1) * LANE)
        o = (_dot(p_sc[sl, :NA_WIN], v_ref[pl.ds(start, NA_WIN), sl])
             + _dot(p_sc[sl, NA_WIN:], vc_ref[:, sl])) * inv[j]
        o_ref[:, sl] = jnp.where(_half_mask((GRID_W, LANE), 0), o[:GRID_W], o[GRID_W:]).astype(BF16)


def _na_latent(q, k, v, kc, vc, bias):
    row_class = lambda r: r - jnp.clip(r - NA_KH // 2, 0, NA_ROWS - NA_KH)
    seq_kv = pl.BlockSpec((None, DEC_SEQ, D), lambda b, r: (b, 0, 0))
    ctx_kv = pl.BlockSpec((None, PAST_LEN, D), lambda b, r: (b, 0, 0))
    n_rows = NA_HEADS * GRID_W
    return pl.pallas_call(
        _na_lat_kernel,
        grid=(DEC_BATCH, NA_ROWS),
        in_specs=[pl.BlockSpec((GRID_W, D), lambda b, r: (b * NA_ROWS + r, 0)),
                  seq_kv, seq_kv, ctx_kv, ctx_kv,
                  pl.BlockSpec((None, n_rows, NA_WIN), lambda b, r: (row_class(r), 0, 0))],
        out_specs=pl.BlockSpec((GRID_W, D), lambda b, r: (b * NA_ROWS + r, 0)),
        out_shape=jax.ShapeDtypeStruct((NS_TOK, D), BF16),
        scratch_shapes=[pltpu.VMEM((n_rows, NA_WIN + PAST_LEN), BF16)],
        compiler_params=_params(2), name="na_latent",
    )(q, k.reshape(DEC_BATCH, DEC_SEQ, D), v.reshape(DEC_BATCH, DEC_SEQ, D), kc, vc, bias)


def kernel(x_prompt, x_sample, state_mlstm_C, state_mlstm_n, state_mlstm_m, cache_na_k, cache_na_v, c, c_ctx, w_ada, b_ada, norm_g, final_g, ffn_w1, ffn_w3, ffn_w2, ml_w_qkv, ml_w_if, ml_b_if, ml_w_og, ml_head_g, ml_w_out, fn_w_out, fn_b_out, gm_w_in, gm_b_in, gm_v_g, gm_w_s, gm_b_s, gm_w_out, na_w_qkv, na_w_out, na_rpb):
    tm = 512
    n_p, n_s, n_all = NP_TOK // tm, NS_TOK // tm, N_TOK // tm
    x = (x_prompt.reshape(NP_TOK, D), x_sample.reshape(NS_TOK, D))

    cond = jnp.zeros((N_COND, D), F32).at[0].set(c_ctx).at[1:1 + DEC_BATCH].set(c)
    mods = _adaln(cond, w_ada, b_ada).reshape(DEPTH * N_COND, 9, D)

    ffn_f32 = (ffn_w1, ffn_w3, ffn_w2)
    wb = tuple(w[0, 0].astype(BF16) for w in ffn_f32)
    outs = {}
    for l in range(DEPTH):
        kind, j = l % 4, l // 4
        x, wb = _ffn(x, mods, norm_g[l, 0], wb, l, 0, 0, N_TOK // FFN_TILE, next_f32=(*ffn_f32, l, 1))
        g = norm_g[l, 1]
        if kind == 0:
            w_dir = jnp.transpose(ml_w_if[j], (1, 0, 2))
            lane_pad = ((0, 0), (0, LANE - N_STATE))
            wif = jnp.concatenate(
                [jnp.pad(w_dir[:, :, :ML_HEADS].reshape(D, N_STATE), lane_pad),
                 jnp.pad(w_dir[:, :, ML_HEADS:].reshape(D, N_STATE), lane_pad)], axis=1)
            bif = jnp.concatenate(
                [jnp.pad(ml_b_if[j][:, :ML_HEADS].reshape(1, N_STATE), lane_pad),
                 jnp.pad(ml_b_if[j][:, ML_HEADS:].reshape(1, N_STATE), lane_pad)], axis=1)
            wqkv, wog, wout = ml_w_qkv[j].astype(BF16), ml_w_og[j].astype(BF16), ml_w_out[j].astype(BF16)
            for off, n, nb, seq in ((0, n_p, BATCH, SEQ), (n_p, n_s, DEC_BATCH, DEC_SEQ)):
                q, k, kt, v, og, gates = _ml_pre(x, mods, g, wqkv, wog, wif.astype(BF16), bif, l, off, n)
                if off == 0:
                    hf, hb, c_new, n_new, m_new = _ml_scan(q, k, kt, v, gates, nb, seq, emit_state=True)
                    outs["C"] = c_new.reshape(BATCH, 1, 2, ML_HEADS, ML_DK, ML_DK)
                    outs["n"] = n_new.reshape(BATCH, 1, 2, ML_HEADS, ML_DK)
                    outs["m"] = m_new[:, 0, :N_STATE].reshape(BATCH, 1, 2, ML_HEADS)
                else:
                    init = (state_mlstm_C[:, j].reshape(DEC_BATCH, N_STATE, ML_DK, ML_DK),
                            state_mlstm_n[:, j].reshape(DEC_BATCH, N_STATE, ML_DK),
                            jnp.pad(state_mlstm_m[:, j].reshape(DEC_BATCH, 1, N_STATE),
                                    ((0, 0), (0, 0), (0, LANE - N_STATE))))
                    hf, hb = _ml_scan(q, k, kt, v, gates, nb, seq, init=init)
                x = _ml_post(x, hf, hb, og, mods, ml_head_g[j], wout, l, off, n)
        elif kind == 1:
            cc, sc = _dft_tables(FG)
            wout = fn_w_out[j].astype(BF16)
            cs, ss = _dft_tables(SEQ)
            x = _fn_fused(x, mods, g, cc, sc, cs, ss, wout, fn_b_out[j], l, 0, n_p, SEQ)
            a, b = _fn_chan(x, mods, g, cc, sc, l, n_p, n_s)
            cs, ss = _dft_tables(DEC_SEQ)
            x = _fn_seq(x, a, b, cs, ss, mods, wout, fn_b_out[j], l, NP_TOK, DEC_BATCH, DEC_SEQ, 512)
        elif kind == 2:
            x = _gmlp(x, mods, g, gm_w_in[j].astype(BF16), gm_b_in[j], gm_v_g[j],
                      gm_w_s[j].astype(BF16), gm_b_s[j].T, gm_w_out[j].astype(BF16), l)
        else:
            wqkv, wout = na_w_qkv[j].astype(BF16), na_w_out[j].astype(BF16)
            q, k, v, k_heads, v_heads = _na_pre(x, mods, g, wqkv, l, 0, n_p, True)
            outs["k"] = k_heads.reshape(BATCH, 1, SEQ, NA_HEADS, NA_HD)
            outs["v"] = v_heads.reshape(BATCH, 1, SEQ, NA_HEADS, NA_HD)
            x = _ctx_attn(x, q, k, v, mods, wout, l)
            q, k, v = _na_pre(x, mods, g, wqkv, l, n_p, n_s, False)
            att = _na_latent(q, k, v,
                             cache_na_k[:, j].reshape(DEC_BATCH, PAST_LEN, D).astype(BF16),
                             cache_na_v[:, j].reshape(DEC_BATCH, PAST_LEN, D).astype(BF16),
                             _na_bias_tables(na_rpb[j]))
            x = _proj_residual(x, att, mods, wout, l, n_p, n_s)
        if l < DEPTH - 1:
            x, wb = _ffn(x, mods, norm_g[l, 2], wb, l, 1, 0, N_TOK // FFN_TILE, next_f32=(*ffn_f32, l + 1, 0))
        else:
            y_p = _ffn(x, mods, norm_g[l, 2], wb, l, 1, 0, NP_TOK // FFN_TILE, final_g=final_g)
            y_s = _ffn(x, mods, norm_g[l, 2], wb, l, 1, NP_TOK // FFN_TILE, NS_TOK // FFN_TILE, final_g=final_g)
    return (y_p.reshape(BATCH, SEQ, D), y_s.reshape(DEC_BATCH, DEC_SEQ, D),
            outs["C"], outs["n"], outs["m"], outs["k"], outs["v"])
```

```python
import functools

import numpy as np
import jax
import jax.numpy as jnp
from jax import lax
from jax.experimental import pallas as pl
from jax.experimental.pallas import tpu as pltpu

D = 1024
BATCH = 32
SEQ = 256
DEPTH = 4
DEC_BATCH = 2
DEC_SEQ = 2048
PAST_LEN = 512
GRID_W = 64
D_FF = 2816
EPS = 1e-6

NP_TOK = BATCH * SEQ
NS_TOK = DEC_BATCH * DEC_SEQ
N_TOK = NP_TOK + NS_TOK

ML_HEADS = 4
ML_DK = D // ML_HEADS
ML_CHUNK = 128
N_STATE = 2 * ML_HEADS
ML_SEQ_GROUP = 2

FOURIER_GROUPS = 4
FG = D // FOURIER_GROUPS

GM_W = D
GM_GROUPS = 4
GM_CHUNK = 128

NA_HEADS = 16
NA_HD = D // NA_HEADS
NA_KH = 8
NA_KW = 16
NA_ROWS = DEC_SEQ // GRID_W
NA_WIN = NA_KH * GRID_W
NA_PAIRS = NA_HEADS // 2
N_COND = 8

LANE = 128
BF16_ROWS = 16
FFN_ROWS = 256
FFN_TILE = 1024
FFN_CAST_CHUNKS = 8
VMEM_LIMIT = 60 * 1024 * 1024

F32 = jnp.float32
BF16 = jnp.bfloat16
NEG_INF = float("-inf")


def _params(n_axes, last_arbitrary=False):
    sem = ["parallel"] * n_axes
    if last_arbitrary:
        sem[-1] = "arbitrary"
    return pltpu.CompilerParams(dimension_semantics=tuple(sem), vmem_limit_bytes=VMEM_LIMIT)


def _resident(shape):
    zeros = (0,) * len(shape)
    return pl.BlockSpec(shape, lambda *_: zeros, pipeline_mode=pl.Buffered(1))


def _tile_cond(i, tm):
    row0 = i * tm
    return jnp.where(row0 < NP_TOK, 0, 1 + (row0 - NP_TOK) // DEC_SEQ)


def _x_spec(tm, off):
    return pl.BlockSpec((tm, D), lambda i: (i + off, 0))


def _mods_spec(layer, tm, off):
    return pl.BlockSpec((None, 9, D), lambda i: (layer * N_COND + _tile_cond(i + off, tm), 0, 0))


def _dot(a, b):
    return jnp.dot(a, b, preferred_element_type=F32)


def _dot_nt(a, b):
    return lax.dot_general(a, b, (((1,), (1,)), ((), ())), preferred_element_type=F32)


def _rms(x):
    return x * lax.rsqrt(jnp.mean(x * x, axis=-1, keepdims=True) + EPS)


def _mod_norm(x, g_ref, mods_ref, idx):
    h = _rms(x) * g_ref[...]
    return h * (1.0 + mods_ref[3 * idx + 1:3 * idx + 2, :]) + mods_ref[3 * idx:3 * idx + 1, :]


def _gate(mods_ref, idx):
    return mods_ref[3 * idx + 2:3 * idx + 3, :]


def _row_groups(n_rows, group=256):
    return [slice(r0, r0 + group) for r0 in range(0, n_rows, group)]


def _adaln_kernel(c_ref, w_ref, b_ref, o_ref):
    c = c_ref[...]
    s = (c * jax.nn.sigmoid(c)).astype(BF16)
    o_ref[...] = _dot(s, w_ref[...].astype(BF16)) + b_ref[...]


def _adaln(cond, w_ada, b_ada):
    tn = 1152
    nj = 9 * D // tn
    return pl.pallas_call(
        _adaln_kernel,
        grid=(DEPTH, nj),
        in_specs=[
            pl.BlockSpec((N_COND, D), lambda l, j: (0, 0)),
            pl.BlockSpec((None, D, tn), lambda l, j: (l, 0, j)),
            pl.BlockSpec((None, 1, tn), lambda l, j: (l, 0, j)),
        ],
        out_specs=pl.BlockSpec((None, N_COND, tn), lambda l, j: (l, 0, j)),
        out_shape=jax.ShapeDtypeStruct((DEPTH, N_COND, 9 * D), F32),
        compiler_params=_params(2),
        name="adaln",
    )(cond, w_ada, b_ada.reshape(DEPTH, 1, 9 * D))


def _ffn_kernel(*refs, idx, final, convert_next, split_at):
    if split_at is None:
        x_ref, refs = refs[0], refs[1:]
        load_x = lambda rows: x_ref[rows, :]
    else:
        (xa_ref, xb_ref), refs = refs[:2], refs[2:]
        first = pl.program_id(0) < split_at
        load_x = lambda rows: jnp.where(first, xa_ref[rows, :], xb_ref[rows, :])
    mods_ref, g_ref, w1_ref, w3_ref, w2_ref = refs[:5]
    rest = refs[5:]
    if convert_next:
        (nw1_ref, nw3_ref, nw2_ref), rest = rest[:3], rest[3:]
        cast_out, rest = rest[-3:], rest[:-3]
    o_ref = rest[-1]
    for rows in _row_groups(o_ref.shape[0], FFN_ROWS):
        x = load_x(rows)
        hb = _mod_norm(x, g_ref, mods_ref, idx).astype(BF16)
        a = _dot(hb, w1_ref[...])
        b = _dot(hb, w3_ref[...])
        act = (a * jax.nn.sigmoid(a) * b).astype(BF16)
        y = x + (0.5 * _gate(mods_ref, idx)) * _dot(act, w2_ref[...])
        if final:
            y = _rms(y) * rest[0][...]
        o_ref[rows, :] = y
    if convert_next:
        for src, dst in zip((nw1_ref, nw3_ref, nw2_ref), cast_out):
            dst[...] = src[...].astype(BF16)


def _ffn(x, mods, g, wb, layer, f, off, n, final_g=None, next_f32=None, tm=FFN_TILE):
    final = final_g is not None
    convert_next = next_f32 is not None
    idx = 2 * f
    split_at = None
    if isinstance(x, tuple):
        xa, xb = x
        split_at = xa.shape[0] // tm
        assert not final and off == 0 and xa.shape[0] % tm == 0 and n * tm == xa.shape[0] + xb.shape[0]
        x_specs = [pl.BlockSpec((tm, D), lambda i: (jnp.minimum(i, split_at - 1), 0)),
                   pl.BlockSpec((tm, D), lambda i: (jnp.maximum(i - split_at, 0), 0))]
        x_args = [xa, xb]
    else:
        x_specs, x_args = [_x_spec(tm, off)], [x]
    in_specs = x_specs + [_mods_spec(layer, tm, off), _resident((1, D)),
                          _resident((D, D_FF)), _resident((D, D_FF)), _resident((D_FF, D))]
    args = x_args + [mods, g.reshape(1, D), *wb]
    if convert_next:
        nw1, nw3, nw2, nl, nf = next_f32
        chunk = lambda i: jnp.minimum(i, FFN_CAST_CHUNKS - 1)
        for w in (nw1, nw3, nw2):
            rows, cols = w.shape[2] // FFN_CAST_CHUNKS, w.shape[3]
            in_specs.append(pl.BlockSpec((None, None, rows, cols), lambda i: (nl, nf, chunk(i), 0)))
            args.append(w)
    if final:
        in_specs.append(_resident((1, D)))
        args.append(final_g.reshape(1, D))
        out_specs = [pl.BlockSpec((tm, D), lambda i: (i, 0))]
        out_shape = [jax.ShapeDtypeStruct((n * tm, D), F32)]
        aliases = {}
    elif split_at is not None:
        out_specs = [_x_spec(tm, 0)]
        out_shape = [jax.ShapeDtypeStruct((n * tm, D), F32)]
        aliases = {}
    else:
        out_specs = [_x_spec(tm, off)]
        out_shape = [jax.ShapeDtypeStruct(x.shape, F32)]
        aliases = {0: 0}
    if convert_next:
        assert n >= FFN_CAST_CHUNKS
        for w in next_f32[:3]:
            rows, cols = w.shape[2] // FFN_CAST_CHUNKS, w.shape[3]
            out_specs.append(pl.BlockSpec((rows, cols), lambda i: (chunk(i), 0)))
            out_shape.append(jax.ShapeDtypeStruct(w.shape[2:], BF16))
    outs = pl.pallas_call(
        functools.partial(_ffn_kernel, idx=idx, final=final, convert_next=convert_next, split_at=split_at),
        grid=(n,), in_specs=in_specs, out_specs=out_specs, out_shape=out_shape,
        input_output_aliases=aliases, compiler_params=_params(1, last_arbitrary=True), name="ffn",
    )(*args)
    return (outs[0], tuple(outs[1:])) if convert_next else outs[0]


def _proj_kernel(x_ref, a_ref, mods_ref, w_ref, o_ref):
    y = _dot(a_ref[...].astype(BF16), w_ref[...])
    o_ref[...] = x_ref[...] + _gate(mods_ref, 1) * y


def _proj_residual(x, a, mods, w, layer, off, n, tm=512):
    return pl.pallas_call(
        _proj_kernel,
        grid=(n,),
        in_specs=[_x_spec(tm, off), pl.BlockSpec((tm, D), lambda i: (i, 0)),
                  _mods_spec(layer, tm, off), _resident((D, D))],
        out_specs=_x_spec(tm, off),
        out_shape=jax.ShapeDtypeStruct(x.shape, F32),
        input_output_aliases={0: 0}, compiler_params=_params(1), name="proj_residual",
    )(x, a, mods, w)


def _ml_pre_kernel(x_ref, mods_ref, g_ref, wqkv_ref, wog_ref, wif_ref, bif_ref,
                   q_ref, k_ref, kt_ref, v_ref, og_ref, gates_ref):
    for rows in _row_groups(x_ref.shape[0]):
        hb = _mod_norm(x_ref[rows, :], g_ref, mods_ref, 1).astype(BF16)
        qkv = _dot(hb, wqkv_ref[...])
        q_ref[rows, :] = qkv[:, :D].astype(BF16)
        k = qkv[:, D:2 * D] * (ML_DK ** -0.5)
        k_ref[rows, :] = k.astype(BF16)
        for ch in range(rows.start // ML_CHUNK, rows.stop // ML_CHUNK):
            kt_ref[ch] = k[ch * ML_CHUNK - rows.start:(ch + 1) * ML_CHUNK - rows.start, :].T.astype(BF16)
        v_ref[rows, :] = qkv[:, 2 * D:].astype(BF16)
        og_ref[rows, :] = jax.nn.sigmoid(_dot(hb, wog_ref[...])).astype(BF16)
        gates_ref[rows, :] = _dot(hb, wif_ref[...]) + bif_ref[...]


def _ml_pre(x, mods, g, wqkv, wog, wif, bif, layer, off, n, tm=512):
    tok = pl.BlockSpec((tm, D), lambda i: (i, 0))
    return pl.pallas_call(
        _ml_pre_kernel,
        grid=(n,),
        in_specs=[_x_spec(tm, off), _mods_spec(layer, tm, off), _resident((1, D)),
                  _resident((D, 3 * D)), _resident((D, D)), _resident((D, 2 * LANE)), _resident((1, 2 * LANE))],
        out_specs=[tok, tok, pl.BlockSpec((tm // ML_CHUNK, D, ML_CHUNK), lambda i: (i, 0, 0)), tok, tok,
                   pl.BlockSpec((tm, 2 * LANE), lambda i: (i, 0))],
        out_shape=[jax.ShapeDtypeStruct((n * tm, D), BF16)] * 2
        + [jax.ShapeDtypeStruct((n * tm // ML_CHUNK, D, ML_CHUNK), BF16), jax.ShapeDtypeStruct((n * tm, D), BF16),
           jax.ShapeDtypeStruct((n * tm, D), BF16), jax.ShapeDtypeStruct((n * tm, 2 * LANE), F32)],
        compiler_params=_params(1), name="mlstm_pre",
    )(x, mods, g.reshape(1, D), wqkv, wog, wif, bif)


def _log_sigmoid(x):
    return jnp.minimum(x, 0.0) - jnp.log1p(jnp.exp(-jnp.abs(x)))


def _scan_rows(x, fwd_lanes, op, identity):
    n = x.shape[0]
    row = lax.broadcasted_iota(jnp.int32, x.shape, 0)
    sh = 1
    while sh < n:
        prev = jnp.where(row >= sh, pltpu.roll(x, sh, 0), identity)
        nxt = jnp.where(row < n - sh, pltpu.roll(x, n - sh, 0), identity)
        x = op(x, jnp.where(fwd_lanes, prev, nxt))
        sh *= 2
    return x


def _ml_gates(fwd_refs, bwd_refs, m_sc, bi):
    L = ML_CHUNK
    lane = lax.broadcasted_iota(jnp.int32, (L, LANE), 1)
    fwd_lanes = lane < ML_HEADS
    gf_ref, gb_ref = fwd_refs[4], bwd_refs[4]
    i_pre = jnp.where(fwd_lanes, gf_ref[bi, :, :LANE], gb_ref[bi, :, :LANE])
    f_pre = jnp.where(fwd_lanes, gf_ref[bi, :, LANE:], gb_ref[bi, :, LANE:])
    log_f = jnp.where(lane < N_STATE, _log_sigmoid(f_pre), 0.0)
    bsum = _scan_rows(log_f, fwd_lanes, jnp.add, 0.0)
    rel = i_pre - bsum
    m_old = m_sc[bi]
    mm = jnp.maximum(m_old, _scan_rows(rel, fwd_lanes, jnp.maximum, NEG_INF))
    mx = jnp.maximum(m_old, jnp.max(rel, axis=0, keepdims=True))
    b_last = jnp.where(fwd_lanes[0:1], bsum[L - 1:L, :], bsum[0:1, :])
    m_sc[bi] = b_last + mx
    return dict(mm=mm, mx=mx,
                w_inter=jnp.exp(m_old - mm),
                floor=jnp.exp(-(bsum + mm)),
                decay=jnp.exp(m_old - mx),
                rel_t=rel.T)


def _ml_chain(refs, h_ref, c_sc, n_sc, gt, bi, d, hd):
    L = ML_CHUNK
    q_ref, k_ref, kt_ref, v_ref, _ = refs
    t_idx = lax.broadcasted_iota(jnp.int32, (L, L), 0)
    s_idx = lax.broadcasted_iota(jnp.int32, (L, L), 1)
    visible = (s_idx >= t_idx) if d == 1 else (s_idx <= t_idx)
    r = d * ML_HEADS + hd
    lo, hi = hd * ML_DK, (hd + 1) * ML_DK
    rel_row = gt["rel_t"][r:r + 1, :]
    w_col = gt["w_inter"][:, r:r + 1]
    n_old = n_sc[bi, r:r + 1, :]
    c_old = c_sc[bi, r]
    qh, kh, vh = q_ref[bi, :, lo:hi], k_ref[bi, :, lo:hi], v_ref[bi, :, lo:hi]
    kth = kt_ref[bi, lo:hi, :]

    a = jnp.exp(jnp.where(visible, rel_row - gt["mm"][:, r:r + 1], NEG_INF)) * _dot(qh, kth)
    num = _dot(a.astype(BF16), vh) + w_col * _dot(qh, c_old.astype(BF16))
    den = jnp.sum(a, axis=1, keepdims=True) + w_col * jnp.sum(qh.astype(F32) * n_old, axis=1, keepdims=True)
    h_ref[bi, :, lo:hi] = (num * (1.0 / jnp.maximum(jnp.abs(den), gt["floor"][:, r:r + 1]))).astype(BF16)

    w_row = jnp.exp(rel_row - gt["mx"][:, r:r + 1])
    dec = gt["decay"][:, r:r + 1]
    c_sc[bi, r] = dec * c_old + _dot((kth.astype(F32) * w_row).astype(BF16), vh)
    w_rows = jnp.broadcast_to(w_row, (BF16_ROWS, L)).astype(BF16)
    n_sc[bi, r:r + 1, :] = dec * n_old + _dot(w_rows, kh)[0:1, :]


def _ml_scan_kernel(*refs, with_init, emit_state):
    fwd_refs, bwd_refs, refs = refs[:5], refs[5:10], refs[10:]
    if with_init:
        (c0_ref, n0_ref, m0_ref), refs = refs[:3], refs[3:]
    (hf_ref, hb_ref), refs = refs[:2], refs[2:]
    if emit_state:
        (co_ref, no_ref, mo_ref), refs = refs[:3], refs[3:]
    c_sc, n_sc, m_sc = refs
    c = pl.program_id(1)

    @pl.when(c == 0)
    def _():
        if with_init:
            c_sc[...] = c0_ref[...]
            n_sc[...] = n0_ref[...]
            m_sc[...] = m0_ref[...]
        else:
            c_sc[...] = jnp.zeros_like(c_sc)
            n_sc[...] = jnp.zeros_like(n_sc)
            m_sc[...] = jnp.zeros_like(m_sc)

    seqs = range(c_sc.shape[0])
    gates = [_ml_gates(fwd_refs, bwd_refs, m_sc, bi) for bi in seqs]
    for d, (refs_d, h_ref) in enumerate(((fwd_refs, hf_ref), (bwd_refs, hb_ref))):
        for hd in range(ML_HEADS):
            for bi in seqs:
                _ml_chain(refs_d, h_ref, c_sc, n_sc, gates[bi], bi, d, hd)

    if emit_state:
        @pl.when(c == pl.num_programs(1) - 1)
        def _():
            co_ref[...] = c_sc[...]
            no_ref[...] = n_sc[...]
            mo_ref[...] = m_sc[...]


def _ml_scan(q, k, kt, v, gates, nb, seq, init=None, emit_state=False, group=ML_SEQ_GROUP):
    nc = seq // ML_CHUNK
    per_chunk = lambda a: a.reshape(nb, nc, *a.shape[-2:]) if a.ndim == 3 else a.reshape(nb, nc, ML_CHUNK, -1)
    fwd = lambda b, c: (b, c, 0, 0)
    bwd = lambda b, c: (b, nc - 1 - c, 0, 0)
    blk = lambda a, im: pl.BlockSpec((group, None) + a.shape[2:], im)
    st_c = pl.BlockSpec((group, N_STATE, ML_DK, ML_DK), lambda b, c: (b, 0, 0, 0))
    st_n = pl.BlockSpec((group, N_STATE, ML_DK), lambda b, c: (b, 0, 0))
    st_m = pl.BlockSpec((group, 1, LANE), lambda b, c: (b, 0, 0))
    arrays = [per_chunk(a) for a in (q, k, kt, v, gates)]
    in_specs = [blk(a, fwd) for a in arrays] + [blk(a, bwd) for a in arrays]
    args = arrays * 2
    if init is not None:
        in_specs += [st_c, st_n, st_m]
        args += list(init)
    h_shape = jax.ShapeDtypeStruct((nb, nc, ML_CHUNK, D), BF16)
    out_specs = [blk(h_shape, fwd), blk(h_shape, bwd)]
    out_shape = [h_shape] * 2
    if emit_state:
        out_specs += [st_c, st_n, st_m]
        out_shape += [jax.ShapeDtypeStruct((nb, N_STATE, ML_DK, ML_DK), F32),
                      jax.ShapeDtypeStruct((nb, N_STATE, ML_DK), F32),
                      jax.ShapeDtypeStruct((nb, 1, LANE), F32)]
    outs = pl.pallas_call(
        functools.partial(_ml_scan_kernel, with_init=init is not None, emit_state=emit_state),
        grid=(nb // group, nc), in_specs=in_specs, out_specs=out_specs, out_shape=out_shape,
        scratch_shapes=[pltpu.VMEM((group, N_STATE, ML_DK, ML_DK), F32),
                        pltpu.VMEM((group, N_STATE, ML_DK), F32), pltpu.VMEM((group, 1, LANE), F32)],
        compiler_params=_params(2, last_arbitrary=True), name="mlstm_scan",
    )(*args)
    return [outs[0].reshape(nb * seq, D), outs[1].reshape(nb * seq, D)] + list(outs[2:])


def _ml_post_kernel(x_ref, hf_ref, hb_ref, og_ref, mods_ref, hg_ref, w_ref, o_ref):
    for rows in _row_groups(x_ref.shape[0]):
        hsum = hf_ref[rows, :].astype(F32) + hb_ref[rows, :].astype(F32)
        hn = jnp.concatenate(
            [_rms(hsum[:, hd * ML_DK:(hd + 1) * ML_DK]) for hd in range(ML_HEADS)], axis=1)
        y = _dot((og_ref[rows, :].astype(F32) * (hn * hg_ref[...])).astype(BF16), w_ref[...])
        o_ref[rows, :] = x_ref[rows, :] + _gate(mods_ref, 1) * y


def _ml_post(x, hf, hb, og, mods, head_g, w_out, layer, off, n, tm=512):
    tok = pl.BlockSpec((tm, D), lambda i: (i, 0))
    return pl.pallas_call(
        _ml_post_kernel,
        grid=(n,),
        in_specs=[_x_spec(tm, off), tok, tok, tok, _mods_spec(layer, tm, off),
                  _resident((1, D)), _resident((D, D))],
        out_specs=_x_spec(tm, off),
        out_shape=jax.ShapeDtypeStruct(x.shape, F32),
        input_output_aliases={0: 0}, compiler_params=_params(1), name="mlstm_post",
    )(x, hf, hb, og, mods, head_g.reshape(1, D), w_out)


def _fn_chan_kernel(x_ref, mods_ref, g_ref, cc_ref, sc_ref, a_ref, b_ref):
    hb = _mod_norm(x_ref[...], g_ref, mods_ref, 1).astype(BF16)
    for gi in range(FOURIER_GROUPS):
        sl = slice(gi * FG, (gi + 1) * FG)
        a_ref[:, sl] = _dot(hb[:, sl], cc_ref[...]).astype(BF16)
        b_ref[:, sl] = _dot(hb[:, sl], sc_ref[...]).astype(BF16)


def _fn_chan(x, mods, g, cc, sc, layer, off, n, tm=512):
    tok = pl.BlockSpec((tm, D), lambda i: (i, 0))
    return pl.pallas_call(
        _fn_chan_kernel,
        grid=(n,),
        in_specs=[_x_spec(tm, off), _mods_spec(layer, tm, off), _resident((1, D)),
                  _resident((FG, FG)), _resident((FG, FG))],
        out_specs=[tok, tok],
        out_shape=[jax.ShapeDtypeStruct((n * tm, D), BF16)] * 2,
        compiler_params=_params(1), name="fourier_chan",
    )(x, mods, g.reshape(1, D), cc, sc)


def _fn_seq_kernel(x_ref, a_ref, b_ref, cs_ref, ss_ref, mods_ref, w_ref, bias_ref, o_ref, *, scale):
    f = (_dot(cs_ref[...], a_ref[...]) - _dot(ss_ref[...], b_ref[...])) * scale
    y = _dot(f.astype(BF16), w_ref[...]) + bias_ref[...]
    o_ref[...] = x_ref[...] + _gate(mods_ref, 1) * y


def _fn_seq(x, a, b, cs, ss, mods, w_out, b_out, layer, off_tok, nb, seq, tr):
    nt = seq // tr
    off = off_tok // tr
    xs = pl.BlockSpec((tr, D), lambda bi, t: (off + bi * nt + t, 0))
    ab = pl.BlockSpec((seq, D), lambda bi, t: (bi, 0))
    tbl = pl.BlockSpec((tr, seq), lambda bi, t: (t, 0))
    mods_spec = pl.BlockSpec(
        (None, 9, D), lambda bi, t: (layer * N_COND + _tile_cond(off + bi * nt + t, tr), 0, 0))
    return pl.pallas_call(
        functools.partial(_fn_seq_kernel, scale=float((seq * FG) ** -0.5)),
        grid=(nb, nt),
        in_specs=[xs, ab, ab, tbl, tbl, mods_spec, _resident((D, D)), _resident((1, D))],
        out_specs=xs,
        out_shape=jax.ShapeDtypeStruct(x.shape, F32),
        input_output_aliases={0: 0}, compiler_params=_params(2), name="fourier_seq",
    )(x, a, b, cs, ss, mods, w_out, b_out.reshape(1, D))


def _fn_fused_kernel(x_ref, mods_ref, g_ref, cc_ref, sc_ref, cs_ref, ss_ref, w_ref, bias_ref, o_ref,
                     a_sc, b_sc, *, seq, scale):
    hb = _mod_norm(x_ref[...], g_ref, mods_ref, 1).astype(BF16)
    for gi in range(FOURIER_GROUPS):
        sl = slice(gi * FG, (gi + 1) * FG)
        a_sc[:, sl] = _dot(hb[:, sl], cc_ref[...]).astype(BF16)
        b_sc[:, sl] = _dot(hb[:, sl], sc_ref[...]).astype(BF16)
    for r0 in range(0, x_ref.shape[0], seq):
        rows = slice(r0, r0 + seq)
        f = (_dot(cs_ref[...], a_sc[rows, :]) - _dot(ss_ref[...], b_sc[rows, :])) * scale
        y = _dot(f.astype(BF16), w_ref[...]) + bias_ref[...]
        o_ref[rows, :] = x_ref[rows, :] + _gate(mods_ref, 1) * y


def _fn_fused(x, mods, g, cc, sc, cs, ss, w_out, b_out, layer, off, n, seq, tm=512):
    return pl.pallas_call(
        functools.partial(_fn_fused_kernel, seq=seq, scale=float((seq * FG) ** -0.5)),
        grid=(n,),
        in_specs=[_x_spec(tm, off), _mods_spec(layer, tm, off), _resident((1, D)),
                  _resident((FG, FG)), _resident((FG, FG)), _resident((seq, seq)), _resident((seq, seq)),
                  _resident((D, D)), _resident((1, D))],
        out_specs=_x_spec(tm, off),
        out_shape=jax.ShapeDtypeStruct(x.shape, F32),
        scratch_shapes=[pltpu.VMEM((tm, D), BF16), pltpu.VMEM((tm, D), BF16)],
        input_output_aliases={0: 0}, compiler_params=_params(1), name="fourier_fused",
    )(x, mods, g.reshape(1, D), cc, sc, cs, ss, w_out, b_out.reshape(1, D))


def _dft_tables(n):
    ang = 2.0 * np.pi * ((np.arange(n)[:, None] * np.arange(n)[None, :]) % n) / n
    return (jnp.asarray(np.cos(ang), F32).astype(BF16), jnp.asarray(np.sin(ang), F32).astype(BF16))


def _gm_kernel(x_ref, mods_ref, g_ref, win_ref, bin_ref, vg_ref, ws_ref, bs_ref, wout_ref, o_ref, sv_sc):
    gw = GM_W // GM_GROUPS
    groups = _row_groups(x_ref.shape[0], 2 * GM_CHUNK)
    zs = [_dot(_mod_norm(x_ref[rows, :], g_ref, mods_ref, 1).astype(BF16), win_ref[...]) + bin_ref[...]
          for rows in groups]
    us = []
    for rows, z in zip(groups, zs):
        z = z * (0.5 * (1.0 + jnp.tanh(np.sqrt(2.0 / np.pi) * (z + 0.044715 * (z * z * z)))))
        us.append(z[:, :GM_W])
        v = (_rms(z[:, GM_W:]) * vg_ref[...]).astype(BF16)
        for ch in range(2):
            crows = slice(ch * GM_CHUNK, (ch + 1) * GM_CHUNK)
            for gi in range(GM_GROUPS):
                cols = slice(gi * gw, (gi + 1) * gw)
                sv_sc[rows.start + ch * GM_CHUNK:rows.start + (ch + 1) * GM_CHUNK, cols] = (
                    _dot(ws_ref[gi], v[crows, cols]) + bs_ref[:, gi:gi + 1])
    for rows, u in zip(groups, us):
        y = _dot((u * sv_sc[rows, :]).astype(BF16), wout_ref[...])
        o_ref[rows, :] = x_ref[rows, :] + _gate(mods_ref, 1) * y


def _gmlp(x, mods, g, w_in, b_in, v_g, w_s, b_s_t, w_out, layer, tm=512):
    return pl.pallas_call(
        _gm_kernel,
        grid=(N_TOK // tm,),
        in_specs=[_x_spec(tm, 0), _mods_spec(layer, tm, 0), _resident((1, D)),
                  _resident((D, 2 * GM_W)), _resident((1, 2 * GM_W)), _resident((1, GM_W)),
                  _resident((GM_GROUPS, GM_CHUNK, GM_CHUNK)), _resident((GM_CHUNK, GM_GROUPS)),
                  _resident((GM_W, D))],
        out_specs=_x_spec(tm, 0),
        out_shape=jax.ShapeDtypeStruct(x.shape, F32),
        scratch_shapes=[pltpu.VMEM((tm, GM_W), F32)],
        input_output_aliases={0: 0}, compiler_params=_params(1), name="gmlp",
    )(x, mods, g.reshape(1, D), w_in, b_in.reshape(1, 2 * GM_W), v_g.reshape(1, GM_W), w_s, b_s_t, w_out)


def _na_pre_kernel(x_ref, mods_ref, g_ref, w_ref, q_ref, k_ref, v_ref, *f32_refs):
    for rows in _row_groups(x_ref.shape[0], SEQ):
        hb = _mod_norm(x_ref[rows, :], g_ref, mods_ref, 1).astype(BF16)
        qkv = _dot(hb, w_ref[...])
        q_ref[rows, :] = (qkv[:, :D] * (NA_HD ** -0.5)).astype(BF16)
        k, v = qkv[:, D:2 * D], qkv[:, 2 * D:]
        k_ref[rows, :] = k.astype(BF16)
        v_ref[rows, :] = v.astype(BF16)
        if f32_refs:
            f32_refs[0][rows.start // SEQ] = k.T
            f32_refs[1][rows.start // SEQ] = v.T


def _na_pre(x, mods, g, w_qkv, layer, off, n, emit_f32, tm=512):
    tok = pl.BlockSpec((tm, D), lambda i: (i, 0))
    out_specs = [tok, tok, tok]
    out_shape = [jax.ShapeDtypeStruct((n * tm, D), BF16)] * 3
    if emit_f32:
        assert tm % SEQ == 0
        out_specs += [pl.BlockSpec((tm // SEQ, D, SEQ), lambda i: (i, 0, 0))] * 2
        out_shape += [jax.ShapeDtypeStruct((n * tm // SEQ, D, SEQ), F32)] * 2
    return pl.pallas_call(
        _na_pre_kernel,
        grid=(n,),
        in_specs=[_x_spec(tm, off), _mods_spec(layer, tm, off), _resident((1, D)), _resident((D, 3 * D))],
        out_specs=out_specs, out_shape=out_shape,
        compiler_params=_params(1), name="na_qkv",
    )(x, mods, g.reshape(1, D), w_qkv)


def _half_mask(shape, e):
    lane = lax.broadcasted_iota(jnp.int32, shape, len(shape) - 1)
    return (lane < NA_HD) if e == 0 else (lane >= NA_HD)


def _ctx_attn_kernel(x_ref, q_ref, k_ref, v_ref, mods_ref, w_ref, o_ref, att_sc):
    for j in range(NA_PAIRS):
        sl = slice(j * LANE, (j + 1) * LANE)
        s = _dot_nt(_pair_rows(q_ref[:, sl]), k_ref[:, sl])
        p = jnp.exp(s - jnp.max(s, axis=1, keepdims=True))
        o = _dot(p.astype(BF16), v_ref[:, sl]) / jnp.sum(p, axis=1, keepdims=True)
        att_sc[:, sl] = jnp.where(_half_mask((SEQ, LANE), 0), o[:SEQ], o[SEQ:]).astype(BF16)
    o_ref[...] = x_ref[...] + _gate(mods_ref, 1) * _dot(att_sc[...], w_ref[...])


def _ctx_attn(x, q, k, v, mods, w_out, layer):
    tok = pl.BlockSpec((SEQ, D), lambda b: (b, 0))
    return pl.pallas_call(
        _ctx_attn_kernel,
        grid=(BATCH,),
        in_specs=[tok, tok, tok, tok, _mods_spec(layer, SEQ, 0), _resident((D, D))],
        out_specs=tok,
        out_shape=jax.ShapeDtypeStruct(x.shape, F32),
        scratch_shapes=[pltpu.VMEM((SEQ, D), BF16)],
        input_output_aliases={0: 0}, compiler_params=_params(1), name="ctx_attn",
    )(x, q, k, v, mods, w_out)


def _na_bias_kernel(rpb_ref, o_ref, pair_sc):
    h = pl.program_id(0)
    shape = (GRID_W, LANE)
    q = lax.broadcasted_iota(jnp.int32, shape, 0)
    lane = lax.broadcasted_iota(jnp.int32, shape, 1)
    x = lane & (GRID_W - 1)
    first = lane < GRID_W
    dc = x - q + (NA_KW - 1)
    q_start = jnp.clip(q - NA_KW // 2, 0, GRID_W - NA_KW)
    in_window = (x >= q_start) & (x < q_start + NA_KW)
    n_dc = 2 * NA_KW - 1
    for dr in range(2 * NA_KH - 2):
        acc = jnp.zeros(shape, F32)
        for j in range(n_dc):
            val = jnp.where(first, rpb_ref[h, dr * n_dc + j], rpb_ref[h, (dr + 1) * n_dc + j])
            acc = jnp.where(dc == j, val, acc)
        pair_sc[dr] = jnp.where(in_window, acc, NEG_INF)
    for o in range(NA_KH):
        for t in range(NA_KH // 2):
            o_ref[o, :, t * LANE:(t + 1) * LANE] = pair_sc[NA_KH - 1 - o + 2 * t]


def _na_bias_tables(rpb):
    n_rel = (2 * NA_KH - 1) * (2 * NA_KW - 1)
    t = pl.pallas_call(
        _na_bias_kernel,
        grid=(NA_HEADS,),
        in_specs=[pl.BlockSpec(memory_space=pltpu.SMEM)],
        out_specs=pl.BlockSpec((NA_KH, None, GRID_W, NA_WIN), lambda h: (0, h, 0, 0)),
        out_shape=jax.ShapeDtypeStruct((NA_KH, NA_HEADS, GRID_W, NA_WIN), F32),
        scratch_shapes=[pltpu.VMEM((2 * NA_KH - 2, GRID_W, LANE), F32)],
        compiler_params=_params(1), name="na_bias",
    )(rpb.reshape(NA_HEADS, n_rel))
    return t.reshape(NA_KH, NA_HEADS * GRID_W, NA_WIN)


def _pair_rows(qp):
    zero = jnp.zeros_like(qp)
    return jnp.concatenate([jnp.where(_half_mask(qp.shape, 0), qp, zero),
                            jnp.where(_half_mask(qp.shape, 1), qp, zero)], axis=0)


def _na_lat_kernel(q_ref, k_ref, v_ref, kc_ref, vc_ref, bias_ref, o_ref, p_sc):
    r = pl.program_id(1)
    start = pl.multiple_of(jnp.clip(r - NA_KH // 2, 0, NA_ROWS - NA_KH) * GRID_W, GRID_W)
    inv = []
    for j in range(NA_PAIRS):
        sl = slice(j * LANE, (j + 1) * LANE)
        qb = _pair_rows(q_ref[:, sl])
        s_w = _dot_nt(qb, k_ref[pl.ds(start, NA_WIN), sl]) + bias_ref[sl, :]
        s_c = _dot_nt(qb, kc_ref[:, sl])
        m = jnp.maximum(jnp.max(s_w, axis=1, keepdims=True), jnp.max(s_c, axis=1, keepdims=True))
        p_w = jnp.exp(s_w - m)
        p_c = jnp.exp(s_c - m)
        inv.append(1.0 / (jnp.sum(p_w, axis=1, keepdims=True) + jnp.sum(p_c, axis=1, keepdims=True)))
        p_sc[sl, :NA_WIN] = p_w.astype(BF16)
        p_sc[sl, NA_WIN:] = p_c.astype(BF16)
    for j in range(NA_PAIRS):
        sl = slice(j * LANE, (j + 1) * LANE)
        o = (_dot(p_sc[sl, :NA_WIN], v_ref[pl.ds(start, NA_WIN), sl])
             + _dot(p_sc[sl, NA_WIN:], vc_ref[:, sl])) * inv[j]
        o_ref[:, sl] = jnp.where(_half_mask((GRID_W, LANE), 0), o[:GRID_W], o[GRID_W:]).astype(BF16)


def _na_latent(q, k, v, kc, vc, bias):
    row_class = lambda r: r - jnp.clip(r - NA_KH // 2, 0, NA_ROWS - NA_KH)
    seq_kv = pl.BlockSpec((None, DEC_SEQ, D), lambda b, r: (b, 0, 0))
    ctx_kv = pl.BlockSpec((None, PAST_LEN, D), lambda b, r: (b, 0, 0))
    n_rows = NA_HEADS * GRID_W
    return pl.pallas_call(
        _na_lat_kernel,
        grid=(DEC_BATCH, NA_ROWS),
        in_specs=[pl.BlockSpec((GRID_W, D), lambda b, r: (b * NA_ROWS + r, 0)),
                  seq_kv, seq_kv, ctx_kv, ctx_kv,
                  pl.BlockSpec((None, n_rows, NA_WIN), lambda b, r: (row_class(r), 0, 0))],
        out_specs=pl.BlockSpec((GRID_W, D), lambda b, r: (b * NA_ROWS + r, 0)),
        out_shape=jax.ShapeDtypeStruct((NS_TOK, D), BF16),
        scratch_shapes=[pltpu.VMEM((n_rows, NA_WIN + PAST_LEN), BF16)],
        compiler_params=_params(2), name="na_latent",
    )(q, k.reshape(DEC_BATCH, DEC_SEQ, D), v.reshape(DEC_BATCH, DEC_SEQ, D), kc, vc, bias)


def kernel(x_prompt, x_sample, state_mlstm_C, state_mlstm_n, state_mlstm_m, cache_na_k, cache_na_v, c, c_ctx, w_ada, b_ada, norm_g, final_g, ffn_w1, ffn_w3, ffn_w2, ml_w_qkv, ml_w_if, ml_b_if, ml_w_og, ml_head_g, ml_w_out, fn_w_out, fn_b_out, gm_w_in, gm_b_in, gm_v_g, gm_w_s, gm_b_s, gm_w_out, na_w_qkv, na_w_out, na_rpb):
    tm = 512
    n_p, n_s, n_all = NP_TOK // tm, NS_TOK // tm, N_TOK // tm
    x = (x_prompt.reshape(NP_TOK, D), x_sample.reshape(NS_TOK, D))

    cond = jnp.zeros((N_COND, D), F32).at[0].set(c_ctx).at[1:1 + DEC_BATCH].set(c)
    mods = _adaln(cond, w_ada, b_ada).reshape(DEPTH * N_COND, 9, D)

    ffn_f32 = (ffn_w1, ffn_w3, ffn_w2)
    wb = tuple(w[0, 0].astype(BF16) for w in ffn_f32)
    outs = {}
    for l in range(DEPTH):
        kind, j = l % 4, l // 4
        x, wb = _ffn(x, mods, norm_g[l, 0], wb, l, 0, 0, N_TOK // FFN_TILE, next_f32=(*ffn_f32, l, 1))
        g = norm_g[l, 1]
        if kind == 0:
            w_dir = jnp.transpose(ml_w_if[j], (1, 0, 2))
            lane_pad = ((0, 0), (0, LANE - N_STATE))
            wif = jnp.concatenate(
                [jnp.pad(w_dir[:, :, :ML_HEADS].reshape(D, N_STATE), lane_pad),
                 jnp.pad(w_dir[:, :, ML_HEADS:].reshape(D, N_STATE), lane_pad)], axis=1)
            bif = jnp.concatenate(
                [jnp.pad(ml_b_if[j][:, :ML_HEADS].reshape(1, N_STATE), lane_pad),
                 jnp.pad(ml_b_if[j][:, ML_HEADS:].reshape(1, N_STATE), lane_pad)], axis=1)
            wqkv, wog, wout = ml_w_qkv[j].astype(BF16), ml_w_og[j].astype(BF16), ml_w_out[j].astype(BF16)
            for off, n, nb, seq in ((0, n_p, BATCH, SEQ), (n_p, n_s, DEC_BATCH, DEC_SEQ)):
                q, k, kt, v, og, gates = _ml_pre(x, mods, g, wqkv, wog, wif.astype(BF16), bif, l, off, n)
                if off == 0:
                    hf, hb, c_new, n_new, m_new = _ml_scan(q, k, kt, v, gates, nb, seq, emit_state=True)
                    outs["C"] = c_new.reshape(BATCH, 1, 2, ML_HEADS, ML_DK, ML_DK)
                    outs["n"] = n_new.reshape(BATCH, 1, 2, ML_HEADS, ML_DK)
                    outs["m"] = m_new[:, 0, :N_STATE].reshape(BATCH, 1, 2, ML_HEADS)
                else:
                    init = (state_mlstm_C[:, j].reshape(DEC_BATCH, N_STATE, ML_DK, ML_DK),
                            state_mlstm_n[:, j].reshape(DEC_BATCH, N_STATE, ML_DK),
                            jnp.pad(state_mlstm_m[:, j].reshape(DEC_BATCH, 1, N_STATE),
                                    ((0, 0), (0, 0), (0, LANE - N_STATE))))
                    hf, hb = _ml_scan(q, k, kt, v, gates, nb, seq, init=init)
                x = _ml_post(x, hf, hb, og, mods, ml_head_g[j], wout, l, off, n)
        elif kind == 1:
            cc, sc = _dft_tables(FG)
            wout = fn_w_out[j].astype(BF16)
            cs, ss = _dft_tables(SEQ)
            x = _fn_fused(x, mods, g, cc, sc, cs, ss, wout, fn_b_out[j], l, 0, n_p, SEQ)
            a, b = _fn_chan(x, mods, g, cc, sc, l, n_p, n_s)
            cs, ss = _dft_tables(DEC_SEQ)
            x = _fn_seq(x, a, b, cs, ss, mods, wout, fn_b_out[j], l, NP_TOK, DEC_BATCH, DEC_SEQ, 512)
        elif kind == 2:
            x = _gmlp(x, mods, g, gm_w_in[j].astype(BF16), gm_b_in[j], gm_v_g[j],
                      gm_w_s[j].astype(BF16), gm_b_s[j].T, gm_w_out[j].astype(BF16), l)
        else:
            wqkv, wout = na_w_qkv[j].astype(BF16), na_w_out[j].astype(BF16)
            q, k, v, k_heads, v_heads = _na_pre(x, mods, g, wqkv, l, 0, n_p, True)
            per_head = lambda t: jnp.transpose(t.reshape(BATCH, 1, NA_HEADS, NA_HD, SEQ), (0, 1, 4, 2, 3))
            outs["k"], outs["v"] = per_head(k_heads), per_head(v_heads)
            x = _ctx_attn(x, q, k, v, mods, wout, l)
            q, k, v = _na_pre(x, mods, g, wqkv, l, n_p, n_s, False)
            att = _na_latent(q, k, v,
                             cache_na_k[:, j].reshape(DEC_BATCH, PAST_LEN, D).astype(BF16),
                             cache_na_v[:, j].reshape(DEC_BATCH, PAST_LEN, D).astype(BF16),
                             _na_bias_tables(na_rpb[j]))
            x = _proj_residual(x, att, mods, wout, l, n_p, n_s)
        if l < DEPTH - 1:
            x, wb = _ffn(x, mods, norm_g[l, 2], wb, l, 1, 0, N_TOK // FFN_TILE, next_f32=(*ffn_f32, l + 1, 0))
        else:
            y_p = _ffn(x, mods, norm_g[l, 2], wb, l, 1, 0, NP_TOK // FFN_TILE, final_g=final_g)
            y_s = _ffn(x, mods, norm_g[l, 2], wb, l, 1, NP_TOK // FFN_TILE, NS_TOK // FFN_TILE, final_g=final_g)
    return (y_p.reshape(BATCH, SEQ, D), y_s.reshape(DEC_BATCH, DEC_SEQ, D),
            outs["C"], outs["n"], outs["m"], outs["k"], outs["v"])
```

```python
import functools

import numpy as np
import jax
import jax.numpy as jnp
from jax import lax
from jax.experimental import pallas as pl
from jax.experimental.pallas import tpu as pltpu

D = 1024
BATCH = 32
SEQ = 256
DEPTH = 4
DEC_BATCH = 2
DEC_SEQ = 2048
PAST_LEN = 512
GRID_W = 64
D_FF = 2816
EPS = 1e-6

NP_TOK = BATCH * SEQ
NS_TOK = DEC_BATCH * DEC_SEQ
N_TOK = NP_TOK + NS_TOK

ML_HEADS = 4
ML_DK = D // ML_HEADS
ML_CHUNK = 128
N_STATE = 2 * ML_HEADS
ML_SEQ_GROUP = 2

FOURIER_GROUPS = 4
FG = D // FOURIER_GROUPS

GM_W = D
GM_GROUPS = 4
GM_CHUNK = 128

NA_HEADS = 16
NA_HD = D // NA_HEADS
NA_KH = 8
NA_KW = 16
NA_ROWS = DEC_SEQ // GRID_W
NA_WIN = NA_KH * GRID_W
NA_PAIRS = NA_HEADS // 2
N_COND = 8

LANE = 128
BF16_ROWS = 16
TOK_TILE = 512
ROW_GROUP = 256
ADALN_COLS = 2304
FFN_ROWS = 256
FFN_TILE = 1024
FFN_CAST_CHUNKS = 8
VMEM_LIMIT = 60 * 1024 * 1024

F32 = jnp.float32
BF16 = jnp.bfloat16
NEG_INF = float("-inf")


def _params(n_axes, last_arbitrary=False):
    sem = ["parallel"] * n_axes
    if last_arbitrary:
        sem[-1] = "arbitrary"
    return pltpu.CompilerParams(dimension_semantics=tuple(sem), vmem_limit_bytes=VMEM_LIMIT)


def _resident(shape):
    zeros = (0,) * len(shape)
    return pl.BlockSpec(shape, lambda *_: zeros, pipeline_mode=pl.Buffered(1))


def _tile_cond(i, tm):
    row0 = i * tm
    return jnp.where(row0 < NP_TOK, 0, 1 + (row0 - NP_TOK) // DEC_SEQ)


def _x_spec(tm, off):
    return pl.BlockSpec((tm, D), lambda i: (i + off, 0))


def _mods_spec(layer, tm, off):
    return pl.BlockSpec((None, 9, D), lambda i: (layer * N_COND + _tile_cond(i + off, tm), 0, 0))


def _dot(a, b):
    return jnp.dot(a, b, preferred_element_type=F32)


def _dot_nt(a, b):
    return lax.dot_general(a, b, (((1,), (1,)), ((), ())), preferred_element_type=F32)


def _rms(x):
    return x * lax.rsqrt(jnp.mean(x * x, axis=-1, keepdims=True) + EPS)


def _mod_norm(x, g_ref, mods_ref, idx):
    h = _rms(x) * g_ref[...]
    return h * (1.0 + mods_ref[3 * idx + 1:3 * idx + 2, :]) + mods_ref[3 * idx:3 * idx + 1, :]


def _gate(mods_ref, idx):
    return mods_ref[3 * idx + 2:3 * idx + 3, :]


def _row_groups(n_rows, group=ROW_GROUP):
    return [slice(r0, r0 + group) for r0 in range(0, n_rows, group)]


def _adaln_kernel(c_ref, w_ref, b_ref, o_ref):
    c = c_ref[...]
    s = (c * jax.nn.sigmoid(c)).astype(BF16)
    o_ref[...] = _dot(s, w_ref[...].astype(BF16)) + b_ref[...]


def _adaln(cond, w_ada, b_ada):
    tn = ADALN_COLS
    nj = 9 * D // tn
    return pl.pallas_call(
        _adaln_kernel,
        grid=(DEPTH, nj),
        in_specs=[
            pl.BlockSpec((N_COND, D), lambda l, j: (0, 0)),
            pl.BlockSpec((None, D, tn), lambda l, j: (l, 0, j)),
            pl.BlockSpec((None, 1, tn), lambda l, j: (l, 0, j)),
        ],
        out_specs=pl.BlockSpec((None, N_COND, tn), lambda l, j: (l, 0, j)),
        out_shape=jax.ShapeDtypeStruct((DEPTH, N_COND, 9 * D), F32),
        compiler_params=_params(2),
        name="adaln",
    )(cond, w_ada, b_ada.reshape(DEPTH, 1, 9 * D))


def _ffn_kernel(*refs, idx, final, convert_next, split_at):
    if split_at is None:
        x_ref, refs = refs[0], refs[1:]
        load_x = lambda rows: x_ref[rows, :]
    else:
        (xa_ref, xb_ref), refs = refs[:2], refs[2:]
        first = pl.program_id(0) < split_at
        load_x = lambda rows: jnp.where(first, xa_ref[rows, :], xb_ref[rows, :])
    mods_ref, g_ref, w1_ref, w3_ref, w2_ref = refs[:5]
    rest = refs[5:]
    if convert_next:
        (nw1_ref, nw3_ref, nw2_ref), rest = rest[:3], rest[3:]
        cast_out, rest = rest[-3:], rest[:-3]
    o_ref = rest[-1]
    for rows in _row_groups(o_ref.shape[0], FFN_ROWS):
        x = load_x(rows)
        hb = _mod_norm(x, g_ref, mods_ref, idx).astype(BF16)
        a = _dot(hb, w1_ref[...])
        b = _dot(hb, w3_ref[...])
        act = (a * jax.nn.sigmoid(a) * b).astype(BF16)
        y = x + (0.5 * _gate(mods_ref, idx)) * _dot(act, w2_ref[...])
        if final:
            y = _rms(y) * rest[0][...]
        o_ref[rows, :] = y
    if convert_next:
        for src, dst in zip((nw1_ref, nw3_ref, nw2_ref), cast_out):
            dst[...] = src[...].astype(BF16)


def _ffn(x, mods, g, wb, layer, f, off, n, final_g=None, next_f32=None, tm=FFN_TILE):
    final = final_g is not None
    convert_next = next_f32 is not None
    idx = 2 * f
    split_at = None
    if isinstance(x, tuple):
        xa, xb = x
        split_at = xa.shape[0] // tm
        assert not final and off == 0 and xa.shape[0] % tm == 0 and n * tm == xa.shape[0] + xb.shape[0]
        x_specs = [pl.BlockSpec((tm, D), lambda i: (jnp.minimum(i, split_at - 1), 0)),
                   pl.BlockSpec((tm, D), lambda i: (jnp.maximum(i - split_at, 0), 0))]
        x_args = [xa, xb]
    else:
        x_specs, x_args = [_x_spec(tm, off)], [x]
    in_specs = x_specs + [_mods_spec(layer, tm, off), _resident((1, D)),
                          _resident((D, D_FF)), _resident((D, D_FF)), _resident((D_FF, D))]
    args = x_args + [mods, g.reshape(1, D), *wb]
    if convert_next:
        nw1, nw3, nw2, nl, nf = next_f32
        chunk = lambda i: jnp.minimum(i, FFN_CAST_CHUNKS - 1)
        for w in (nw1, nw3, nw2):
            rows, cols = w.shape[2] // FFN_CAST_CHUNKS, w.shape[3]
            in_specs.append(pl.BlockSpec((None, None, rows, cols), lambda i: (nl, nf, chunk(i), 0)))
            args.append(w)
    if final:
        in_specs.append(_resident((1, D)))
        args.append(final_g.reshape(1, D))
        out_specs = [pl.BlockSpec((tm, D), lambda i: (i, 0))]
        out_shape = [jax.ShapeDtypeStruct((n * tm, D), F32)]
        aliases = {}
    elif split_at is not None:
        out_specs = [_x_spec(tm, 0)]
        out_shape = [jax.ShapeDtypeStruct((n * tm, D), F32)]
        aliases = {}
    else:
        out_specs = [_x_spec(tm, off)]
        out_shape = [jax.ShapeDtypeStruct(x.shape, F32)]
        aliases = {0: 0}
    if convert_next:
        assert n >= FFN_CAST_CHUNKS
        for w in next_f32[:3]:
            rows, cols = w.shape[2] // FFN_CAST_CHUNKS, w.shape[3]
            out_specs.append(pl.BlockSpec((rows, cols), lambda i: (chunk(i), 0)))
            out_shape.append(jax.ShapeDtypeStruct(w.shape[2:], BF16))
    outs = pl.pallas_call(
        functools.partial(_ffn_kernel, idx=idx, final=final, convert_next=convert_next, split_at=split_at),
        grid=(n,), in_specs=in_specs, out_specs=out_specs, out_shape=out_shape,
        input_output_aliases=aliases, compiler_params=_params(1, last_arbitrary=True), name="ffn",
    )(*args)
    return (outs[0], tuple(outs[1:])) if convert_next else outs[0]


def _proj_kernel(x_ref, a_ref, mods_ref, w_ref, o_ref):
    y = _dot(a_ref[...].astype(BF16), w_ref[...])
    o_ref[...] = x_ref[...] + _gate(mods_ref, 1) * y


def _proj_residual(x, a, mods, w, layer, off, n, tm=TOK_TILE):
    return pl.pallas_call(
        _proj_kernel,
        grid=(n,),
        in_specs=[_x_spec(tm, off), pl.BlockSpec((tm, D), lambda i: (i, 0)),
                  _mods_spec(layer, tm, off), _resident((D, D))],
        out_specs=_x_spec(tm, off),
        out_shape=jax.ShapeDtypeStruct(x.shape, F32),
        input_output_aliases={0: 0}, compiler_params=_params(1), name="proj_residual",
    )(x, a, mods, w)


def _ml_pre_kernel(x_ref, mods_ref, g_ref, wqkv_ref, wog_ref, wif_ref, bif_ref,
                   q_ref, k_ref, kt_ref, v_ref, og_ref, gates_ref):
    for rows in _row_groups(x_ref.shape[0]):
        hb = _mod_norm(x_ref[rows, :], g_ref, mods_ref, 1).astype(BF16)
        qkv = _dot(hb, wqkv_ref[...])
        q_ref[rows, :] = qkv[:, :D].astype(BF16)
        k = qkv[:, D:2 * D] * (ML_DK ** -0.5)
        k_ref[rows, :] = k.astype(BF16)
        for ch in range(rows.start // ML_CHUNK, rows.stop // ML_CHUNK):
            kt_ref[ch] = k[ch * ML_CHUNK - rows.start:(ch + 1) * ML_CHUNK - rows.start, :].T.astype(BF16)
        v_ref[rows, :] = qkv[:, 2 * D:].astype(BF16)
        og_ref[rows, :] = jax.nn.sigmoid(_dot(hb, wog_ref[...])).astype(BF16)
        gates_ref[rows, :] = _dot(hb, wif_ref[...]) + bif_ref[...]


def _ml_pre(x, mods, g, wqkv, wog, wif, bif, layer, off, n, tm=TOK_TILE):
    tok = pl.BlockSpec((tm, D), lambda i: (i, 0))
    return pl.pallas_call(
        _ml_pre_kernel,
        grid=(n,),
        in_specs=[_x_spec(tm, off), _mods_spec(layer, tm, off), _resident((1, D)),
                  _resident((D, 3 * D)), _resident((D, D)), _resident((D, 2 * LANE)), _resident((1, 2 * LANE))],
        out_specs=[tok, tok, pl.BlockSpec((tm // ML_CHUNK, D, ML_CHUNK), lambda i: (i, 0, 0)), tok, tok,
                   pl.BlockSpec((tm, 2 * LANE), lambda i: (i, 0))],
        out_shape=[jax.ShapeDtypeStruct((n * tm, D), BF16)] * 2
        + [jax.ShapeDtypeStruct((n * tm // ML_CHUNK, D, ML_CHUNK), BF16), jax.ShapeDtypeStruct((n * tm, D), BF16),
           jax.ShapeDtypeStruct((n * tm, D), BF16), jax.ShapeDtypeStruct((n * tm, 2 * LANE), F32)],
        compiler_params=_params(1), name="mlstm_pre",
    )(x, mods, g.reshape(1, D), wqkv, wog, wif, bif)


def _log_sigmoid(x):
    return jnp.minimum(x, 0.0) - jnp.log1p(jnp.exp(-jnp.abs(x)))


def _scan_rows(x, fwd_lanes, op, identity):
    n = x.shape[0]
    row = lax.broadcasted_iota(jnp.int32, x.shape, 0)
    sh = 1
    while sh < n:
        prev = jnp.where(row >= sh, pltpu.roll(x, sh, 0), identity)
        nxt = jnp.where(row < n - sh, pltpu.roll(x, n - sh, 0), identity)
        x = op(x, jnp.where(fwd_lanes, prev, nxt))
        sh *= 2
    return x


def _ml_gates(fwd_refs, bwd_refs, m_sc, bi):
    L = ML_CHUNK
    lane = lax.broadcasted_iota(jnp.int32, (L, LANE), 1)
    fwd_lanes = lane < ML_HEADS
    gf_ref, gb_ref = fwd_refs[4], bwd_refs[4]
    i_pre = jnp.where(fwd_lanes, gf_ref[bi, :, :LANE], gb_ref[bi, :, :LANE])
    f_pre = jnp.where(fwd_lanes, gf_ref[bi, :, LANE:], gb_ref[bi, :, LANE:])
    log_f = jnp.where(lane < N_STATE, _log_sigmoid(f_pre), 0.0)
    bsum = _scan_rows(log_f, fwd_lanes, jnp.add, 0.0)
    rel = i_pre - bsum
    m_old = m_sc[bi]
    mm = jnp.maximum(m_old, _scan_rows(rel, fwd_lanes, jnp.maximum, NEG_INF))
    mx = jnp.maximum(m_old, jnp.max(rel, axis=0, keepdims=True))
    b_last = jnp.where(fwd_lanes[0:1], bsum[L - 1:L, :], bsum[0:1, :])
    m_sc[bi] = b_last + mx
    return dict(mm=mm, mx=mx,
                w_inter=jnp.exp(m_old - mm),
                floor=jnp.exp(-(bsum + mm)),
                decay=jnp.exp(m_old - mx),
                rel_t=rel.T)


def _ml_chain(refs, h_ref, c_sc, n_sc, gt, bi, d, hd):
    L = ML_CHUNK
    q_ref, k_ref, kt_ref, v_ref, _ = refs
    t_idx = lax.broadcasted_iota(jnp.int32, (L, L), 0)
    s_idx = lax.broadcasted_iota(jnp.int32, (L, L), 1)
    visible = (s_idx >= t_idx) if d == 1 else (s_idx <= t_idx)
    r = d * ML_HEADS + hd
    lo, hi = hd * ML_DK, (hd + 1) * ML_DK
    rel_row = gt["rel_t"][r:r + 1, :]
    w_col = gt["w_inter"][:, r:r + 1]
    n_old = n_sc[bi, r:r + 1, :]
    c_old = c_sc[bi, r]
    qh, kh, vh = q_ref[bi, :, lo:hi], k_ref[bi, :, lo:hi], v_ref[bi, :, lo:hi]
    kth = kt_ref[bi, lo:hi, :]

    a = jnp.exp(jnp.where(visible, rel_row - gt["mm"][:, r:r + 1], NEG_INF)) * _dot(qh, kth)
    num = _dot(a.astype(BF16), vh) + w_col * _dot(qh, c_old.astype(BF16))
    den = jnp.sum(a, axis=1, keepdims=True) + w_col * jnp.sum(qh.astype(F32) * n_old, axis=1, keepdims=True)
    h_ref[bi, :, lo:hi] = (num * (1.0 / jnp.maximum(jnp.abs(den), gt["floor"][:, r:r + 1]))).astype(BF16)

    w_row = jnp.exp(rel_row - gt["mx"][:, r:r + 1])
    dec = gt["decay"][:, r:r + 1]
    c_sc[bi, r] = dec * c_old + _dot((kth.astype(F32) * w_row).astype(BF16), vh)
    w_rows = jnp.broadcast_to(w_row, (BF16_ROWS, L)).astype(BF16)
    n_sc[bi, r:r + 1, :] = dec * n_old + _dot(w_rows, kh)[0:1, :]


def _ml_scan_kernel(*refs, with_init, emit_state):
    fwd_refs, bwd_refs, refs = refs[:5], refs[5:10], refs[10:]
    if with_init:
        (c0_ref, n0_ref, m0_ref), refs = refs[:3], refs[3:]
    (hf_ref, hb_ref), refs = refs[:2], refs[2:]
    if emit_state:
        (co_ref, no_ref, mo_ref), refs = refs[:3], refs[3:]
    c_sc, n_sc, m_sc = refs
    c = pl.program_id(1)

    @pl.when(c == 0)
    def _():
        if with_init:
            c_sc[...] = c0_ref[...]
            n_sc[...] = n0_ref[...]
            m_sc[...] = m0_ref[...]
        else:
            c_sc[...] = jnp.zeros_like(c_sc)
            n_sc[...] = jnp.zeros_like(n_sc)
            m_sc[...] = jnp.zeros_like(m_sc)

    seqs = range(c_sc.shape[0])
    gates = [_ml_gates(fwd_refs, bwd_refs, m_sc, bi) for bi in seqs]
    for d, (refs_d, h_ref) in enumerate(((fwd_refs, hf_ref), (bwd_refs, hb_ref))):
        for hd in range(ML_HEADS):
            for bi in seqs:
                _ml_chain(refs_d, h_ref, c_sc, n_sc, gates[bi], bi, d, hd)

    if emit_state:
        @pl.when(c == pl.num_programs(1) - 1)
        def _():
            co_ref[...] = c_sc[...]
            no_ref[...] = n_sc[...]
            mo_ref[...] = m_sc[...]


def _ml_scan(q, k, kt, v, gates, nb, seq, init=None, emit_state=False, group=ML_SEQ_GROUP):
    nc = seq // ML_CHUNK
    per_chunk = lambda a: a.reshape(nb, nc, *a.shape[-2:]) if a.ndim == 3 else a.reshape(nb, nc, ML_CHUNK, -1)
    fwd = lambda b, c: (b, c, 0, 0)
    bwd = lambda b, c: (b, nc - 1 - c, 0, 0)
    blk = lambda a, im: pl.BlockSpec((group, None) + a.shape[2:], im)
    st_c = pl.BlockSpec((group, N_STATE, ML_DK, ML_DK), lambda b, c: (b, 0, 0, 0))
    st_n = pl.BlockSpec((group, N_STATE, ML_DK), lambda b, c: (b, 0, 0))
    st_m = pl.BlockSpec((group, 1, LANE), lambda b, c: (b, 0, 0))
    arrays = [per_chunk(a) for a in (q, k, kt, v, gates)]
    in_specs = [blk(a, fwd) for a in arrays] + [blk(a, bwd) for a in arrays]
    args = arrays * 2
    if init is not None:
        in_specs += [st_c, st_n, st_m]
        args += list(init)
    h_shape = jax.ShapeDtypeStruct((nb, nc, ML_CHUNK, D), BF16)
    out_specs = [blk(h_shape, fwd), blk(h_shape, bwd)]
    out_shape = [h_shape] * 2
    if emit_state:
        out_specs += [st_c, st_n, st_m]
        out_shape += [jax.ShapeDtypeStruct((nb, N_STATE, ML_DK, ML_DK), F32),
                      jax.ShapeDtypeStruct((nb, N_STATE, ML_DK), F32),
                      jax.ShapeDtypeStruct((nb, 1, LANE), F32)]
    outs = pl.pallas_call(
        functools.partial(_ml_scan_kernel, with_init=init is not None, emit_state=emit_state),
        grid=(nb // group, nc), in_specs=in_specs, out_specs=out_specs, out_shape=out_shape,
        scratch_shapes=[pltpu.VMEM((group, N_STATE, ML_DK, ML_DK), F32),
                        pltpu.VMEM((group, N_STATE, ML_DK), F32), pltpu.VMEM((group, 1, LANE), F32)],
        compiler_params=_params(2, last_arbitrary=True), name="mlstm_scan",
    )(*args)
    return [outs[0].reshape(nb * seq, D), outs[1].reshape(nb * seq, D)] + list(outs[2:])


def _ml_post_kernel(x_ref, hf_ref, hb_ref, og_ref, mods_ref, hg_ref, w_ref, o_ref):
    for rows in _row_groups(x_ref.shape[0]):
        hsum = hf_ref[rows, :].astype(F32) + hb_ref[rows, :].astype(F32)
        hn = jnp.concatenate(
            [_rms(hsum[:, hd * ML_DK:(hd + 1) * ML_DK]) for hd in range(ML_HEADS)], axis=1)
        y = _dot((og_ref[rows, :].astype(F32) * (hn * hg_ref[...])).astype(BF16), w_ref[...])
        o_ref[rows, :] = x_ref[rows, :] + _gate(mods_ref, 1) * y


def _ml_post(x, hf, hb, og, mods, head_g, w_out, layer, off, n, tm=TOK_TILE):
    tok = pl.BlockSpec((tm, D), lambda i: (i, 0))
    return pl.pallas_call(
        _ml_post_kernel,
        grid=(n,),
        in_specs=[_x_spec(tm, off), tok, tok, tok, _mods_spec(layer, tm, off),
                  _resident((1, D)), _resident((D, D))],
        out_specs=_x_spec(tm, off),
        out_shape=jax.ShapeDtypeStruct(x.shape, F32),
        input_output_aliases={0: 0}, compiler_params=_params(1), name="mlstm_post",
    )(x, hf, hb, og, mods, head_g.reshape(1, D), w_out)


def _fn_chan_kernel(x_ref, mods_ref, g_ref, cc_ref, sc_ref, a_ref, b_ref):
    hb = _mod_norm(x_ref[...], g_ref, mods_ref, 1).astype(BF16)
    for gi in range(FOURIER_GROUPS):
        sl = slice(gi * FG, (gi + 1) * FG)
        a_ref[:, sl] = _dot(hb[:, sl], cc_ref[...]).astype(BF16)
        b_ref[:, sl] = _dot(hb[:, sl], sc_ref[...]).astype(BF16)


def _fn_chan(x, mods, g, cc, sc, layer, off, n, tm=TOK_TILE):
    tok = pl.BlockSpec((tm, D), lambda i: (i, 0))
    return pl.pallas_call(
        _fn_chan_kernel,
        grid=(n,),
        in_specs=[_x_spec(tm, off), _mods_spec(layer, tm, off), _resident((1, D)),
                  _resident((FG, FG)), _resident((FG, FG))],
        out_specs=[tok, tok],
        out_shape=[jax.ShapeDtypeStruct((n * tm, D), BF16)] * 2,
        compiler_params=_params(1), name="fourier_chan",
    )(x, mods, g.reshape(1, D), cc, sc)


def _fn_seq_kernel(x_ref, a_ref, b_ref, cs_ref, ss_ref, mods_ref, w_ref, bias_ref, o_ref, *, scale):
    f = (_dot(cs_ref[...], a_ref[...]) - _dot(ss_ref[...], b_ref[...])) * scale
    y = _dot(f.astype(BF16), w_ref[...]) + bias_ref[...]
    o_ref[...] = x_ref[...] + _gate(mods_ref, 1) * y


def _fn_seq(x, a, b, cs, ss, mods, w_out, b_out, layer, off_tok, nb, seq, tr):
    nt = seq // tr
    off = off_tok // tr
    xs = pl.BlockSpec((tr, D), lambda bi, t: (off + bi * nt + t, 0))
    ab = pl.BlockSpec((seq, D), lambda bi, t: (bi, 0))
    tbl = pl.BlockSpec((tr, seq), lambda bi, t: (t, 0))
    mods_spec = pl.BlockSpec(
        (None, 9, D), lambda bi, t: (layer * N_COND + _tile_cond(off + bi * nt + t, tr), 0, 0))
    return pl.pallas_call(
        functools.partial(_fn_seq_kernel, scale=float((seq * FG) ** -0.5)),
        grid=(nb, nt),
        in_specs=[xs, ab, ab, tbl, tbl, mods_spec, _resident((D, D)), _resident((1, D))],
        out_specs=xs,
        out_shape=jax.ShapeDtypeStruct(x.shape, F32),
        input_output_aliases={0: 0}, compiler_params=_params(2), name="fourier_seq",
    )(x, a, b, cs, ss, mods, w_out, b_out.reshape(1, D))


def _fn_fused_kernel(x_ref, mods_ref, g_ref, cc_ref, sc_ref, cs_ref, ss_ref, w_ref, bias_ref, o_ref,
                     a_sc, b_sc, *, seq, scale):
    hb = _mod_norm(x_ref[...], g_ref, mods_ref, 1).astype(BF16)
    for gi in range(FOURIER_GROUPS):
        sl = slice(gi * FG, (gi + 1) * FG)
        a_sc[:, sl] = _dot(hb[:, sl], cc_ref[...]).astype(BF16)
        b_sc[:, sl] = _dot(hb[:, sl], sc_ref[...]).astype(BF16)
    for r0 in range(0, x_ref.shape[0], seq):
        rows = slice(r0, r0 + seq)
        f = (_dot(cs_ref[...], a_sc[rows, :]) - _dot(ss_ref[...], b_sc[rows, :])) * scale
        y = _dot(f.astype(BF16), w_ref[...]) + bias_ref[...]
        o_ref[rows, :] = x_ref[rows, :] + _gate(mods_ref, 1) * y


def _fn_fused(x, mods, g, cc, sc, cs, ss, w_out, b_out, layer, off, n, seq, tm=TOK_TILE):
    return pl.pallas_call(
        functools.partial(_fn_fused_kernel, seq=seq, scale=float((seq * FG) ** -0.5)),
        grid=(n,),
        in_specs=[_x_spec(tm, off), _mods_spec(layer, tm, off), _resident((1, D)),
                  _resident((FG, FG)), _resident((FG, FG)), _resident((seq, seq)), _resident((seq, seq)),
                  _resident((D, D)), _resident((1, D))],
        out_specs=_x_spec(tm, off),
        out_shape=jax.ShapeDtypeStruct(x.shape, F32),
        scratch_shapes=[pltpu.VMEM((tm, D), BF16), pltpu.VMEM((tm, D), BF16)],
        input_output_aliases={0: 0}, compiler_params=_params(1), name="fourier_fused",
    )(x, mods, g.reshape(1, D), cc, sc, cs, ss, w_out, b_out.reshape(1, D))


def _dft_tables(n):
    ang = 2.0 * np.pi * ((np.arange(n)[:, None] * np.arange(n)[None, :]) % n) / n
    return (jnp.asarray(np.cos(ang), F32).astype(BF16), jnp.asarray(np.sin(ang), F32).astype(BF16))


def _gm_kernel(x_ref, mods_ref, g_ref, win_ref, bin_ref, vg_ref, ws_ref, bs_ref, wout_ref, o_ref, sv_sc):
    gw = GM_W // GM_GROUPS
    groups = _row_groups(x_ref.shape[0], 2 * GM_CHUNK)
    zs = [_dot(_mod_norm(x_ref[rows, :], g_ref, mods_ref, 1).astype(BF16), win_ref[...]) + bin_ref[...]
          for rows in groups]
    us = []
    for rows, z in zip(groups, zs):
        z = z * (0.5 * (1.0 + jnp.tanh(np.sqrt(2.0 / np.pi) * (z + 0.044715 * (z * z * z)))))
        us.append(z[:, :GM_W])
        v = (_rms(z[:, GM_W:]) * vg_ref[...]).astype(BF16)
        for ch in range(2):
            crows = slice(ch * GM_CHUNK, (ch + 1) * GM_CHUNK)
            for gi in range(GM_GROUPS):
                cols = slice(gi * gw, (gi + 1) * gw)
                sv_sc[rows.start + ch * GM_CHUNK:rows.start + (ch + 1) * GM_CHUNK, cols] = (
                    _dot(ws_ref[gi], v[crows, cols]) + bs_ref[:, gi:gi + 1])
    for rows, u in zip(groups, us):
        y = _dot((u * sv_sc[rows, :]).astype(BF16), wout_ref[...])
        o_ref[rows, :] = x_ref[rows, :] + _gate(mods_ref, 1) * y


def _gmlp(x, mods, g, w_in, b_in, v_g, w_s, b_s_t, w_out, layer, tm=TOK_TILE):
    return pl.pallas_call(
        _gm_kernel,
        grid=(N_TOK // tm,),
        in_specs=[_x_spec(tm, 0), _mods_spec(layer, tm, 0), _resident((1, D)),
                  _resident((D, 2 * GM_W)), _resident((1, 2 * GM_W)), _resident((1, GM_W)),
                  _resident((GM_GROUPS, GM_CHUNK, GM_CHUNK)), _resident((GM_CHUNK, GM_GROUPS)),
                  _resident((GM_W, D))],
        out_specs=_x_spec(tm, 0),
        out_shape=jax.ShapeDtypeStruct(x.shape, F32),
        scratch_shapes=[pltpu.VMEM((tm, GM_W), F32)],
        input_output_aliases={0: 0}, compiler_params=_params(1), name="gmlp",
    )(x, mods, g.reshape(1, D), w_in, b_in.reshape(1, 2 * GM_W), v_g.reshape(1, GM_W), w_s, b_s_t, w_out)


def _na_pre_kernel(x_ref, mods_ref, g_ref, w_ref, q_ref, k_ref, v_ref, *f32_refs):
    for rows in _row_groups(x_ref.shape[0], SEQ):
        hb = _mod_norm(x_ref[rows, :], g_ref, mods_ref, 1).astype(BF16)
        qkv = _dot(hb, w_ref[...])
        q_ref[rows, :] = (qkv[:, :D] * (NA_HD ** -0.5)).astype(BF16)
        k, v = qkv[:, D:2 * D], qkv[:, 2 * D:]
        k_ref[rows, :] = k.astype(BF16)
        v_ref[rows, :] = v.astype(BF16)
        if f32_refs:
            f32_refs[0][rows.start // SEQ] = k.T
            f32_refs[1][rows.start // SEQ] = v.T


def _na_pre(x, mods, g, w_qkv, layer, off, n, emit_f32, tm=TOK_TILE):
    tok = pl.BlockSpec((tm, D), lambda i: (i, 0))
    out_specs = [tok, tok, tok]
    out_shape = [jax.ShapeDtypeStruct((n * tm, D), BF16)] * 3
    if emit_f32:
        assert tm % SEQ == 0
        out_specs += [pl.BlockSpec((tm // SEQ, D, SEQ), lambda i: (i, 0, 0))] * 2
        out_shape += [jax.ShapeDtypeStruct((n * tm // SEQ, D, SEQ), F32)] * 2
    return pl.pallas_call(
        _na_pre_kernel,
        grid=(n,),
        in_specs=[_x_spec(tm, off), _mods_spec(layer, tm, off), _resident((1, D)), _resident((D, 3 * D))],
        out_specs=out_specs, out_shape=out_shape,
        compiler_params=_params(1), name="na_qkv",
    )(x, mods, g.reshape(1, D), w_qkv)


def _half_mask(shape, e):
    lane = lax.broadcasted_iota(jnp.int32, shape, len(shape) - 1)
    return (lane < NA_HD) if e == 0 else (lane >= NA_HD)


def _ctx_attn_kernel(x_ref, q_ref, k_ref, v_ref, mods_ref, w_ref, o_ref, att_sc):
    for j in range(NA_PAIRS):
        sl = slice(j * LANE, (j + 1) * LANE)
        s = _dot_nt(_pair_rows(q_ref[:, sl]), k_ref[:, sl])
        p = jnp.exp(s - jnp.max(s, axis=1, keepdims=True))
        o = _dot(p.astype(BF16), v_ref[:, sl]) / jnp.sum(p, axis=1, keepdims=True)
        att_sc[:, sl] = jnp.where(_half_mask((SEQ, LANE), 0), o[:SEQ], o[SEQ:]).astype(BF16)
    o_ref[...] = x_ref[...] + _gate(mods_ref, 1) * _dot(att_sc[...], w_ref[...])


def _ctx_attn(x, q, k, v, mods, w_out, layer):
    tok = pl.BlockSpec((SEQ, D), lambda b: (b, 0))
    return pl.pallas_call(
        _ctx_attn_kernel,
        grid=(BATCH,),
        in_specs=[tok, tok, tok, tok, _mods_spec(layer, SEQ, 0), _resident((D, D))],
        out_specs=tok,
        out_shape=jax.ShapeDtypeStruct(x.shape, F32),
        scratch_shapes=[pltpu.VMEM((SEQ, D), BF16)],
        input_output_aliases={0: 0}, compiler_params=_params(1), name="ctx_attn",
    )(x, q, k, v, mods, w_out)


def _na_bias_kernel(rpb_ref, o_ref, pair_sc):
    h = pl.program_id(0)
    shape = (GRID_W, LANE)
    q = lax.broadcasted_iota(jnp.int32, shape, 0)
    lane = lax.broadcasted_iota(jnp.int32, shape, 1)
    x = lane & (GRID_W - 1)
    first = lane < GRID_W
    dc = x - q + (NA_KW - 1)
    q_start = jnp.clip(q - NA_KW // 2, 0, GRID_W - NA_KW)
    in_window = (x >= q_start) & (x < q_start + NA_KW)
    n_dc = 2 * NA_KW - 1
    n_pairs = 2 * NA_KH - 2
    for dr0 in range(0, n_pairs, 5):
        drs = range(dr0, min(dr0 + 5, n_pairs))
        accs = [jnp.zeros(shape, F32) for _ in drs]
        for j in range(n_dc):
            hit = dc == j
            for a, dr in enumerate(drs):
                val = jnp.where(first, rpb_ref[h, dr * n_dc + j], rpb_ref[h, (dr + 1) * n_dc + j])
                accs[a] = jnp.where(hit, val, accs[a])
        for a, dr in enumerate(drs):
            pair_sc[dr] = jnp.where(in_window, accs[a], NEG_INF)
    for o in range(NA_KH):
        for t in range(NA_KH // 2):
            o_ref[o, :, t * LANE:(t + 1) * LANE] = pair_sc[NA_KH - 1 - o + 2 * t]


def _na_bias_tables(rpb):
    n_rel = (2 * NA_KH - 1) * (2 * NA_KW - 1)
    t = pl.pallas_call(
        _na_bias_kernel,
        grid=(NA_HEADS,),
        in_specs=[pl.BlockSpec(memory_space=pltpu.SMEM)],
        out_specs=pl.BlockSpec((NA_KH, None, GRID_W, NA_WIN), lambda h: (0, h, 0, 0)),
        out_shape=jax.ShapeDtypeStruct((NA_KH, NA_HEADS, GRID_W, NA_WIN), F32),
        scratch_shapes=[pltpu.VMEM((2 * NA_KH - 2, GRID_W, LANE), F32)],
        compiler_params=_params(1), name="na_bias",
    )(rpb.reshape(NA_HEADS, n_rel))
    return t.reshape(NA_KH, NA_HEADS * GRID_W, NA_WIN)


def _pair_rows(qp):
    zero = jnp.zeros_like(qp)
    return jnp.concatenate([jnp.where(_half_mask(qp.shape, 0), qp, zero),
                            jnp.where(_half_mask(qp.shape, 1), qp, zero)], axis=0)


def _na_lat_kernel(q_ref, k_ref, v_ref, kc_ref, vc_ref, bias_ref, o_ref, p_sc):
    r = pl.program_id(1)
    start = pl.multiple_of(jnp.clip(r - NA_KH // 2, 0, NA_ROWS - NA_KH) * GRID_W, GRID_W)
    inv = []
    for j in range(NA_PAIRS):
        sl = slice(j * LANE, (j + 1) * LANE)
        qb = _pair_rows(q_ref[:, sl])
        s_w = _dot_nt(qb, k_ref[pl.ds(start, NA_WIN), sl]) + bias_ref[sl, :]
        s_c = _dot_nt(qb, kc_ref[:, sl])
        m = jnp.maximum(jnp.max(s_w, axis=1, keepdims=True), jnp.max(s_c, axis=1, keepdims=True))
        p_w = jnp.exp(s_w - m)
        p_c = jnp.exp(s_c - m)
        inv.append(1.0 / (jnp.sum(p_w, axis=1, keepdims=True) + jnp.sum(p_c, axis=1, keepdims=True)))
        p_sc[sl, :NA_WIN] = p_w.astype(BF16)
        p_sc[sl, NA_WIN:] = p_c.astype(BF16)
    for j in range(NA_PAIRS):
        sl = slice(j * LANE, (j + 1) * LANE)
        o = (_dot(p_sc[sl, :NA_WIN], v_ref[pl.ds(start, NA_WIN), sl])
             + _dot(p_sc[sl, NA_WIN:], vc_ref[:, sl])) * inv[j]
        o_ref[:, sl] = jnp.where(_half_mask((GRID_W, LANE), 0), o[:GRID_W], o[GRID_W:]).astype(BF16)


def _na_latent(q, k, v, kc, vc, bias):
    row_class = lambda r: r - jnp.clip(r - NA_KH // 2, 0, NA_ROWS - NA_KH)
    seq_kv = pl.BlockSpec((None, DEC_SEQ, D), lambda b, r: (b, 0, 0))
    ctx_kv = pl.BlockSpec((None, PAST_LEN, D), lambda b, r: (b, 0, 0))
    n_rows = NA_HEADS * GRID_W
    return pl.pallas_call(
        _na_lat_kernel,
        grid=(DEC_BATCH, NA_ROWS),
        in_specs=[pl.BlockSpec((GRID_W, D), lambda b, r: (b * NA_ROWS + r, 0)),
                  seq_kv, seq_kv, ctx_kv, ctx_kv,
                  pl.BlockSpec((None, n_rows, NA_WIN), lambda b, r: (row_class(r), 0, 0))],
        out_specs=pl.BlockSpec((GRID_W, D), lambda b, r: (b * NA_ROWS + r, 0)),
        out_shape=jax.ShapeDtypeStruct((NS_TOK, D), BF16),
        scratch_shapes=[pltpu.VMEM((n_rows, NA_WIN + PAST_LEN), BF16)],
        compiler_params=_params(2), name="na_latent",
    )(q, k.reshape(DEC_BATCH, DEC_SEQ, D), v.reshape(DEC_BATCH, DEC_SEQ, D), kc, vc, bias)


def kernel(x_prompt, x_sample, state_mlstm_C, state_mlstm_n, state_mlstm_m, cache_na_k, cache_na_v, c, c_ctx, w_ada, b_ada, norm_g, final_g, ffn_w1, ffn_w3, ffn_w2, ml_w_qkv, ml_w_if, ml_b_if, ml_w_og, ml_head_g, ml_w_out, fn_w_out, fn_b_out, gm_w_in, gm_b_in, gm_v_g, gm_w_s, gm_b_s, gm_w_out, na_w_qkv, na_w_out, na_rpb):
    n_p, n_s = NP_TOK // TOK_TILE, NS_TOK // TOK_TILE
    x = (x_prompt.reshape(NP_TOK, D), x_sample.reshape(NS_TOK, D))

    cond = jnp.zeros((N_COND, D), F32).at[0].set(c_ctx).at[1:1 + DEC_BATCH].set(c)
    mods = _adaln(cond, w_ada, b_ada).reshape(DEPTH * N_COND, 9, D)

    ffn_f32 = (ffn_w1, ffn_w3, ffn_w2)
    wb = tuple(w[0, 0].astype(BF16) for w in ffn_f32)
    outs = {}
    for l in range(DEPTH):
        kind, j = l % 4, l // 4
        x, wb = _ffn(x, mods, norm_g[l, 0], wb, l, 0, 0, N_TOK // FFN_TILE, next_f32=(*ffn_f32, l, 1))
        g = norm_g[l, 1]
        if kind == 0:
            w_dir = jnp.transpose(ml_w_if[j], (1, 0, 2))
            lane_pad = ((0, 0), (0, LANE - N_STATE))
            wif = jnp.concatenate(
                [jnp.pad(w_dir[:, :, :ML_HEADS].reshape(D, N_STATE), lane_pad),
                 jnp.pad(w_dir[:, :, ML_HEADS:].reshape(D, N_STATE), lane_pad)], axis=1)
            bif = jnp.concatenate(
                [jnp.pad(ml_b_if[j][:, :ML_HEADS].reshape(1, N_STATE), lane_pad),
                 jnp.pad(ml_b_if[j][:, ML_HEADS:].reshape(1, N_STATE), lane_pad)], axis=1)
            wqkv, wog, wout = ml_w_qkv[j].astype(BF16), ml_w_og[j].astype(BF16), ml_w_out[j].astype(BF16)
            for off, n, nb, seq in ((0, n_p, BATCH, SEQ), (n_p, n_s, DEC_BATCH, DEC_SEQ)):
                q, k, kt, v, og, gates = _ml_pre(x, mods, g, wqkv, wog, wif.astype(BF16), bif, l, off, n)
                if off == 0:
                    hf, hb, c_new, n_new, m_new = _ml_scan(q, k, kt, v, gates, nb, seq, emit_state=True)
                    outs["C"] = c_new.reshape(BATCH, 1, 2, ML_HEADS, ML_DK, ML_DK)
                    outs["n"] = n_new.reshape(BATCH, 1, 2, ML_HEADS, ML_DK)
                    outs["m"] = m_new[:, 0, :N_STATE].reshape(BATCH, 1, 2, ML_HEADS)
                else:
                    init = (state_mlstm_C[:, j].reshape(DEC_BATCH, N_STATE, ML_DK, ML_DK),
                            state_mlstm_n[:, j].reshape(DEC_BATCH, N_STATE, ML_DK),
                            jnp.pad(state_mlstm_m[:, j].reshape(DEC_BATCH, 1, N_STATE),
                                    ((0, 0), (0, 0), (0, LANE - N_STATE))))
                    hf, hb = _ml_scan(q, k, kt, v, gates, nb, seq, init=init)
                x = _ml_post(x, hf, hb, og, mods, ml_head_g[j], wout, l, off, n)
        elif kind == 1:
            cc, sc = _dft_tables(FG)
            wout = fn_w_out[j].astype(BF16)
            cs, ss = _dft_tables(SEQ)
            x = _fn_fused(x, mods, g, cc, sc, cs, ss, wout, fn_b_out[j], l, 0, n_p, SEQ)
            a, b = _fn_chan(x, mods, g, cc, sc, l, n_p, n_s)
            cs, ss = _dft_tables(DEC_SEQ)
            x = _fn_seq(x, a, b, cs, ss, mods, wout, fn_b_out[j], l, NP_TOK, DEC_BATCH, DEC_SEQ, TOK_TILE)
        elif kind == 2:
            x = _gmlp(x, mods, g, gm_w_in[j].astype(BF16), gm_b_in[j], gm_v_g[j],
                      gm_w_s[j].astype(BF16), gm_b_s[j].T, gm_w_out[j].astype(BF16), l)
        else:
            wqkv, wout = na_w_qkv[j].astype(BF16), na_w_out[j].astype(BF16)
            q, k, v, k_heads, v_heads = _na_pre(x, mods, g, wqkv, l, 0, n_p, True)
            per_head = lambda t: jnp.transpose(t.reshape(BATCH, 1, NA_HEADS, NA_HD, SEQ), (0, 1, 4, 2, 3))
            outs["k"], outs["v"] = per_head(k_heads), per_head(v_heads)
            x = _ctx_attn(x, q, k, v, mods, wout, l)
            q, k, v = _na_pre(x, mods, g, wqkv, l, n_p, n_s, False)
            att = _na_latent(q, k, v,
                             cache_na_k[:, j].reshape(DEC_BATCH, PAST_LEN, D).astype(BF16),
                             cache_na_v[:, j].reshape(DEC_BATCH, PAST_LEN, D).astype(BF16),
                             _na_bias_tables(na_rpb[j]))
            x = _proj_residual(x, att, mods, wout, l, n_p, n_s)
        if l < DEPTH - 1:
            x, wb = _ffn(x, mods, norm_g[l, 2], wb, l, 1, 0, N_TOK // FFN_TILE, next_f32=(*ffn_f32, l + 1, 0))
        else:
            y_p = _ffn(x, mods, norm_g[l, 2], wb, l, 1, 0, NP_TOK // FFN_TILE, final_g=final_g)
            y_s = _ffn(x, mods, norm_g[l, 2], wb, l, 1, NP_TOK // FFN_TILE, NS_TOK // FFN_TILE, final_g=final_g)
    return (y_p.reshape(BATCH, SEQ, D), y_s.reshape(DEC_BATCH, DEC_SEQ, D),
            outs["C"], outs["n"], outs["m"], outs["k"], outs["v"])
```

```python
import functools

import numpy as np
import jax
import jax.numpy as jnp
from jax import lax
from jax.experimental import pallas as pl
from jax.experimental.pallas import tpu as pltpu

D = 1024
BATCH = 32
SEQ = 256
DEPTH = 4
DEC_BATCH = 2
DEC_SEQ = 2048
PAST_LEN = 512
GRID_W = 64
D_FF = 2816
EPS = 1e-6

NP_TOK = BATCH * SEQ
NS_TOK = DEC_BATCH * DEC_SEQ
N_TOK = NP_TOK + NS_TOK

ML_HEADS = 4
ML_DK = D // ML_HEADS
ML_CHUNK = 128
N_STATE = 2 * ML_HEADS
ML_SEQ_GROUP = 2

FOURIER_GROUPS = 4
FG = D // FOURIER_GROUPS

GM_W = D
GM_GROUPS = 4
GM_CHUNK = 128

NA_HEADS = 16
NA_HD = D // NA_HEADS
NA_KH = 8
NA_KW = 16
NA_ROWS = DEC_SEQ // GRID_W
NA_WIN = NA_KH * GRID_W
NA_PAIRS = NA_HEADS // 2
N_COND = 8

LANE = 128
BF16_ROWS = 16
TOK_TILE = 512
ROW_GROUP = 256
ADALN_COLS = 2304
FFN_ROWS = 256
FFN_TILE = 1024
FFN_CAST_CHUNKS = 8
VMEM_LIMIT = 60 * 1024 * 1024

F32 = jnp.float32
BF16 = jnp.bfloat16
NEG_INF = float("-inf")


def _params(n_axes, last_arbitrary=False):
    sem = ["parallel"] * n_axes
    if last_arbitrary:
        sem[-1] = "arbitrary"
    return pltpu.CompilerParams(dimension_semantics=tuple(sem), vmem_limit_bytes=VMEM_LIMIT)


def _resident(shape):
    zeros = (0,) * len(shape)
    return pl.BlockSpec(shape, lambda *_: zeros, pipeline_mode=pl.Buffered(1))


def _tile_cond(i, tm):
    row0 = i * tm
    return jnp.where(row0 < NP_TOK, 0, 1 + (row0 - NP_TOK) // DEC_SEQ)


def _x_spec(tm, off):
    return pl.BlockSpec((tm, D), lambda i: (i + off, 0))


def _mods_spec(layer, tm, off):
    return pl.BlockSpec((None, 9, D), lambda i: (layer * N_COND + _tile_cond(i + off, tm), 0, 0))


def _dot(a, b):
    return jnp.dot(a, b, preferred_element_type=F32)


def _dot_nt(a, b):
    return lax.dot_general(a, b, (((1,), (1,)), ((), ())), preferred_element_type=F32)


def _rms(x):
    return x * lax.rsqrt(jnp.mean(x * x, axis=-1, keepdims=True) + EPS)


def _mod_norm(x, g_ref, mods_ref, idx):
    h = _rms(x) * g_ref[...]
    return h * (1.0 + mods_ref[3 * idx + 1:3 * idx + 2, :]) + mods_ref[3 * idx:3 * idx + 1, :]


def _gate(mods_ref, idx):
    return mods_ref[3 * idx + 2:3 * idx + 3, :]


def _row_groups(n_rows, group=ROW_GROUP):
    return [slice(r0, r0 + group) for r0 in range(0, n_rows, group)]


def _adaln_kernel(c_ref, w_ref, b_ref, o_ref):
    c = c_ref[...]
    s = (c * jax.nn.sigmoid(c)).astype(BF16)
    o_ref[...] = _dot(s, w_ref[...].astype(BF16)) + b_ref[...]


def _adaln(cond, w_ada, b_ada):
    tn = ADALN_COLS
    nj = 9 * D // tn
    return pl.pallas_call(
        _adaln_kernel,
        grid=(DEPTH, nj),
        in_specs=[
            pl.BlockSpec((N_COND, D), lambda l, j: (0, 0)),
            pl.BlockSpec((None, D, tn), lambda l, j: (l, 0, j)),
            pl.BlockSpec((None, 1, tn), lambda l, j: (l, 0, j)),
        ],
        out_specs=pl.BlockSpec((None, N_COND, tn), lambda l, j: (l, 0, j)),
        out_shape=jax.ShapeDtypeStruct((DEPTH, N_COND, 9 * D), F32),
        compiler_params=_params(2),
        name="adaln",
    )(cond, w_ada, b_ada.reshape(DEPTH, 1, 9 * D))


def _ffn_kernel(*refs, idx, final, convert_next, split_at):
    if split_at is None:
        x_ref, refs = refs[0], refs[1:]
        load_x = lambda rows: x_ref[rows, :]
    else:
        (xa_ref, xb_ref), refs = refs[:2], refs[2:]
        first = pl.program_id(0) < split_at
        load_x = lambda rows: jnp.where(first, xa_ref[rows, :], xb_ref[rows, :])
    mods_ref, g_ref, w1_ref, w3_ref, w2_ref = refs[:5]
    rest = refs[5:]
    if convert_next:
        (nw1_ref, nw3_ref, nw2_ref), rest = rest[:3], rest[3:]
        cast_out, rest = rest[-3:], rest[:-3]
    o_ref = rest[-1]
    for rows in _row_groups(o_ref.shape[0], FFN_ROWS):
        x = load_x(rows)
        hb = _mod_norm(x, g_ref, mods_ref, idx).astype(BF16)
        a = _dot(hb, w1_ref[...])
        b = _dot(hb, w3_ref[...])
        act = (a * jax.nn.sigmoid(a) * b).astype(BF16)
        y = x + (0.5 * _gate(mods_ref, idx)) * _dot(act, w2_ref[...])
        if final:
            y = _rms(y) * rest[0][...]
        o_ref[rows, :] = y
    if convert_next:
        for src, dst in zip((nw1_ref, nw3_ref, nw2_ref), cast_out):
            dst[...] = src[...].astype(BF16)


def _ffn(x, mods, g, wb, layer, f, off, n, final_g=None, next_f32=None, tm=FFN_TILE):
    final = final_g is not None
    convert_next = next_f32 is not None
    idx = 2 * f
    split_at = None
    if isinstance(x, tuple):
        xa, xb = x
        split_at = xa.shape[0] // tm
        assert not final and off == 0 and xa.shape[0] % tm == 0 and n * tm == xa.shape[0] + xb.shape[0]
        x_specs = [pl.BlockSpec((tm, D), lambda i: (jnp.minimum(i, split_at - 1), 0)),
                   pl.BlockSpec((tm, D), lambda i: (jnp.maximum(i - split_at, 0), 0))]
        x_args = [xa, xb]
    else:
        x_specs, x_args = [_x_spec(tm, off)], [x]
    in_specs = x_specs + [_mods_spec(layer, tm, off), _resident((1, D)),
                          _resident((D, D_FF)), _resident((D, D_FF)), _resident((D_FF, D))]
    args = x_args + [mods, g.reshape(1, D), *wb]
    if convert_next:
        nw1, nw3, nw2, nl, nf = next_f32
        chunk = lambda i: jnp.minimum(i, FFN_CAST_CHUNKS - 1)
        for w in (nw1, nw3, nw2):
            rows, cols = w.shape[2] // FFN_CAST_CHUNKS, w.shape[3]
            in_specs.append(pl.BlockSpec((None, None, rows, cols), lambda i: (nl, nf, chunk(i), 0)))
            args.append(w)
    if final:
        in_specs.append(_resident((1, D)))
        args.append(final_g.reshape(1, D))
        out_specs = [pl.BlockSpec((tm, D), lambda i: (i, 0))]
        out_shape = [jax.ShapeDtypeStruct((n * tm, D), F32)]
        aliases = {}
    elif split_at is not None:
        out_specs = [_x_spec(tm, 0)]
        out_shape = [jax.ShapeDtypeStruct((n * tm, D), F32)]
        aliases = {}
    else:
        out_specs = [_x_spec(tm, off)]
        out_shape = [jax.ShapeDtypeStruct(x.shape, F32)]
        aliases = {0: 0}
    if convert_next:
        assert n >= FFN_CAST_CHUNKS
        for w in next_f32[:3]:
            rows, cols = w.shape[2] // FFN_CAST_CHUNKS, w.shape[3]
            out_specs.append(pl.BlockSpec((rows, cols), lambda i: (chunk(i), 0)))
            out_shape.append(jax.ShapeDtypeStruct(w.shape[2:], BF16))
    outs = pl.pallas_call(
        functools.partial(_ffn_kernel, idx=idx, final=final, convert_next=convert_next, split_at=split_at),
        grid=(n,), in_specs=in_specs, out_specs=out_specs, out_shape=out_shape,
        input_output_aliases=aliases, compiler_params=_params(1, last_arbitrary=True), name="ffn",
    )(*args)
    return (outs[0], tuple(outs[1:])) if convert_next else outs[0]


def _proj_kernel(x_ref, a_ref, mods_ref, w_ref, o_ref):
    y = _dot(a_ref[...].astype(BF16), w_ref[...])
    o_ref[...] = x_ref[...] + _gate(mods_ref, 1) * y


def _proj_residual(x, a, mods, w, layer, off, n, tm=TOK_TILE):
    return pl.pallas_call(
        _proj_kernel,
        grid=(n,),
        in_specs=[_x_spec(tm, off), pl.BlockSpec((tm, D), lambda i: (i, 0)),
                  _mods_spec(layer, tm, off), _resident((D, D))],
        out_specs=_x_spec(tm, off),
        out_shape=jax.ShapeDtypeStruct(x.shape, F32),
        input_output_aliases={0: 0}, compiler_params=_params(1), name="proj_residual",
    )(x, a, mods, w)


def _ml_pre_kernel(x_ref, mods_ref, g_ref, wqkv_ref, wog_ref, wif_ref, bif_ref,
                   q_ref, k_ref, kt_ref, v_ref, og_ref, gates_ref):
    for rows in _row_groups(x_ref.shape[0]):
        hb = _mod_norm(x_ref[rows, :], g_ref, mods_ref, 1).astype(BF16)
        qkv = _dot(hb, wqkv_ref[...])
        q_ref[rows, :] = qkv[:, :D].astype(BF16)
        k = qkv[:, D:2 * D] * (ML_DK ** -0.5)
        k_ref[rows, :] = k.astype(BF16)
        for ch in range(rows.start // ML_CHUNK, rows.stop // ML_CHUNK):
            kt_ref[ch] = k[ch * ML_CHUNK - rows.start:(ch + 1) * ML_CHUNK - rows.start, :].T.astype(BF16)
        v_ref[rows, :] = qkv[:, 2 * D:].astype(BF16)
        og_ref[rows, :] = jax.nn.sigmoid(_dot(hb, wog_ref[...])).astype(BF16)
        gates_ref[rows, :] = _dot(hb, wif_ref[...]) + bif_ref[...]


def _ml_pre(x, mods, g, wqkv, wog, wif, bif, layer, off, n, tm=TOK_TILE):
    tok = pl.BlockSpec((tm, D), lambda i: (i, 0))
    return pl.pallas_call(
        _ml_pre_kernel,
        grid=(n,),
        in_specs=[_x_spec(tm, off), _mods_spec(layer, tm, off), _resident((1, D)),
                  _resident((D, 3 * D)), _resident((D, D)), _resident((D, 2 * LANE)), _resident((1, 2 * LANE))],
        out_specs=[tok, tok, pl.BlockSpec((tm // ML_CHUNK, D, ML_CHUNK), lambda i: (i, 0, 0)), tok, tok,
                   pl.BlockSpec((tm, 2 * LANE), lambda i: (i, 0))],
        out_shape=[jax.ShapeDtypeStruct((n * tm, D), BF16)] * 2
        + [jax.ShapeDtypeStruct((n * tm // ML_CHUNK, D, ML_CHUNK), BF16), jax.ShapeDtypeStruct((n * tm, D), BF16),
           jax.ShapeDtypeStruct((n * tm, D), BF16), jax.ShapeDtypeStruct((n * tm, 2 * LANE), F32)],
        compiler_params=_params(1), name="mlstm_pre",
    )(x, mods, g.reshape(1, D), wqkv, wog, wif, bif)


def _log_sigmoid(x):
    return jnp.minimum(x, 0.0) - jnp.log1p(jnp.exp(-jnp.abs(x)))


def _scan_rows(x, fwd_lanes, op, identity):
    n = x.shape[0]
    row = lax.broadcasted_iota(jnp.int32, x.shape, 0)
    sh = 1
    while sh < n:
        prev = jnp.where(row >= sh, pltpu.roll(x, sh, 0), identity)
        nxt = jnp.where(row < n - sh, pltpu.roll(x, n - sh, 0), identity)
        x = op(x, jnp.where(fwd_lanes, prev, nxt))
        sh *= 2
    return x


def _ml_gates(fwd_refs, bwd_refs, m_sc, bi):
    L = ML_CHUNK
    lane = lax.broadcasted_iota(jnp.int32, (L, LANE), 1)
    fwd_lanes = lane < ML_HEADS
    gf_ref, gb_ref = fwd_refs[4], bwd_refs[4]
    i_pre = jnp.where(fwd_lanes, gf_ref[bi, :, :LANE], gb_ref[bi, :, :LANE])
    f_pre = jnp.where(fwd_lanes, gf_ref[bi, :, LANE:], gb_ref[bi, :, LANE:])
    log_f = jnp.where(lane < N_STATE, _log_sigmoid(f_pre), 0.0)
    bsum = _scan_rows(log_f, fwd_lanes, jnp.add, 0.0)
    rel = i_pre - bsum
    m_old = m_sc[bi]
    mm = jnp.maximum(m_old, _scan_rows(rel, fwd_lanes, jnp.maximum, NEG_INF))
    mx = jnp.maximum(m_old, jnp.max(rel, axis=0, keepdims=True))
    b_last = jnp.where(fwd_lanes[0:1], bsum[L - 1:L, :], bsum[0:1, :])
    m_sc[bi] = b_last + mx
    return dict(mm=mm, mx=mx,
                w_inter=jnp.exp(m_old - mm),
                floor=jnp.exp(-(bsum + mm)),
                decay=jnp.exp(m_old - mx),
                rel_t=rel.T)


def _ml_chain(refs, h_ref, c_sc, n_sc, gt, bi, d, hd):
    L = ML_CHUNK
    q_ref, k_ref, kt_ref, v_ref, _ = refs
    t_idx = lax.broadcasted_iota(jnp.int32, (L, L), 0)
    s_idx = lax.broadcasted_iota(jnp.int32, (L, L), 1)
    visible = (s_idx >= t_idx) if d == 1 else (s_idx <= t_idx)
    r = d * ML_HEADS + hd
    lo, hi = hd * ML_DK, (hd + 1) * ML_DK
    rel_row = gt["rel_t"][r:r + 1, :]
    w_col = gt["w_inter"][:, r:r + 1]
    n_old = n_sc[bi, r:r + 1, :]
    c_old = c_sc[bi, r]
    qh, kh, vh = q_ref[bi, :, lo:hi], k_ref[bi, :, lo:hi], v_ref[bi, :, lo:hi]
    kth = kt_ref[bi, lo:hi, :]

    a = jnp.exp(jnp.where(visible, rel_row - gt["mm"][:, r:r + 1], NEG_INF)) * _dot(qh, kth)
    num = _dot(a.astype(BF16), vh) + w_col * _dot(qh, c_old.astype(BF16))
    den = jnp.sum(a, axis=1, keepdims=True) + w_col * jnp.sum(qh.astype(F32) * n_old, axis=1, keepdims=True)
    h_ref[bi, :, lo:hi] = (num * (1.0 / jnp.maximum(jnp.abs(den), gt["floor"][:, r:r + 1]))).astype(BF16)

    w_row = jnp.exp(rel_row - gt["mx"][:, r:r + 1])
    dec = gt["decay"][:, r:r + 1]
    c_sc[bi, r] = dec * c_old + _dot((kth.astype(F32) * w_row).astype(BF16), vh)
    w_rows = jnp.broadcast_to(w_row, (BF16_ROWS, L)).astype(BF16)
    n_sc[bi, r:r + 1, :] = dec * n_old + _dot(w_rows, kh)[0:1, :]


def _ml_scan_kernel(*refs, with_init, emit_state):
    fwd_refs, bwd_refs, refs = refs[:5], refs[5:10], refs[10:]
    if with_init:
        (c0_ref, n0_ref, m0_ref), refs = refs[:3], refs[3:]
    (hf_ref, hb_ref), refs = refs[:2], refs[2:]
    if emit_state:
        (co_ref, no_ref, mo_ref), refs = refs[:3], refs[3:]
    c_sc, n_sc, m_sc = refs
    c = pl.program_id(1)

    @pl.when(c == 0)
    def _():
        if with_init:
            c_sc[...] = c0_ref[...]
            n_sc[...] = n0_ref[...]
            m_sc[...] = m0_ref[...]
        else:
            c_sc[...] = jnp.zeros_like(c_sc)
            n_sc[...] = jnp.zeros_like(n_sc)
            m_sc[...] = jnp.zeros_like(m_sc)

    seqs = range(c_sc.shape[0])
    gates = [_ml_gates(fwd_refs, bwd_refs, m_sc, bi) for bi in seqs]
    for d, (refs_d, h_ref) in enumerate(((fwd_refs, hf_ref), (bwd_refs, hb_ref))):
        for hd in range(ML_HEADS):
            for bi in seqs:
                _ml_chain(refs_d, h_ref, c_sc, n_sc, gates[bi], bi, d, hd)

    if emit_state:
        @pl.when(c == pl.num_programs(1) - 1)
        def _():
            co_ref[...] = c_sc[...]
            no_ref[...] = n_sc[...]
            mo_ref[...] = m_sc[...]


def _ml_scan(q, k, kt, v, gates, nb, seq, init=None, emit_state=False, group=ML_SEQ_GROUP):
    nc = seq // ML_CHUNK
    per_chunk = lambda a: a.reshape(nb, nc, *a.shape[-2:]) if a.ndim == 3 else a.reshape(nb, nc, ML_CHUNK, -1)
    fwd = lambda b, c: (b, c, 0, 0)
    bwd = lambda b, c: (b, nc - 1 - c, 0, 0)
    blk = lambda a, im: pl.BlockSpec((group, None) + a.shape[2:], im)
    st_c = pl.BlockSpec((group, N_STATE, ML_DK, ML_DK), lambda b, c: (b, 0, 0, 0))
    st_n = pl.BlockSpec((group, N_STATE, ML_DK), lambda b, c: (b, 0, 0))
    st_m = pl.BlockSpec((group, 1, LANE), lambda b, c: (b, 0, 0))
    arrays = [per_chunk(a) for a in (q, k, kt, v, gates)]
    in_specs = [blk(a, fwd) for a in arrays] + [blk(a, bwd) for a in arrays]
    args = arrays * 2
    if init is not None:
        in_specs += [st_c, st_n, st_m]
        args += list(init)
    h_shape = jax.ShapeDtypeStruct((nb, nc, ML_CHUNK, D), BF16)
    out_specs = [blk(h_shape, fwd), blk(h_shape, bwd)]
    out_shape = [h_shape] * 2
    if emit_state:
        out_specs += [st_c, st_n, st_m]
        out_shape += [jax.ShapeDtypeStruct((nb, N_STATE, ML_DK, ML_DK), F32),
                      jax.ShapeDtypeStruct((nb, N_STATE, ML_DK), F32),
                      jax.ShapeDtypeStruct((nb, 1, LANE), F32)]
    outs = pl.pallas_call(
        functools.partial(_ml_scan_kernel, with_init=init is not None, emit_state=emit_state),
        grid=(nb // group, nc), in_specs=in_specs, out_specs=out_specs, out_shape=out_shape,
        scratch_shapes=[pltpu.VMEM((group, N_STATE, ML_DK, ML_DK), F32),
                        pltpu.VMEM((group, N_STATE, ML_DK), F32), pltpu.VMEM((group, 1, LANE), F32)],
        compiler_params=_params(2, last_arbitrary=True), name="mlstm_scan",
    )(*args)
    return [outs[0].reshape(nb * seq, D), outs[1].reshape(nb * seq, D)] + list(outs[2:])


def _ml_post_kernel(x_ref, hf_ref, hb_ref, og_ref, mods_ref, hg_ref, w_ref, o_ref):
    for rows in _row_groups(x_ref.shape[0]):
        hsum = hf_ref[rows, :].astype(F32) + hb_ref[rows, :].astype(F32)
        hn = jnp.concatenate(
            [_rms(hsum[:, hd * ML_DK:(hd + 1) * ML_DK]) for hd in range(ML_HEADS)], axis=1)
        y = _dot((og_ref[rows, :].astype(F32) * (hn * hg_ref[...])).astype(BF16), w_ref[...])
        o_ref[rows, :] = x_ref[rows, :] + _gate(mods_ref, 1) * y


def _ml_post(x, hf, hb, og, mods, head_g, w_out, layer, off, n, tm=TOK_TILE):
    tok = pl.BlockSpec((tm, D), lambda i: (i, 0))
    return pl.pallas_call(
        _ml_post_kernel,
        grid=(n,),
        in_specs=[_x_spec(tm, off), tok, tok, tok, _mods_spec(layer, tm, off),
                  _resident((1, D)), _resident((D, D))],
        out_specs=_x_spec(tm, off),
        out_shape=jax.ShapeDtypeStruct(x.shape, F32),
        input_output_aliases={0: 0}, compiler_params=_params(1), name="mlstm_post",
    )(x, hf, hb, og, mods, head_g.reshape(1, D), w_out)


def _fn_chan_kernel(x_ref, mods_ref, g_ref, cc_ref, sc_ref, a_ref, b_ref):
    hb = _mod_norm(x_ref[...], g_ref, mods_ref, 1).astype(BF16)
    for gi in range(FOURIER_GROUPS):
        sl = slice(gi * FG, (gi + 1) * FG)
        a_ref[:, sl] = _dot(hb[:, sl], cc_ref[...]).astype(BF16)
        b_ref[:, sl] = _dot(hb[:, sl], sc_ref[...]).astype(BF16)


def _fn_chan(x, mods, g, cc, sc, layer, off, n, tm=TOK_TILE):
    tok = pl.BlockSpec((tm, D), lambda i: (i, 0))
    return pl.pallas_call(
        _fn_chan_kernel,
        grid=(n,),
        in_specs=[_x_spec(tm, off), _mods_spec(layer, tm, off), _resident((1, D)),
                  _resident((FG, FG)), _resident((FG, FG))],
        out_specs=[tok, tok],
        out_shape=[jax.ShapeDtypeStruct((n * tm, D), BF16)] * 2,
        compiler_params=_params(1), name="fourier_chan",
    )(x, mods, g.reshape(1, D), cc, sc)


def _fn_seq_kernel(x_ref, a_ref, b_ref, cs_ref, ss_ref, mods_ref, w_ref, bias_ref, o_ref, *, scale):
    f = (_dot(cs_ref[...], a_ref[...]) - _dot(ss_ref[...], b_ref[...])) * scale
    y = _dot(f.astype(BF16), w_ref[...]) + bias_ref[...]
    o_ref[...] = x_ref[...] + _gate(mods_ref, 1) * y


def _fn_seq(x, a, b, cs, ss, mods, w_out, b_out, layer, off_tok, nb, seq, tr):
    nt = seq // tr
    off = off_tok // tr
    xs = pl.BlockSpec((tr, D), lambda bi, t: (off + bi * nt + t, 0))
    ab = pl.BlockSpec((seq, D), lambda bi, t: (bi, 0))
    tbl = pl.BlockSpec((tr, seq), lambda bi, t: (t, 0))
    mods_spec = pl.BlockSpec(
        (None, 9, D), lambda bi, t: (layer * N_COND + _tile_cond(off + bi * nt + t, tr), 0, 0))
    return pl.pallas_call(
        functools.partial(_fn_seq_kernel, scale=float((seq * FG) ** -0.5)),
        grid=(nb, nt),
        in_specs=[xs, ab, ab, tbl, tbl, mods_spec, _resident((D, D)), _resident((1, D))],
        out_specs=xs,
        out_shape=jax.ShapeDtypeStruct(x.shape, F32),
        input_output_aliases={0: 0}, compiler_params=_params(2), name="fourier_seq",
    )(x, a, b, cs, ss, mods, w_out, b_out.reshape(1, D))


def _fn_fused_kernel(x_ref, mods_ref, g_ref, cc_ref, sc_ref, cs_ref, ss_ref, w_ref, bias_ref, o_ref,
                     a_sc, b_sc, *, seq, scale):
    hb = _mod_norm(x_ref[...], g_ref, mods_ref, 1).astype(BF16)
    for gi in range(FOURIER_GROUPS):
        sl = slice(gi * FG, (gi + 1) * FG)
        a_sc[:, sl] = _dot(hb[:, sl], cc_ref[...]).astype(BF16)
        b_sc[:, sl] = _dot(hb[:, sl], sc_ref[...]).astype(BF16)
    for r0 in range(0, x_ref.shape[0], seq):
        rows = slice(r0, r0 + seq)
        f = (_dot(cs_ref[...], a_sc[rows, :]) - _dot(ss_ref[...], b_sc[rows, :])) * scale
        y = _dot(f.astype(BF16), w_ref[...]) + bias_ref[...]
        o_ref[rows, :] = x_ref[rows, :] + _gate(mods_ref, 1) * y


def _fn_fused(x, mods, g, cc, sc, cs, ss, w_out, b_out, layer, off, n, seq, tm=TOK_TILE):
    return pl.pallas_call(
        functools.partial(_fn_fused_kernel, seq=seq, scale=float((seq * FG) ** -0.5)),
        grid=(n,),
        in_specs=[_x_spec(tm, off), _mods_spec(layer, tm, off), _resident((1, D)),
                  _resident((FG, FG)), _resident((FG, FG)), _resident((seq, seq)), _resident((seq, seq)),
                  _resident((D, D)), _resident((1, D))],
        out_specs=_x_spec(tm, off),
        out_shape=jax.ShapeDtypeStruct(x.shape, F32),
        scratch_shapes=[pltpu.VMEM((tm, D), BF16), pltpu.VMEM((tm, D), BF16)],
        input_output_aliases={0: 0}, compiler_params=_params(1), name="fourier_fused",
    )(x, mods, g.reshape(1, D), cc, sc, cs, ss, w_out, b_out.reshape(1, D))


def _dft_tables(n):
    ang = 2.0 * np.pi * ((np.arange(n)[:, None] * np.arange(n)[None, :]) % n) / n
    return (jnp.asarray(np.cos(ang), F32).astype(BF16), jnp.asarray(np.sin(ang), F32).astype(BF16))


def _gm_kernel(x_ref, mods_ref, g_ref, win_ref, bin_ref, vg_ref, ws_ref, bs_ref, wout_ref, o_ref, sv_sc):
    gw = GM_W // GM_GROUPS
    groups = _row_groups(x_ref.shape[0], 2 * GM_CHUNK)
    zs = [_dot(_mod_norm(x_ref[rows, :], g_ref, mods_ref, 1).astype(BF16), win_ref[...]) + bin_ref[...]
          for rows in groups]
    us = []
    for rows, z in zip(groups, zs):
        z = z * (0.5 * (1.0 + jnp.tanh(np.sqrt(2.0 / np.pi) * (z + 0.044715 * (z * z * z)))))
        us.append(z[:, :GM_W])
        v = (_rms(z[:, GM_W:]) * vg_ref[...]).astype(BF16)
        for ch in range(2):
            crows = slice(ch * GM_CHUNK, (ch + 1) * GM_CHUNK)
            for gi in range(GM_GROUPS):
                cols = slice(gi * gw, (gi + 1) * gw)
                sv_sc[rows.start + ch * GM_CHUNK:rows.start + (ch + 1) * GM_CHUNK, cols] = (
                    _dot(ws_ref[gi], v[crows, cols]) + bs_ref[:, gi:gi + 1])
    for rows, u in zip(groups, us):
        y = _dot((u * sv_sc[rows, :]).astype(BF16), wout_ref[...])
        o_ref[rows, :] = x_ref[rows, :] + _gate(mods_ref, 1) * y


def _gmlp(x, mods, g, w_in, b_in, v_g, w_s, b_s_t, w_out, layer, tm=TOK_TILE):
    return pl.pallas_call(
        _gm_kernel,
        grid=(N_TOK // tm,),
        in_specs=[_x_spec(tm, 0), _mods_spec(layer, tm, 0), _resident((1, D)),
                  _resident((D, 2 * GM_W)), _resident((1, 2 * GM_W)), _resident((1, GM_W)),
                  _resident((GM_GROUPS, GM_CHUNK, GM_CHUNK)), _resident((GM_CHUNK, GM_GROUPS)),
                  _resident((GM_W, D))],
        out_specs=_x_spec(tm, 0),
        out_shape=jax.ShapeDtypeStruct(x.shape, F32),
        scratch_shapes=[pltpu.VMEM((tm, GM_W), F32)],
        input_output_aliases={0: 0}, compiler_params=_params(1), name="gmlp",
    )(x, mods, g.reshape(1, D), w_in, b_in.reshape(1, 2 * GM_W), v_g.reshape(1, GM_W), w_s, b_s_t, w_out)


def _na_pre_kernel(x_ref, mods_ref, g_ref, w_ref, q_ref, k_ref, v_ref, *f32_refs):
    for rows in _row_groups(x_ref.shape[0], SEQ):
        hb = _mod_norm(x_ref[rows, :], g_ref, mods_ref, 1).astype(BF16)
        qkv = _dot(hb, w_ref[...])
        q_ref[rows, :] = (qkv[:, :D] * (NA_HD ** -0.5)).astype(BF16)
        k, v = qkv[:, D:2 * D], qkv[:, 2 * D:]
        k_ref[rows, :] = k.astype(BF16)
        v_ref[rows, :] = v.astype(BF16)
        if f32_refs:
            f32_refs[0][rows.start // SEQ] = k.T
            f32_refs[1][rows.start // SEQ] = v.T


def _na_pre(x, mods, g, w_qkv, layer, off, n, emit_f32, tm=TOK_TILE):
    tok = pl.BlockSpec((tm, D), lambda i: (i, 0))
    out_specs = [tok, tok, tok]
    out_shape = [jax.ShapeDtypeStruct((n * tm, D), BF16)] * 3
    if emit_f32:
        assert tm % SEQ == 0
        out_specs += [pl.BlockSpec((tm // SEQ, D, SEQ), lambda i: (i, 0, 0))] * 2
        out_shape += [jax.ShapeDtypeStruct((n * tm // SEQ, D, SEQ), F32)] * 2
    return pl.pallas_call(
        _na_pre_kernel,
        grid=(n,),
        in_specs=[_x_spec(tm, off), _mods_spec(layer, tm, off), _resident((1, D)), _resident((D, 3 * D))],
        out_specs=out_specs, out_shape=out_shape,
        compiler_params=_params(1), name="na_qkv",
    )(x, mods, g.reshape(1, D), w_qkv)


def _half_mask(shape, e):
    lane = lax.broadcasted_iota(jnp.int32, shape, len(shape) - 1)
    return (lane < NA_HD) if e == 0 else (lane >= NA_HD)


def _ctx_attn_kernel(x_ref, q_ref, k_ref, v_ref, mods_ref, w_ref, o_ref, att_sc):
    for j in range(NA_PAIRS):
        sl = slice(j * LANE, (j + 1) * LANE)
        s = _dot_nt(_pair_rows(q_ref[:, sl]), k_ref[:, sl])
        p = jnp.exp(s - jnp.max(s, axis=1, keepdims=True))
        o = _dot(p.astype(BF16), v_ref[:, sl]) / jnp.sum(p, axis=1, keepdims=True)
        att_sc[:, sl] = jnp.where(_half_mask((SEQ, LANE), 0), o[:SEQ], o[SEQ:]).astype(BF16)
    o_ref[...] = x_ref[...] + _gate(mods_ref, 1) * _dot(att_sc[...], w_ref[...])


def _ctx_attn(x, q, k, v, mods, w_out, layer):
    tok = pl.BlockSpec((SEQ, D), lambda b: (b, 0))
    return pl.pallas_call(
        _ctx_attn_kernel,
        grid=(BATCH,),
        in_specs=[tok, tok, tok, tok, _mods_spec(layer, SEQ, 0), _resident((D, D))],
        out_specs=tok,
        out_shape=jax.ShapeDtypeStruct(x.shape, F32),
        scratch_shapes=[pltpu.VMEM((SEQ, D), BF16)],
        input_output_aliases={0: 0}, compiler_params=_params(1), name="ctx_attn",
    )(x, q, k, v, mods, w_out)


def _na_bias_kernel(rpb_ref, o_ref, pair_sc):
    h = pl.program_id(0)
    shape = (GRID_W, LANE)
    q = lax.broadcasted_iota(jnp.int32, shape, 0)
    lane = lax.broadcasted_iota(jnp.int32, shape, 1)
    x = lane & (GRID_W - 1)
    first = lane < GRID_W
    dc = x - q + (NA_KW - 1)
    q_start = jnp.clip(q - NA_KW // 2, 0, GRID_W - NA_KW)
    in_window = (x >= q_start) & (x < q_start + NA_KW)
    n_dc = 2 * NA_KW - 1
    n_pairs = 2 * NA_KH - 2
    for dr0 in range(0, n_pairs, 5):
        drs = range(dr0, min(dr0 + 5, n_pairs))
        accs = [jnp.zeros(shape, F32) for _ in drs]
        for j in range(n_dc):
            hit = dc == j
            for a, dr in enumerate(drs):
                val = jnp.where(first, rpb_ref[h, dr * n_dc + j], rpb_ref[h, (dr + 1) * n_dc + j])
                accs[a] = jnp.where(hit, val, accs[a])
        for a, dr in enumerate(drs):
            pair_sc[dr] = jnp.where(in_window, accs[a], NEG_INF)
    for o in range(NA_KH):
        for t in range(NA_KH // 2):
            o_ref[o, :, t * LANE:(t + 1) * LANE] = pair_sc[NA_KH - 1 - o + 2 * t]


def _na_bias_tables(rpb):
    n_rel = (2 * NA_KH - 1) * (2 * NA_KW - 1)
    t = pl.pallas_call(
        _na_bias_kernel,
        grid=(NA_HEADS,),
        in_specs=[pl.BlockSpec(memory_space=pltpu.SMEM)],
        out_specs=pl.BlockSpec((NA_KH, None, GRID_W, NA_WIN), lambda h: (0, h, 0, 0)),
        out_shape=jax.ShapeDtypeStruct((NA_KH, NA_HEADS, GRID_W, NA_WIN), F32),
        scratch_shapes=[pltpu.VMEM((2 * NA_KH - 2, GRID_W, LANE), F32)],
        compiler_params=_params(1), name="na_bias",
    )(rpb.reshape(NA_HEADS, n_rel))
    return t.reshape(NA_KH, NA_HEADS * GRID_W, NA_WIN)


def _pair_rows(qp):
    zero = jnp.zeros_like(qp)
    return jnp.concatenate([jnp.where(_half_mask(qp.shape, 0), qp, zero),
                            jnp.where(_half_mask(qp.shape, 1), qp, zero)], axis=0)


def _na_lat_kernel(q_ref, k_ref, v_ref, kc_ref, vc_ref, bias_ref, o_ref, p_sc):
    r = pl.program_id(1)
    start = pl.multiple_of(jnp.clip(r - NA_KH // 2, 0, NA_ROWS - NA_KH) * GRID_W, GRID_W)
    inv = []
    for j in range(NA_PAIRS):
        sl = slice(j * LANE, (j + 1) * LANE)
        qb = _pair_rows(q_ref[:, sl])
        s_w = _dot_nt(qb, k_ref[pl.ds(start, NA_WIN), sl]) + bias_ref[sl, :]
        s_c = _dot_nt(qb, kc_ref[:, sl])
        m = jnp.maximum(jnp.max(s_w, axis=1, keepdims=True), jnp.max(s_c, axis=1, keepdims=True))
        p_w = jnp.exp(s_w - m)
        p_c = jnp.exp(s_c - m)
        inv.append(1.0 / (jnp.sum(p_w, axis=1, keepdims=True) + jnp.sum(p_c, axis=1, keepdims=True)))
        p_sc[sl, :NA_WIN] = p_w.astype(BF16)
        p_sc[sl, NA_WIN:] = p_c.astype(BF16)
    for j in range(NA_PAIRS):
        sl = slice(j * LANE, (j + 1) * LANE)
        o = (_dot(p_sc[sl, :NA_WIN], v_ref[pl.ds(start, NA_WIN), sl])
             + _dot(p_sc[sl, NA_WIN:], vc_ref[:, sl])) * inv[j]
        o_ref[:, sl] = jnp.where(_half_mask((GRID_W, LANE), 0), o[:GRID_W], o[GRID_W:]).astype(BF16)


def _na_latent(q, k, v, kc, vc, bias):
    row_class = lambda r: r - jnp.clip(r - NA_KH // 2, 0, NA_ROWS - NA_KH)
    seq_kv = pl.BlockSpec((None, DEC_SEQ, D), lambda b, r: (b, 0, 0))
    ctx_kv = pl.BlockSpec((None, PAST_LEN, D), lambda b, r: (b, 0, 0))
    n_rows = NA_HEADS * GRID_W
    return pl.pallas_call(
        _na_lat_kernel,
        grid=(DEC_BATCH, NA_ROWS),
        in_specs=[pl.BlockSpec((GRID_W, D), lambda b, r: (b * NA_ROWS + r, 0)),
                  seq_kv, seq_kv, ctx_kv, ctx_kv,
                  pl.BlockSpec((None, n_rows, NA_WIN), lambda b, r: (row_class(r), 0, 0))],
        out_specs=pl.BlockSpec((GRID_W, D), lambda b, r: (b * NA_ROWS + r, 0)),
        out_shape=jax.ShapeDtypeStruct((NS_TOK, D), BF16),
        scratch_shapes=[pltpu.VMEM((n_rows, NA_WIN + PAST_LEN), BF16)],
        compiler_params=_params(2), name="na_latent",
    )(q, k.reshape(DEC_BATCH, DEC_SEQ, D), v.reshape(DEC_BATCH, DEC_SEQ, D), kc, vc, bias)


def kernel(x_prompt, x_sample, state_mlstm_C, state_mlstm_n, state_mlstm_m, cache_na_k, cache_na_v, c, c_ctx, w_ada, b_ada, norm_g, final_g, ffn_w1, ffn_w3, ffn_w2, ml_w_qkv, ml_w_if, ml_b_if, ml_w_og, ml_head_g, ml_w_out, fn_w_out, fn_b_out, gm_w_in, gm_b_in, gm_v_g, gm_w_s, gm_b_s, gm_w_out, na_w_qkv, na_w_out, na_rpb):
    n_p, n_s = NP_TOK // TOK_TILE, NS_TOK // TOK_TILE
    x = (x_prompt.reshape(NP_TOK, D), x_sample.reshape(NS_TOK, D))

    cond = jnp.zeros((N_COND, D), F32).at[0].set(c_ctx).at[1:1 + DEC_BATCH].set(c)
    mods = _adaln(cond, w_ada, b_ada).reshape(DEPTH * N_COND, 9, D)

    ffn_f32 = (ffn_w1, ffn_w3, ffn_w2)
    wb = tuple(w[0, 0].astype(BF16) for w in ffn_f32)
    outs = {}
    for l in range(DEPTH):
        kind, j = l % 4, l // 4
        x, wb = _ffn(x, mods, norm_g[l, 0], wb, l, 0, 0, N_TOK // FFN_TILE, next_f32=(*ffn_f32, l, 1))
        g = norm_g[l, 1]
        if kind == 0:
            w_dir = jnp.transpose(ml_w_if[j], (1, 0, 2))
            lane_pad = ((0, 0), (0, LANE - N_STATE))
            wif = jnp.concatenate(
                [jnp.pad(w_dir[:, :, :ML_HEADS].reshape(D, N_STATE), lane_pad),
                 jnp.pad(w_dir[:, :, ML_HEADS:].reshape(D, N_STATE), lane_pad)], axis=1)
            bif = jnp.concatenate(
                [jnp.pad(ml_b_if[j][:, :ML_HEADS].reshape(1, N_STATE), lane_pad),
                 jnp.pad(ml_b_if[j][:, ML_HEADS:].reshape(1, N_STATE), lane_pad)], axis=1)
            wqkv, wog, wout = ml_w_qkv[j].astype(BF16), ml_w_og[j].astype(BF16), ml_w_out[j].astype(BF16)
            for off, n, nb, seq in ((0, n_p, BATCH, SEQ), (n_p, n_s, DEC_BATCH, DEC_SEQ)):
                q, k, kt, v, og, gates = _ml_pre(x, mods, g, wqkv, wog, wif.astype(BF16), bif, l, off, n)
                if off == 0:
                    hf, hb, c_new, n_new, m_new = _ml_scan(q, k, kt, v, gates, nb, seq, emit_state=True,
                                                           group=2 * ML_SEQ_GROUP)
                    outs["C"] = c_new.reshape(BATCH, 1, 2, ML_HEADS, ML_DK, ML_DK)
                    outs["n"] = n_new.reshape(BATCH, 1, 2, ML_HEADS, ML_DK)
                    outs["m"] = m_new[:, 0, :N_STATE].reshape(BATCH, 1, 2, ML_HEADS)
                else:
                    init = (state_mlstm_C[:, j].reshape(DEC_BATCH, N_STATE, ML_DK, ML_DK),
                            state_mlstm_n[:, j].reshape(DEC_BATCH, N_STATE, ML_DK),
                            jnp.pad(state_mlstm_m[:, j].reshape(DEC_BATCH, 1, N_STATE),
                                    ((0, 0), (0, 0), (0, LANE - N_STATE))))
                    hf, hb = _ml_scan(q, k, kt, v, gates, nb, seq, init=init)
                x = _ml_post(x, hf, hb, og, mods, ml_head_g[j], wout, l, off, n)
        elif kind == 1:
            cc, sc = _dft_tables(FG)
            wout = fn_w_out[j].astype(BF16)
            cs, ss = _dft_tables(SEQ)
            x = _fn_fused(x, mods, g, cc, sc, cs, ss, wout, fn_b_out[j], l, 0, n_p, SEQ)
            a, b = _fn_chan(x, mods, g, cc, sc, l, n_p, n_s)
            cs, ss = _dft_tables(DEC_SEQ)
            x = _fn_seq(x, a, b, cs, ss, mods, wout, fn_b_out[j], l, NP_TOK, DEC_BATCH, DEC_SEQ, TOK_TILE)
        elif kind == 2:
            x = _gmlp(x, mods, g, gm_w_in[j].astype(BF16), gm_b_in[j], gm_v_g[j],
                      gm_w_s[j].astype(BF16), gm_b_s[j].T, gm_w_out[j].astype(BF16), l, tm=2 * TOK_TILE)
        else:
            wqkv, wout = na_w_qkv[j].astype(BF16), na_w_out[j].astype(BF16)
            q, k, v, k_heads, v_heads = _na_pre(x, mods, g, wqkv, l, 0, n_p, True)
            per_head = lambda t: jnp.transpose(t.reshape(BATCH, 1, NA_HEADS, NA_HD, SEQ), (0, 1, 4, 2, 3))
            outs["k"], outs["v"] = per_head(k_heads), per_head(v_heads)
            x = _ctx_attn(x, q, k, v, mods, wout, l)
            q, k, v = _na_pre(x, mods, g, wqkv, l, n_p, n_s, False)
            att = _na_latent(q, k, v,
                             cache_na_k[:, j].reshape(DEC_BATCH, PAST_LEN, D).astype(BF16),
                             cache_na_v[:, j].reshape(DEC_BATCH, PAST_LEN, D).astype(BF16),
                             _na_bias_tables(na_rpb[j]))
            x = _proj_residual(x, att, mods, wout, l, n_p, n_s)
        if l < DEPTH - 1:
            x, wb = _ffn(x, mods, norm_g[l, 2], wb, l, 1, 0, N_TOK // FFN_TILE, next_f32=(*ffn_f32, l + 1, 0))
        else:
            y_p = _ffn(x, mods, norm_g[l, 2], wb, l, 1, 0, NP_TOK // FFN_TILE, final_g=final_g)
            y_s = _ffn(x, mods, norm_g[l, 2], wb, l, 1, NP_TOK // FFN_TILE, NS_TOK // FFN_TILE, final_g=final_g)
    return (y_p.reshape(BATCH, SEQ, D), y_s.reshape(DEC_BATCH, DEC_SEQ, D),
            outs["C"], outs["n"], outs["m"], outs["k"], outs["v"])
```

```python
import functools

import numpy as np
import jax
import jax.numpy as jnp
from jax import lax
from jax.experimental import pallas as pl
from jax.experimental.pallas import tpu as pltpu

D = 1024
BATCH = 32
SEQ = 256
DEPTH = 4
DEC_BATCH = 2
DEC_SEQ = 2048
PAST_LEN = 512
GRID_W = 64
D_FF = 2816
EPS = 1e-6

NP_TOK = BATCH * SEQ
NS_TOK = DEC_BATCH * DEC_SEQ
N_TOK = NP_TOK + NS_TOK

ML_HEADS = 4
ML_DK = D // ML_HEADS
ML_CHUNK = 128
N_STATE = 2 * ML_HEADS
ML_SEQ_GROUP = 2

FOURIER_GROUPS = 4
FG = D // FOURIER_GROUPS

GM_W = D
GM_GROUPS = 4
GM_CHUNK = 128

NA_HEADS = 16
NA_HD = D // NA_HEADS
NA_KH = 8
NA_KW = 16
NA_ROWS = DEC_SEQ // GRID_W
NA_WIN = NA_KH * GRID_W
NA_PAIRS = NA_HEADS // 2
NA_ROWS_PER_STEP = 2
N_COND = 8

LANE = 128
BF16_ROWS = 16
TOK_TILE = 512
ROW_GROUP = 256
ADALN_COLS = 2304
FFN_ROWS = 256
FFN_TILE = 1024
FFN_CAST_CHUNKS = 8
VMEM_LIMIT = 60 * 1024 * 1024

F32 = jnp.float32
BF16 = jnp.bfloat16
NEG_INF = float("-inf")


def _params(n_axes, last_arbitrary=False):
    sem = ["parallel"] * n_axes
    if last_arbitrary:
        sem[-1] = "arbitrary"
    return pltpu.CompilerParams(dimension_semantics=tuple(sem), vmem_limit_bytes=VMEM_LIMIT)


def _resident(shape):
    zeros = (0,) * len(shape)
    return pl.BlockSpec(shape, lambda *_: zeros, pipeline_mode=pl.Buffered(1))


def _tile_cond(i, tm):
    row0 = i * tm
    return jnp.where(row0 < NP_TOK, 0, 1 + (row0 - NP_TOK) // DEC_SEQ)


def _x_spec(tm, off):
    return pl.BlockSpec((tm, D), lambda i: (i + off, 0))


def _mods_spec(layer, tm, off):
    return pl.BlockSpec((None, 9, D), lambda i: (layer * N_COND + _tile_cond(i + off, tm), 0, 0))


def _dot(a, b):
    return jnp.dot(a, b, preferred_element_type=F32)


def _dot_nt(a, b):
    return lax.dot_general(a, b, (((1,), (1,)), ((), ())), preferred_element_type=F32)


def _rms(x):
    return x * lax.rsqrt(jnp.mean(x * x, axis=-1, keepdims=True) + EPS)


def _mod_norm(x, g_ref, mods_ref, idx):
    h = _rms(x) * g_ref[...]
    return h * (1.0 + mods_ref[3 * idx + 1:3 * idx + 2, :]) + mods_ref[3 * idx:3 * idx + 1, :]


def _gate(mods_ref, idx):
    return mods_ref[3 * idx + 2:3 * idx + 3, :]


def _row_groups(n_rows, group=ROW_GROUP):
    return [slice(r0, r0 + group) for r0 in range(0, n_rows, group)]


def _adaln_kernel(c_ref, w_ref, b_ref, o_ref):
    c = c_ref[...]
    s = (c * jax.nn.sigmoid(c)).astype(BF16)
    o_ref[...] = _dot(s, w_ref[...].astype(BF16)) + b_ref[...]


def _adaln(cond, w_ada, b_ada):
    tn = ADALN_COLS
    nj = 9 * D // tn
    return pl.pallas_call(
        _adaln_kernel,
        grid=(DEPTH, nj),
        in_specs=[
            pl.BlockSpec((N_COND, D), lambda l, j: (0, 0)),
            pl.BlockSpec((None, D, tn), lambda l, j: (l, 0, j)),
            pl.BlockSpec((None, 1, tn), lambda l, j: (l, 0, j)),
        ],
        out_specs=pl.BlockSpec((None, N_COND, tn), lambda l, j: (l, 0, j)),
        out_shape=jax.ShapeDtypeStruct((DEPTH, N_COND, 9 * D), F32),
        compiler_params=_params(2),
        name="adaln",
    )(cond, w_ada, b_ada.reshape(DEPTH, 1, 9 * D))


def _ffn_kernel(*refs, idx, final, convert_next, split_at):
    if split_at is None:
        x_ref, refs = refs[0], refs[1:]
        load_x = lambda rows: x_ref[rows, :]
    else:
        (xa_ref, xb_ref), refs = refs[:2], refs[2:]
        first = pl.program_id(0) < split_at
        load_x = lambda rows: jnp.where(first, xa_ref[rows, :], xb_ref[rows, :])
    mods_ref, g_ref, w1_ref, w3_ref, w2_ref = refs[:5]
    rest = refs[5:]
    if convert_next:
        (nw1_ref, nw3_ref, nw2_ref), rest = rest[:3], rest[3:]
        cast_out, rest = rest[-3:], rest[:-3]
    o_ref = rest[-1]
    for rows in _row_groups(o_ref.shape[0], FFN_ROWS):
        x = load_x(rows)
        hb = _mod_norm(x, g_ref, mods_ref, idx).astype(BF16)
        a = _dot(hb, w1_ref[...])
        b = _dot(hb, w3_ref[...])
        act = (a * jax.nn.sigmoid(a) * b).astype(BF16)
        y = x + (0.5 * _gate(mods_ref, idx)) * _dot(act, w2_ref[...])
        if final:
            y = _rms(y) * rest[0][...]
        o_ref[rows, :] = y
    if convert_next:
        for src, dst in zip((nw1_ref, nw3_ref, nw2_ref), cast_out):
            dst[...] = src[...].astype(BF16)


def _ffn(x, mods, g, wb, layer, f, off, n, final_g=None, next_f32=None, tm=FFN_TILE):
    final = final_g is not None
    convert_next = next_f32 is not None
    idx = 2 * f
    split_at = None
    if isinstance(x, tuple):
        xa, xb = x
        split_at = xa.shape[0] // tm
        assert not final and off == 0 and xa.shape[0] % tm == 0 and n * tm == xa.shape[0] + xb.shape[0]
        x_specs = [pl.BlockSpec((tm, D), lambda i: (jnp.minimum(i, split_at - 1), 0)),
                   pl.BlockSpec((tm, D), lambda i: (jnp.maximum(i - split_at, 0), 0))]
        x_args = [xa, xb]
    else:
        x_specs, x_args = [_x_spec(tm, off)], [x]
    in_specs = x_specs + [_mods_spec(layer, tm, off), _resident((1, D)),
                          _resident((D, D_FF)), _resident((D, D_FF)), _resident((D_FF, D))]
    args = x_args + [mods, g.reshape(1, D), *wb]
    if convert_next:
        nw1, nw3, nw2, nl, nf = next_f32
        chunk = lambda i: jnp.minimum(i, FFN_CAST_CHUNKS - 1)
        for w in (nw1, nw3, nw2):
            rows, cols = w.shape[2] // FFN_CAST_CHUNKS, w.shape[3]
            in_specs.append(pl.BlockSpec((None, None, rows, cols), lambda i: (nl, nf, chunk(i), 0)))
            args.append(w)
    if final:
        in_specs.append(_resident((1, D)))
        args.append(final_g.reshape(1, D))
        out_specs = [pl.BlockSpec((tm, D), lambda i: (i, 0))]
        out_shape = [jax.ShapeDtypeStruct((n * tm, D), F32)]
        aliases = {}
    elif split_at is not None:
        out_specs = [_x_spec(tm, 0)]
        out_shape = [jax.ShapeDtypeStruct((n * tm, D), F32)]
        aliases = {}
    else:
        out_specs = [_x_spec(tm, off)]
        out_shape = [jax.ShapeDtypeStruct(x.shape, F32)]
        aliases = {0: 0}
    if convert_next:
        assert n >= FFN_CAST_CHUNKS
        for w in next_f32[:3]:
            rows, cols = w.shape[2] // FFN_CAST_CHUNKS, w.shape[3]
            out_specs.append(pl.BlockSpec((rows, cols), lambda i: (chunk(i), 0)))
            out_shape.append(jax.ShapeDtypeStruct(w.shape[2:], BF16))
    outs = pl.pallas_call(
        functools.partial(_ffn_kernel, idx=idx, final=final, convert_next=convert_next, split_at=split_at),
        grid=(n,), in_specs=in_specs, out_specs=out_specs, out_shape=out_shape,
        input_output_aliases=aliases, compiler_params=_params(1, last_arbitrary=True), name="ffn",
    )(*args)
    return (outs[0], tuple(outs[1:])) if convert_next else outs[0]


def _proj_kernel(x_ref, a_ref, mods_ref, w_ref, o_ref):
    y = _dot(a_ref[...].astype(BF16), w_ref[...])
    o_ref[...] = x_ref[...] + _gate(mods_ref, 1) * y


def _proj_residual(x, a, mods, w, layer, off, n, tm=TOK_TILE):
    return pl.pallas_call(
        _proj_kernel,
        grid=(n,),
        in_specs=[_x_spec(tm, off), pl.BlockSpec((tm, D), lambda i: (i, 0)),
                  _mods_spec(layer, tm, off), _resident((D, D))],
        out_specs=_x_spec(tm, off),
        out_shape=jax.ShapeDtypeStruct(x.shape, F32),
        input_output_aliases={0: 0}, compiler_params=_params(1), name="proj_residual",
    )(x, a, mods, w)


def _ml_pre_kernel(x_ref, mods_ref, g_ref, wqkv_ref, wog_ref, wif_ref, bif_ref,
                   q_ref, k_ref, kt_ref, v_ref, og_ref, gates_ref):
    for rows in _row_groups(x_ref.shape[0]):
        hb = _mod_norm(x_ref[rows, :], g_ref, mods_ref, 1).astype(BF16)
        qkv = _dot(hb, wqkv_ref[...])
        q_ref[rows, :] = qkv[:, :D].astype(BF16)
        k = qkv[:, D:2 * D] * (ML_DK ** -0.5)
        k_ref[rows, :] = k.astype(BF16)
        for ch in range(rows.start // ML_CHUNK, rows.stop // ML_CHUNK):
            kt_ref[ch] = k[ch * ML_CHUNK - rows.start:(ch + 1) * ML_CHUNK - rows.start, :].T.astype(BF16)
        v_ref[rows, :] = qkv[:, 2 * D:].astype(BF16)
        og_ref[rows, :] = jax.nn.sigmoid(_dot(hb, wog_ref[...])).astype(BF16)
        gates_ref[rows, :] = _dot(hb, wif_ref[...]) + bif_ref[...]


def _ml_pre(x, mods, g, wqkv, wog, wif, bif, layer, off, n, tm=TOK_TILE):
    tok = pl.BlockSpec((tm, D), lambda i: (i, 0))
    return pl.pallas_call(
        _ml_pre_kernel,
        grid=(n,),
        in_specs=[_x_spec(tm, off), _mods_spec(layer, tm, off), _resident((1, D)),
                  _resident((D, 3 * D)), _resident((D, D)), _resident((D, 2 * LANE)), _resident((1, 2 * LANE))],
        out_specs=[tok, tok, pl.BlockSpec((tm // ML_CHUNK, D, ML_CHUNK), lambda i: (i, 0, 0)), tok, tok,
                   pl.BlockSpec((tm, 2 * LANE), lambda i: (i, 0))],
        out_shape=[jax.ShapeDtypeStruct((n * tm, D), BF16)] * 2
        + [jax.ShapeDtypeStruct((n * tm // ML_CHUNK, D, ML_CHUNK), BF16), jax.ShapeDtypeStruct((n * tm, D), BF16),
           jax.ShapeDtypeStruct((n * tm, D), BF16), jax.ShapeDtypeStruct((n * tm, 2 * LANE), F32)],
        compiler_params=_params(1), name="mlstm_pre",
    )(x, mods, g.reshape(1, D), wqkv, wog, wif, bif)


def _log_sigmoid(x):
    return jnp.minimum(x, 0.0) - jnp.log1p(jnp.exp(-jnp.abs(x)))


def _scan_rows(x, fwd_lanes, op, identity):
    n = x.shape[0]
    row = lax.broadcasted_iota(jnp.int32, x.shape, 0)
    sh = 1
    while sh < n:
        prev = jnp.where(row >= sh, pltpu.roll(x, sh, 0), identity)
        nxt = jnp.where(row < n - sh, pltpu.roll(x, n - sh, 0), identity)
        x = op(x, jnp.where(fwd_lanes, prev, nxt))
        sh *= 2
    return x


def _ml_gates(fwd_refs, bwd_refs, m_sc, bi):
    L = ML_CHUNK
    lane = lax.broadcasted_iota(jnp.int32, (L, LANE), 1)
    fwd_lanes = lane < ML_HEADS
    gf_ref, gb_ref = fwd_refs[4], bwd_refs[4]
    i_pre = jnp.where(fwd_lanes, gf_ref[bi, :, :LANE], gb_ref[bi, :, :LANE])
    f_pre = jnp.where(fwd_lanes, gf_ref[bi, :, LANE:], gb_ref[bi, :, LANE:])
    log_f = jnp.where(lane < N_STATE, _log_sigmoid(f_pre), 0.0)
    bsum = _scan_rows(log_f, fwd_lanes, jnp.add, 0.0)
    rel = i_pre - bsum
    m_old = m_sc[bi]
    mm = jnp.maximum(m_old, _scan_rows(rel, fwd_lanes, jnp.maximum, NEG_INF))
    mx = jnp.maximum(m_old, jnp.max(rel, axis=0, keepdims=True))
    b_last = jnp.where(fwd_lanes[0:1], bsum[L - 1:L, :], bsum[0:1, :])
    m_sc[bi] = b_last + mx
    return dict(mm=mm, mx=mx,
                w_inter=jnp.exp(m_old - mm),
                floor=jnp.exp(-(bsum + mm)),
                decay=jnp.exp(m_old - mx),
                rel_t=rel.T)


def _ml_chain(refs, h_ref, c_sc, n_sc, gt, bi, d, hd):
    L = ML_CHUNK
    q_ref, k_ref, kt_ref, v_ref, _ = refs
    t_idx = lax.broadcasted_iota(jnp.int32, (L, L), 0)
    s_idx = lax.broadcasted_iota(jnp.int32, (L, L), 1)
    visible = (s_idx >= t_idx) if d == 1 else (s_idx <= t_idx)
    r = d * ML_HEADS + hd
    lo, hi = hd * ML_DK, (hd + 1) * ML_DK
    rel_row = gt["rel_t"][r:r + 1, :]
    w_col = gt["w_inter"][:, r:r + 1]
    n_old = n_sc[bi, r:r + 1, :]
    c_old = c_sc[bi, r]
    qh, kh, vh = q_ref[bi, :, lo:hi], k_ref[bi, :, lo:hi], v_ref[bi, :, lo:hi]
    kth = kt_ref[bi, lo:hi, :]

    a = jnp.exp(jnp.where(visible, rel_row - gt["mm"][:, r:r + 1], NEG_INF)) * _dot(qh, kth)
    num = _dot(a.astype(BF16), vh) + w_col * _dot(qh, c_old.astype(BF16))
    den = jnp.sum(a, axis=1, keepdims=True) + w_col * jnp.sum(qh.astype(F32) * n_old, axis=1, keepdims=True)
    h_ref[bi, :, lo:hi] = (num * (1.0 / jnp.maximum(jnp.abs(den), gt["floor"][:, r:r + 1]))).astype(BF16)

    w_row = jnp.exp(rel_row - gt["mx"][:, r:r + 1])
    dec = gt["decay"][:, r:r + 1]
    c_sc[bi, r] = dec * c_old + _dot((kth.astype(F32) * w_row).astype(BF16), vh)
    w_rows = jnp.broadcast_to(w_row, (BF16_ROWS, L)).astype(BF16)
    n_sc[bi, r:r + 1, :] = dec * n_old + _dot(w_rows, kh)[0:1, :]


def _ml_scan_kernel(*refs, with_init, emit_state):
    fwd_refs, bwd_refs, refs = refs[:5], refs[5:10], refs[10:]
    if with_init:
        (c0_ref, n0_ref, m0_ref), refs = refs[:3], refs[3:]
    (hf_ref, hb_ref), refs = refs[:2], refs[2:]
    if emit_state:
        (co_ref, no_ref, mo_ref), refs = refs[:3], refs[3:]
    c_sc, n_sc, m_sc = refs
    c = pl.program_id(1)

    @pl.when(c == 0)
    def _():
        if with_init:
            c_sc[...] = c0_ref[...]
            n_sc[...] = n0_ref[...]
            m_sc[...] = m0_ref[...]
        else:
            c_sc[...] = jnp.zeros_like(c_sc)
            n_sc[...] = jnp.zeros_like(n_sc)
            m_sc[...] = jnp.zeros_like(m_sc)

    seqs = range(c_sc.shape[0])
    gates = [_ml_gates(fwd_refs, bwd_refs, m_sc, bi) for bi in seqs]
    for d, (refs_d, h_ref) in enumerate(((fwd_refs, hf_ref), (bwd_refs, hb_ref))):
        for hd in range(ML_HEADS):
            for bi in seqs:
                _ml_chain(refs_d, h_ref, c_sc, n_sc, gates[bi], bi, d, hd)

    if emit_state:
        @pl.when(c == pl.num_programs(1) - 1)
        def _():
            co_ref[...] = c_sc[...]
            no_ref[...] = n_sc[...]
            mo_ref[...] = m_sc[...]


def _ml_scan(q, k, kt, v, gates, nb, seq, init=None, emit_state=False, group=ML_SEQ_GROUP):
    nc = seq // ML_CHUNK
    per_chunk = lambda a: a.reshape(nb, nc, *a.shape[-2:]) if a.ndim == 3 else a.reshape(nb, nc, ML_CHUNK, -1)
    fwd = lambda b, c: (b, c, 0, 0)
    bwd = lambda b, c: (b, nc - 1 - c, 0, 0)
    blk = lambda a, im: pl.BlockSpec((group, None) + a.shape[2:], im)
    st_c = pl.BlockSpec((group, N_STATE, ML_DK, ML_DK), lambda b, c: (b, 0, 0, 0))
    st_n = pl.BlockSpec((group, N_STATE, ML_DK), lambda b, c: (b, 0, 0))
    st_m = pl.BlockSpec((group, 1, LANE), lambda b, c: (b, 0, 0))
    arrays = [per_chunk(a) for a in (q, k, kt, v, gates)]
    in_specs = [blk(a, fwd) for a in arrays] + [blk(a, bwd) for a in arrays]
    args = arrays * 2
    if init is not None:
        in_specs += [st_c, st_n, st_m]
        args += list(init)
    h_shape = jax.ShapeDtypeStruct((nb, nc, ML_CHUNK, D), BF16)
    out_specs = [blk(h_shape, fwd), blk(h_shape, bwd)]
    out_shape = [h_shape] * 2
    if emit_state:
        out_specs += [st_c, st_n, st_m]
        out_shape += [jax.ShapeDtypeStruct((nb, N_STATE, ML_DK, ML_DK), F32),
                      jax.ShapeDtypeStruct((nb, N_STATE, ML_DK), F32),
                      jax.ShapeDtypeStruct((nb, 1, LANE), F32)]
    outs = pl.pallas_call(
        functools.partial(_ml_scan_kernel, with_init=init is not None, emit_state=emit_state),
        grid=(nb // group, nc), in_specs=in_specs, out_specs=out_specs, out_shape=out_shape,
        scratch_shapes=[pltpu.VMEM((group, N_STATE, ML_DK, ML_DK), F32),
                        pltpu.VMEM((group, N_STATE, ML_DK), F32), pltpu.VMEM((group, 1, LANE), F32)],
        compiler_params=_params(2, last_arbitrary=True), name="mlstm_scan",
    )(*args)
    return [outs[0].reshape(nb * seq, D), outs[1].reshape(nb * seq, D)] + list(outs[2:])


def _ml_post_kernel(x_ref, hf_ref, hb_ref, og_ref, mods_ref, hg_ref, w_ref, o_ref):
    for rows in _row_groups(x_ref.shape[0]):
        hsum = hf_ref[rows, :].astype(F32) + hb_ref[rows, :].astype(F32)
        hn = jnp.concatenate(
            [_rms(hsum[:, hd * ML_DK:(hd + 1) * ML_DK]) for hd in range(ML_HEADS)], axis=1)
        y = _dot((og_ref[rows, :].astype(F32) * (hn * hg_ref[...])).astype(BF16), w_ref[...])
        o_ref[rows, :] = x_ref[rows, :] + _gate(mods_ref, 1) * y


def _ml_post(x, hf, hb, og, mods, head_g, w_out, layer, off, n, tm=TOK_TILE):
    tok = pl.BlockSpec((tm, D), lambda i: (i, 0))
    return pl.pallas_call(
        _ml_post_kernel,
        grid=(n,),
        in_specs=[_x_spec(tm, off), tok, tok, tok, _mods_spec(layer, tm, off),
                  _resident((1, D)), _resident((D, D))],
        out_specs=_x_spec(tm, off),
        out_shape=jax.ShapeDtypeStruct(x.shape, F32),
        input_output_aliases={0: 0}, compiler_params=_params(1), name="mlstm_post",
    )(x, hf, hb, og, mods, head_g.reshape(1, D), w_out)


def _fn_chan_kernel(x_ref, mods_ref, g_ref, cc_ref, sc_ref, a_ref, b_ref):
    hb = _mod_norm(x_ref[...], g_ref, mods_ref, 1).astype(BF16)
    for gi in range(FOURIER_GROUPS):
        sl = slice(gi * FG, (gi + 1) * FG)
        a_ref[:, sl] = _dot(hb[:, sl], cc_ref[...]).astype(BF16)
        b_ref[:, sl] = _dot(hb[:, sl], sc_ref[...]).astype(BF16)


def _fn_chan(x, mods, g, cc, sc, layer, off, n, tm=TOK_TILE):
    tok = pl.BlockSpec((tm, D), lambda i: (i, 0))
    return pl.pallas_call(
        _fn_chan_kernel,
        grid=(n,),
        in_specs=[_x_spec(tm, off), _mods_spec(layer, tm, off), _resident((1, D)),
                  _resident((FG, FG)), _resident((FG, FG))],
        out_specs=[tok, tok],
        out_shape=[jax.ShapeDtypeStruct((n * tm, D), BF16)] * 2,
        compiler_params=_params(1), name="fourier_chan",
    )(x, mods, g.reshape(1, D), cc, sc)


def _fn_seq_kernel(x_ref, a_ref, b_ref, cs_ref, ss_ref, mods_ref, w_ref, bias_ref, o_ref, *, scale):
    f = (_dot(cs_ref[...], a_ref[...]) - _dot(ss_ref[...], b_ref[...])) * scale
    y = _dot(f.astype(BF16), w_ref[...]) + bias_ref[...]
    o_ref[...] = x_ref[...] + _gate(mods_ref, 1) * y


def _fn_seq(x, a, b, cs, ss, mods, w_out, b_out, layer, off_tok, nb, seq, tr):
    nt = seq // tr
    off = off_tok // tr
    xs = pl.BlockSpec((tr, D), lambda bi, t: (off + bi * nt + t, 0))
    ab = pl.BlockSpec((seq, D), lambda bi, t: (bi, 0))
    tbl = pl.BlockSpec((tr, seq), lambda bi, t: (t, 0))
    mods_spec = pl.BlockSpec(
        (None, 9, D), lambda bi, t: (layer * N_COND + _tile_cond(off + bi * nt + t, tr), 0, 0))
    return pl.pallas_call(
        functools.partial(_fn_seq_kernel, scale=float((seq * FG) ** -0.5)),
        grid=(nb, nt),
        in_specs=[xs, ab, ab, tbl, tbl, mods_spec, _resident((D, D)), _resident((1, D))],
        out_specs=xs,
        out_shape=jax.ShapeDtypeStruct(x.shape, F32),
        input_output_aliases={0: 0}, compiler_params=_params(2), name="fourier_seq",
    )(x, a, b, cs, ss, mods, w_out, b_out.reshape(1, D))


def _fn_fused_kernel(x_ref, mods_ref, g_ref, cc_ref, sc_ref, cs_ref, ss_ref, w_ref, bias_ref, o_ref,
                     a_sc, b_sc, *, seq, scale):
    hb = _mod_norm(x_ref[...], g_ref, mods_ref, 1).astype(BF16)
    for gi in range(FOURIER_GROUPS):
        sl = slice(gi * FG, (gi + 1) * FG)
        a_sc[:, sl] = _dot(hb[:, sl], cc_ref[...]).astype(BF16)
        b_sc[:, sl] = _dot(hb[:, sl], sc_ref[...]).astype(BF16)
    for r0 in range(0, x_ref.shape[0], seq):
        rows = slice(r0, r0 + seq)
        f = (_dot(cs_ref[...], a_sc[rows, :]) - _dot(ss_ref[...], b_sc[rows, :])) * scale
        y = _dot(f.astype(BF16), w_ref[...]) + bias_ref[...]
        o_ref[rows, :] = x_ref[rows, :] + _gate(mods_ref, 1) * y


def _fn_fused(x, mods, g, cc, sc, cs, ss, w_out, b_out, layer, off, n, seq, tm=TOK_TILE):
    return pl.pallas_call(
        functools.partial(_fn_fused_kernel, seq=seq, scale=float((seq * FG) ** -0.5)),
        grid=(n,),
        in_specs=[_x_spec(tm, off), _mods_spec(layer, tm, off), _resident((1, D)),
                  _resident((FG, FG)), _resident((FG, FG)), _resident((seq, seq)), _resident((seq, seq)),
                  _resident((D, D)), _resident((1, D))],
        out_specs=_x_spec(tm, off),
        out_shape=jax.ShapeDtypeStruct(x.shape, F32),
        scratch_shapes=[pltpu.VMEM((tm, D), BF16), pltpu.VMEM((tm, D), BF16)],
        input_output_aliases={0: 0}, compiler_params=_params(1), name="fourier_fused",
    )(x, mods, g.reshape(1, D), cc, sc, cs, ss, w_out, b_out.reshape(1, D))


def _dft_tables(n):
    ang = 2.0 * np.pi * ((np.arange(n)[:, None] * np.arange(n)[None, :]) % n) / n
    return (jnp.asarray(np.cos(ang), F32).astype(BF16), jnp.asarray(np.sin(ang), F32).astype(BF16))


def _gm_kernel(x_ref, mods_ref, g_ref, win_ref, bin_ref, vg_ref, ws_ref, bs_ref, wout_ref, o_ref, sv_sc):
    gw = GM_W // GM_GROUPS
    groups = _row_groups(x_ref.shape[0], 2 * GM_CHUNK)
    zs = [_dot(_mod_norm(x_ref[rows, :], g_ref, mods_ref, 1).astype(BF16), win_ref[...]) + bin_ref[...]
          for rows in groups]
    us = []
    for rows, z in zip(groups, zs):
        z = z * (0.5 * (1.0 + jnp.tanh(np.sqrt(2.0 / np.pi) * (z + 0.044715 * (z * z * z)))))
        us.append(z[:, :GM_W])
        v = (_rms(z[:, GM_W:]) * vg_ref[...]).astype(BF16)
        for ch in range(2):
            crows = slice(ch * GM_CHUNK, (ch + 1) * GM_CHUNK)
            for gi in range(GM_GROUPS):
                cols = slice(gi * gw, (gi + 1) * gw)
                sv_sc[rows.start + ch * GM_CHUNK:rows.start + (ch + 1) * GM_CHUNK, cols] = (
                    _dot(ws_ref[gi], v[crows, cols]) + bs_ref[:, gi:gi + 1])
    for rows, u in zip(groups, us):
        y = _dot((u * sv_sc[rows, :]).astype(BF16), wout_ref[...])
        o_ref[rows, :] = x_ref[rows, :] + _gate(mods_ref, 1) * y


def _gmlp(x, mods, g, w_in, b_in, v_g, w_s, b_s_t, w_out, layer, tm=TOK_TILE):
    return pl.pallas_call(
        _gm_kernel,
        grid=(N_TOK // tm,),
        in_specs=[_x_spec(tm, 0), _mods_spec(layer, tm, 0), _resident((1, D)),
                  _resident((D, 2 * GM_W)), _resident((1, 2 * GM_W)), _resident((1, GM_W)),
                  _resident((GM_GROUPS, GM_CHUNK, GM_CHUNK)), _resident((GM_CHUNK, GM_GROUPS)),
                  _resident((GM_W, D))],
        out_specs=_x_spec(tm, 0),
        out_shape=jax.ShapeDtypeStruct(x.shape, F32),
        scratch_shapes=[pltpu.VMEM((tm, GM_W), F32)],
        input_output_aliases={0: 0}, compiler_params=_params(1), name="gmlp",
    )(x, mods, g.reshape(1, D), w_in, b_in.reshape(1, 2 * GM_W), v_g.reshape(1, GM_W), w_s, b_s_t, w_out)


def _na_pre_kernel(x_ref, mods_ref, g_ref, w_ref, q_ref, k_ref, v_ref, *f32_refs):
    for rows in _row_groups(x_ref.shape[0], SEQ):
        hb = _mod_norm(x_ref[rows, :], g_ref, mods_ref, 1).astype(BF16)
        qkv = _dot(hb, w_ref[...])
        q_ref[rows, :] = (qkv[:, :D] * (NA_HD ** -0.5)).astype(BF16)
        k, v = qkv[:, D:2 * D], qkv[:, 2 * D:]
        k_ref[rows, :] = k.astype(BF16)
        v_ref[rows, :] = v.astype(BF16)
        if f32_refs:
            f32_refs[0][rows.start // SEQ] = k.T
            f32_refs[1][rows.start // SEQ] = v.T


def _na_pre(x, mods, g, w_qkv, layer, off, n, emit_f32, tm=TOK_TILE):
    tok = pl.BlockSpec((tm, D), lambda i: (i, 0))
    out_specs = [tok, tok, tok]
    out_shape = [jax.ShapeDtypeStruct((n * tm, D), BF16)] * 3
    if emit_f32:
        assert tm % SEQ == 0
        out_specs += [pl.BlockSpec((tm // SEQ, D, SEQ), lambda i: (i, 0, 0))] * 2
        out_shape += [jax.ShapeDtypeStruct((n * tm // SEQ, D, SEQ), F32)] * 2
    return pl.pallas_call(
        _na_pre_kernel,
        grid=(n,),
        in_specs=[_x_spec(tm, off), _mods_spec(layer, tm, off), _resident((1, D)), _resident((D, 3 * D))],
        out_specs=out_specs, out_shape=out_shape,
        compiler_params=_params(1), name="na_qkv",
    )(x, mods, g.reshape(1, D), w_qkv)


def _half_mask(shape, e):
    lane = lax.broadcasted_iota(jnp.int32, shape, len(shape) - 1)
    return (lane < NA_HD) if e == 0 else (lane >= NA_HD)


def _ctx_attn_kernel(x_ref, q_ref, k_ref, v_ref, mods_ref, w_ref, o_ref, att_sc):
    for rows in _row_groups(x_ref.shape[0], SEQ):
        for j in range(NA_PAIRS):
            sl = slice(j * LANE, (j + 1) * LANE)
            s = _dot_nt(_pair_rows(q_ref[rows, sl]), k_ref[rows, sl])
            p = jnp.exp(s - jnp.max(s, axis=1, keepdims=True))
            o = _dot(p.astype(BF16), v_ref[rows, sl]) / jnp.sum(p, axis=1, keepdims=True)
            att_sc[rows, sl] = jnp.where(_half_mask((SEQ, LANE), 0), o[:SEQ], o[SEQ:]).astype(BF16)
    o_ref[...] = x_ref[...] + _gate(mods_ref, 1) * _dot(att_sc[...], w_ref[...])


def _ctx_attn(x, q, k, v, mods, w_out, layer, tm=TOK_TILE):
    tok = pl.BlockSpec((tm, D), lambda b: (b, 0))
    return pl.pallas_call(
        _ctx_attn_kernel,
        grid=(NP_TOK // tm,),
        in_specs=[tok, tok, tok, tok, _mods_spec(layer, tm, 0), _resident((D, D))],
        out_specs=tok,
        out_shape=jax.ShapeDtypeStruct(x.shape, F32),
        scratch_shapes=[pltpu.VMEM((tm, D), BF16)],
        input_output_aliases={0: 0}, compiler_params=_params(1), name="ctx_attn",
    )(x, q, k, v, mods, w_out)


def _na_bias_kernel(rpb_ref, o_ref, pair_sc):
    h = pl.program_id(0)
    shape = (GRID_W, LANE)
    q = lax.broadcasted_iota(jnp.int32, shape, 0)
    lane = lax.broadcasted_iota(jnp.int32, shape, 1)
    x = lane & (GRID_W - 1)
    first = lane < GRID_W
    dc = x - q + (NA_KW - 1)
    q_start = jnp.clip(q - NA_KW // 2, 0, GRID_W - NA_KW)
    in_window = (x >= q_start) & (x < q_start + NA_KW)
    n_dc = 2 * NA_KW - 1
    n_pairs = 2 * NA_KH - 2
    for dr0 in range(0, n_pairs, 5):
        drs = range(dr0, min(dr0 + 5, n_pairs))
        accs = [jnp.zeros(shape, F32) for _ in drs]
        for j in range(n_dc):
            hit = dc == j
            for a, dr in enumerate(drs):
                val = jnp.where(first, rpb_ref[h, dr * n_dc + j], rpb_ref[h, (dr + 1) * n_dc + j])
                accs[a] = jnp.where(hit, val, accs[a])
        for a, dr in enumerate(drs):
            pair_sc[dr] = jnp.where(in_window, accs[a], NEG_INF)
    for o in range(NA_KH):
        for t in range(NA_KH // 2):
            o_ref[o, :, t * LANE:(t + 1) * LANE] = pair_sc[NA_KH - 1 - o + 2 * t]


def _na_bias_tables(rpb):
    n_rel = (2 * NA_KH - 1) * (2 * NA_KW - 1)
    t = pl.pallas_call(
        _na_bias_kernel,
        grid=(NA_HEADS,),
        in_specs=[pl.BlockSpec(memory_space=pltpu.SMEM)],
        out_specs=pl.BlockSpec((NA_KH, None, GRID_W, NA_WIN), lambda h: (0, h, 0, 0)),
        out_shape=jax.ShapeDtypeStruct((NA_KH, NA_HEADS, GRID_W, NA_WIN), F32),
        scratch_shapes=[pltpu.VMEM((2 * NA_KH - 2, GRID_W, LANE), F32)],
        compiler_params=_params(1), name="na_bias",
    )(rpb.reshape(NA_HEADS, n_rel))
    return t.reshape(NA_KH, NA_HEADS * GRID_W, NA_WIN)


def _pair_rows(qp):
    zero = jnp.zeros_like(qp)
    return jnp.concatenate([jnp.where(_half_mask(qp.shape, 0), qp, zero),
                            jnp.where(_half_mask(qp.shape, 1), qp, zero)], axis=0)


def _na_lat_kernel(q_ref, k_ref, v_ref, kc_ref, vc_ref, *rest):
    bias_refs, (o_ref, p_sc) = rest[:NA_ROWS_PER_STEP], rest[NA_ROWS_PER_STEP:]
    for i, bias_ref in enumerate(bias_refs):
        rows = slice(i * GRID_W, (i + 1) * GRID_W)
        _na_lat_row(q_ref, k_ref, v_ref, kc_ref, vc_ref, bias_ref, o_ref, p_sc,
                    pl.program_id(1) * NA_ROWS_PER_STEP + i, rows)


def _na_lat_row(q_ref, k_ref, v_ref, kc_ref, vc_ref, bias_ref, o_ref, p_sc, r, rows):
    start = pl.multiple_of(jnp.clip(r - NA_KH // 2, 0, NA_ROWS - NA_KH) * GRID_W, GRID_W)
    inv = []
    for j in range(NA_PAIRS):
        sl = slice(j * LANE, (j + 1) * LANE)
        qb = _pair_rows(q_ref[rows, sl])
        s_w = _dot_nt(qb, k_ref[pl.ds(start, NA_WIN), sl]) + bias_ref[sl, :]
        s_c = _dot_nt(qb, kc_ref[:, sl])
        m = jnp.maximum(jnp.max(s_w, axis=1, keepdims=True), jnp.max(s_c, axis=1, keepdims=True))
        p_w = jnp.exp(s_w - m)
        p_c = jnp.exp(s_c - m)
        inv.append(1.0 / (jnp.sum(p_w, axis=1, keepdims=True) + jnp.sum(p_c, axis=1, keepdims=True)))
        p_sc[sl, :NA_WIN] = p_w.astype(BF16)
        p_sc[sl, NA_WIN:] = p_c.astype(BF16)
    for j in range(NA_PAIRS):
        sl = slice(j * LANE, (j + 1) * LANE)
        o = (_dot(p_sc[sl, :NA_WIN], v_ref[pl.ds(start, NA_WIN), sl])
             + _dot(p_sc[sl, NA_WIN:], vc_ref[:, sl])) * inv[j]
        o_ref[rows, sl] = jnp.where(_half_mask((GRID_W, LANE), 0), o[:GRID_W], o[GRID_W:]).astype(BF16)


def _na_latent(q, k, v, kc, vc, bias):
    row_class = lambda r: r - jnp.clip(r - NA_KH // 2, 0, NA_ROWS - NA_KH)
    seq_kv = pl.BlockSpec((None, DEC_SEQ, D), lambda b, t: (b, 0, 0))
    ctx_kv = pl.BlockSpec((None, PAST_LEN, D), lambda b, t: (b, 0, 0))
    n_rows = NA_HEADS * GRID_W
    steps = NA_ROWS // NA_ROWS_PER_STEP
    tok = pl.BlockSpec((NA_ROWS_PER_STEP * GRID_W, D), lambda b, t: (b * steps + t, 0))
    bias_spec = lambda i: pl.BlockSpec(
        (None, n_rows, NA_WIN), lambda b, t: (row_class(t * NA_ROWS_PER_STEP + i), 0, 0))
    return pl.pallas_call(
        _na_lat_kernel,
        grid=(DEC_BATCH, steps),
        in_specs=[tok, seq_kv, seq_kv, ctx_kv, ctx_kv] + [bias_spec(i) for i in range(NA_ROWS_PER_STEP)],
        out_specs=tok,
        out_shape=jax.ShapeDtypeStruct((NS_TOK, D), BF16),
        scratch_shapes=[pltpu.VMEM((n_rows, NA_WIN + PAST_LEN), BF16)],
        compiler_params=_params(2), name="na_latent",
    )(q, k.reshape(DEC_BATCH, DEC_SEQ, D), v.reshape(DEC_BATCH, DEC_SEQ, D), kc, vc,
      *([bias] * NA_ROWS_PER_STEP))


def kernel(x_prompt, x_sample, state_mlstm_C, state_mlstm_n, state_mlstm_m, cache_na_k, cache_na_v, c, c_ctx, w_ada, b_ada, norm_g, final_g, ffn_w1, ffn_w3, ffn_w2, ml_w_qkv, ml_w_if, ml_b_if, ml_w_og, ml_head_g, ml_w_out, fn_w_out, fn_b_out, gm_w_in, gm_b_in, gm_v_g, gm_w_s, gm_b_s, gm_w_out, na_w_qkv, na_w_out, na_rpb):
    n_p, n_s = NP_TOK // TOK_TILE, NS_TOK // TOK_TILE
    x = (x_prompt.reshape(NP_TOK, D), x_sample.reshape(NS_TOK, D))

    cond = jnp.zeros((N_COND, D), F32).at[0].set(c_ctx).at[1:1 + DEC_BATCH].set(c)
    mods = _adaln(cond, w_ada, b_ada).reshape(DEPTH * N_COND, 9, D)

    ffn_f32 = (ffn_w1, ffn_w3, ffn_w2)
    wb = tuple(w[0, 0].astype(BF16) for w in ffn_f32)
    outs = {}
    for l in range(DEPTH):
        kind, j = l % 4, l // 4
        x, wb = _ffn(x, mods, norm_g[l, 0], wb, l, 0, 0, N_TOK // FFN_TILE, next_f32=(*ffn_f32, l, 1))
        g = norm_g[l, 1]
        if kind == 0:
            w_dir = jnp.transpose(ml_w_if[j], (1, 0, 2))
            lane_pad = ((0, 0), (0, LANE - N_STATE))
            wif = jnp.concatenate(
                [jnp.pad(w_dir[:, :, :ML_HEADS].reshape(D, N_STATE), lane_pad),
                 jnp.pad(w_dir[:, :, ML_HEADS:].reshape(D, N_STATE), lane_pad)], axis=1)
            bif = jnp.concatenate(
                [jnp.pad(ml_b_if[j][:, :ML_HEADS].reshape(1, N_STATE), lane_pad),
                 jnp.pad(ml_b_if[j][:, ML_HEADS:].reshape(1, N_STATE), lane_pad)], axis=1)
            wqkv, wog, wout = ml_w_qkv[j].astype(BF16), ml_w_og[j].astype(BF16), ml_w_out[j].astype(BF16)
            for off, n, nb, seq in ((0, n_p, BATCH, SEQ), (n_p, n_s, DEC_BATCH, DEC_SEQ)):
                q, k, kt, v, og, gates = _ml_pre(x, mods, g, wqkv, wog, wif.astype(BF16), bif, l, off, n)
                if off == 0:
                    hf, hb, c_new, n_new, m_new = _ml_scan(q, k, kt, v, gates, nb, seq, emit_state=True,
                                                           group=2 * ML_SEQ_GROUP)
                    outs["C"] = c_new.reshape(BATCH, 1, 2, ML_HEADS, ML_DK, ML_DK)
                    outs["n"] = n_new.reshape(BATCH, 1, 2, ML_HEADS, ML_DK)
                    outs["m"] = m_new[:, 0, :N_STATE].reshape(BATCH, 1, 2, ML_HEADS)
                else:
                    init = (state_mlstm_C[:, j].reshape(DEC_BATCH, N_STATE, ML_DK, ML_DK),
                            state_mlstm_n[:, j].reshape(DEC_BATCH, N_STATE, ML_DK),
                            jnp.pad(state_mlstm_m[:, j].reshape(DEC_BATCH, 1, N_STATE),
                                    ((0, 0), (0, 0), (0, LANE - N_STATE))))
                    hf, hb = _ml_scan(q, k, kt, v, gates, nb, seq, init=init)
                x = _ml_post(x, hf, hb, og, mods, ml_head_g[j], wout, l, off, n)
        elif kind == 1:
            cc, sc = _dft_tables(FG)
            wout = fn_w_out[j].astype(BF16)
            cs, ss = _dft_tables(SEQ)
            x = _fn_fused(x, mods, g, cc, sc, cs, ss, wout, fn_b_out[j], l, 0, n_p, SEQ)
            a, b = _fn_chan(x, mods, g, cc, sc, l, n_p, n_s)
            cs, ss = _dft_tables(DEC_SEQ)
            x = _fn_seq(x, a, b, cs, ss, mods, wout, fn_b_out[j], l, NP_TOK, DEC_BATCH, DEC_SEQ, TOK_TILE)
        elif kind == 2:
            x = _gmlp(x, mods, g, gm_w_in[j].astype(BF16), gm_b_in[j], gm_v_g[j],
                      gm_w_s[j].astype(BF16), gm_b_s[j].T, gm_w_out[j].astype(BF16), l, tm=2 * TOK_TILE)
        else:
            wqkv, wout = na_w_qkv[j].astype(BF16), na_w_out[j].astype(BF16)
            q, k, v, k_heads, v_heads = _na_pre(x, mods, g, wqkv, l, 0, n_p, True)
            per_head = lambda t: jnp.transpose(t.reshape(BATCH, 1, NA_HEADS, NA_HD, SEQ), (0, 1, 4, 2, 3))
            outs["k"], outs["v"] = per_head(k_heads), per_head(v_heads)
            x = _ctx_attn(x, q, k, v, mods, wout, l)
            q, k, v = _na_pre(x, mods, g, wqkv, l, n_p, n_s, False)
            att = _na_latent(q, k, v,
                             cache_na_k[:, j].reshape(DEC_BATCH, PAST_LEN, D).astype(BF16),
                             cache_na_v[:, j].reshape(DEC_BATCH, PAST_LEN, D).astype(BF16),
                             _na_bias_tables(na_rpb[j]))
            x = _proj_residual(x, att, mods, wout, l, n_p, n_s)
        if l < DEPTH - 1:
            x, wb = _ffn(x, mods, norm_g[l, 2], wb, l, 1, 0, N_TOK // FFN_TILE, next_f32=(*ffn_f32, l + 1, 0))
        else:
            y_p = _ffn(x, mods, norm_g[l, 2], wb, l, 1, 0, NP_TOK // FFN_TILE, final_g=final_g)
            y_s = _ffn(x, mods, norm_g[l, 2], wb, l, 1, NP_TOK // FFN_TILE, NS_TOK // FFN_TILE, final_g=final_g)
    return (y_p.reshape(BATCH, SEQ, D), y_s.reshape(DEC_BATCH, DEC_SEQ, D),
            outs["C"], outs["n"], outs["m"], outs["k"], outs["v"])
```

```python
import functools

import numpy as np
import jax
import jax.numpy as jnp
from jax import lax
from jax.experimental import pallas as pl
from jax.experimental.pallas import tpu as pltpu

D = 1024
BATCH = 32
SEQ = 256
DEPTH = 4
DEC_BATCH = 2
DEC_SEQ = 2048
PAST_LEN = 512
GRID_W = 64
D_FF = 2816
EPS = 1e-6

NP_TOK = BATCH * SEQ
NS_TOK = DEC_BATCH * DEC_SEQ
N_TOK = NP_TOK + NS_TOK

ML_HEADS = 4
ML_DK = D // ML_HEADS
ML_CHUNK = 128
N_STATE = 2 * ML_HEADS
ML_SEQ_GROUP = 2

FOURIER_GROUPS = 4
FG = D // FOURIER_GROUPS

GM_W = D
GM_GROUPS = 4
GM_CHUNK = 128

NA_HEADS = 16
NA_HD = D // NA_HEADS
NA_KH = 8
NA_KW = 16
NA_ROWS = DEC_SEQ // GRID_W
NA_WIN = NA_KH * GRID_W
NA_PAIRS = NA_HEADS // 2
NA_ROWS_PER_STEP = 2
N_COND = 8

LANE = 128
BF16_ROWS = 16
TOK_TILE = 1024
ROW_GROUP = 256
ADALN_COLS = 2304
FFN_ROWS = 256
FFN_TILE = 1024
FFN_CAST_CHUNKS = 8
VMEM_LIMIT = 60 * 1024 * 1024

F32 = jnp.float32
BF16 = jnp.bfloat16
NEG_INF = float("-inf")


def _params(n_axes, last_arbitrary=False):
    sem = ["parallel"] * n_axes
    if last_arbitrary:
        sem[-1] = "arbitrary"
    return pltpu.CompilerParams(dimension_semantics=tuple(sem), vmem_limit_bytes=VMEM_LIMIT)


def _resident(shape):
    zeros = (0,) * len(shape)
    return pl.BlockSpec(shape, lambda *_: zeros, pipeline_mode=pl.Buffered(1))


def _tile_cond(i, tm):
    row0 = i * tm
    return jnp.where(row0 < NP_TOK, 0, 1 + (row0 - NP_TOK) // DEC_SEQ)


def _x_spec(tm, off):
    return pl.BlockSpec((tm, D), lambda i: (i + off, 0))


def _mods_spec(layer, tm, off):
    return pl.BlockSpec((None, 9, D), lambda i: (layer * N_COND + _tile_cond(i + off, tm), 0, 0))


def _dot(a, b):
    return jnp.dot(a, b, preferred_element_type=F32)


def _dot_nt(a, b):
    return lax.dot_general(a, b, (((1,), (1,)), ((), ())), preferred_element_type=F32)


def _rms(x):
    return x * lax.rsqrt(jnp.mean(x * x, axis=-1, keepdims=True) + EPS)


def _mod_norm(x, g_ref, mods_ref, idx):
    h = _rms(x) * g_ref[...]
    return h * (1.0 + mods_ref[3 * idx + 1:3 * idx + 2, :]) + mods_ref[3 * idx:3 * idx + 1, :]


def _gate(mods_ref, idx):
    return mods_ref[3 * idx + 2:3 * idx + 3, :]


def _row_groups(n_rows, group=ROW_GROUP):
    return [slice(r0, r0 + group) for r0 in range(0, n_rows, group)]


def _adaln_kernel(c_ref, w_ref, b_ref, o_ref):
    c = c_ref[...]
    s = (c * jax.nn.sigmoid(c)).astype(BF16)
    o_ref[...] = _dot(s, w_ref[...].astype(BF16)) + b_ref[...]


def _adaln(cond, w_ada, b_ada):
    tn = ADALN_COLS
    nj = 9 * D // tn
    return pl.pallas_call(
        _adaln_kernel,
        grid=(DEPTH, nj),
        in_specs=[
            pl.BlockSpec((N_COND, D), lambda l, j: (0, 0)),
            pl.BlockSpec((None, D, tn), lambda l, j: (l, 0, j)),
            pl.BlockSpec((None, 1, tn), lambda l, j: (l, 0, j)),
        ],
        out_specs=pl.BlockSpec((None, N_COND, tn), lambda l, j: (l, 0, j)),
        out_shape=jax.ShapeDtypeStruct((DEPTH, N_COND, 9 * D), F32),
        compiler_params=_params(2),
        name="adaln",
    )(cond, w_ada, b_ada.reshape(DEPTH, 1, 9 * D))


def _ffn_kernel(*refs, idx, final, convert_next, split_at):
    if split_at is None:
        x_ref, refs = refs[0], refs[1:]
        load_x = lambda rows: x_ref[rows, :]
    else:
        (xa_ref, xb_ref), refs = refs[:2], refs[2:]
        first = pl.program_id(0) < split_at
        load_x = lambda rows: jnp.where(first, xa_ref[rows, :], xb_ref[rows, :])
    mods_ref, g_ref, w1_ref, w3_ref, w2_ref = refs[:5]
    rest = refs[5:]
    if convert_next:
        (nw1_ref, nw3_ref, nw2_ref), rest = rest[:3], rest[3:]
        cast_out, rest = rest[-3:], rest[:-3]
    o_ref = rest[-1]
    for rows in _row_groups(o_ref.shape[0], FFN_ROWS):
        x = load_x(rows)
        hb = _mod_norm(x, g_ref, mods_ref, idx).astype(BF16)
        a = _dot(hb, w1_ref[...])
        b = _dot(hb, w3_ref[...])
        act = (a * jax.nn.sigmoid(a) * b).astype(BF16)
        y = x + (0.5 * _gate(mods_ref, idx)) * _dot(act, w2_ref[...])
        if final:
            y = _rms(y) * rest[0][...]
        o_ref[rows, :] = y
    if convert_next:
        for src, dst in zip((nw1_ref, nw3_ref, nw2_ref), cast_out):
            dst[...] = src[...].astype(BF16)


def _ffn(x, mods, g, wb, layer, f, off, n, final_g=None, next_f32=None, tm=FFN_TILE):
    final = final_g is not None
    convert_next = next_f32 is not None
    idx = 2 * f
    split_at = None
    if isinstance(x, tuple):
        xa, xb = x
        split_at = xa.shape[0] // tm
        assert not final and off == 0 and xa.shape[0] % tm == 0 and n * tm == xa.shape[0] + xb.shape[0]
        x_specs = [pl.BlockSpec((tm, D), lambda i: (jnp.minimum(i, split_at - 1), 0)),
                   pl.BlockSpec((tm, D), lambda i: (jnp.maximum(i - split_at, 0), 0))]
        x_args = [xa, xb]
    else:
        x_specs, x_args = [_x_spec(tm, off)], [x]
    in_specs = x_specs + [_mods_spec(layer, tm, off), _resident((1, D)),
                          _resident((D, D_FF)), _resident((D, D_FF)), _resident((D_FF, D))]
    args = x_args + [mods, g.reshape(1, D), *wb]
    if convert_next:
        nw1, nw3, nw2, nl, nf = next_f32
        chunk = lambda i: jnp.minimum(i, FFN_CAST_CHUNKS - 1)
        for w in (nw1, nw3, nw2):
            rows, cols = w.shape[2] // FFN_CAST_CHUNKS, w.shape[3]
            in_specs.append(pl.BlockSpec((None, None, rows, cols), lambda i: (nl, nf, chunk(i), 0)))
            args.append(w)
    if final:
        in_specs.append(_resident((1, D)))
        args.append(final_g.reshape(1, D))
        out_specs = [pl.BlockSpec((tm, D), lambda i: (i, 0))]
        out_shape = [jax.ShapeDtypeStruct((n * tm, D), F32)]
        aliases = {}
    elif split_at is not None:
        out_specs = [_x_spec(tm, 0)]
        out_shape = [jax.ShapeDtypeStruct((n * tm, D), F32)]
        aliases = {}
    else:
        out_specs = [_x_spec(tm, off)]
        out_shape = [jax.ShapeDtypeStruct(x.shape, F32)]
        aliases = {0: 0}
    if convert_next:
        assert n >= FFN_CAST_CHUNKS
        for w in next_f32[:3]:
            rows, cols = w.shape[2] // FFN_CAST_CHUNKS, w.shape[3]
            out_specs.append(pl.BlockSpec((rows, cols), lambda i: (chunk(i), 0)))
            out_shape.append(jax.ShapeDtypeStruct(w.shape[2:], BF16))
    outs = pl.pallas_call(
        functools.partial(_ffn_kernel, idx=idx, final=final, convert_next=convert_next, split_at=split_at),
        grid=(n,), in_specs=in_specs, out_specs=out_specs, out_shape=out_shape,
        input_output_aliases=aliases, compiler_params=_params(1, last_arbitrary=True), name="ffn",
    )(*args)
    return (outs[0], tuple(outs[1:])) if convert_next else outs[0]


def _proj_kernel(x_ref, a_ref, mods_ref, w_ref, o_ref):
    y = _dot(a_ref[...].astype(BF16), w_ref[...])
    o_ref[...] = x_ref[...] + _gate(mods_ref, 1) * y


def _proj_residual(x, a, mods, w, layer, off, n, tm=TOK_TILE):
    return pl.pallas_call(
        _proj_kernel,
        grid=(n,),
        in_specs=[_x_spec(tm, off), pl.BlockSpec((tm, D), lambda i: (i, 0)),
                  _mods_spec(layer, tm, off), _resident((D, D))],
        out_specs=_x_spec(tm, off),
        out_shape=jax.ShapeDtypeStruct(x.shape, F32),
        input_output_aliases={0: 0}, compiler_params=_params(1), name="proj_residual",
    )(x, a, mods, w)


def _ml_pre_kernel(x_ref, mods_ref, g_ref, wqkv_ref, wog_ref, wif_ref, bif_ref,
                   q_ref, k_ref, kt_ref, v_ref, og_ref, gates_ref):
    for rows in _row_groups(x_ref.shape[0]):
        hb = _mod_norm(x_ref[rows, :], g_ref, mods_ref, 1).astype(BF16)
        qkv = _dot(hb, wqkv_ref[...])
        q_ref[rows, :] = qkv[:, :D].astype(BF16)
        k = qkv[:, D:2 * D] * (ML_DK ** -0.5)
        k_ref[rows, :] = k.astype(BF16)
        for ch in range(rows.start // ML_CHUNK, rows.stop // ML_CHUNK):
            kt_ref[ch] = k[ch * ML_CHUNK - rows.start:(ch + 1) * ML_CHUNK - rows.start, :].T.astype(BF16)
        v_ref[rows, :] = qkv[:, 2 * D:].astype(BF16)
        og_ref[rows, :] = jax.nn.sigmoid(_dot(hb, wog_ref[...])).astype(BF16)
        gates_ref[rows, :] = _dot(hb, wif_ref[...]) + bif_ref[...]


def _ml_pre(x, mods, g, wqkv, wog, wif, bif, layer, off, n, tm=TOK_TILE):
    tok = pl.BlockSpec((tm, D), lambda i: (i, 0))
    return pl.pallas_call(
        _ml_pre_kernel,
        grid=(n,),
        in_specs=[_x_spec(tm, off), _mods_spec(layer, tm, off), _resident((1, D)),
                  _resident((D, 3 * D)), _resident((D, D)), _resident((D, 2 * LANE)), _resident((1, 2 * LANE))],
        out_specs=[tok, tok, pl.BlockSpec((tm // ML_CHUNK, D, ML_CHUNK), lambda i: (i, 0, 0)), tok, tok,
                   pl.BlockSpec((tm, 2 * LANE), lambda i: (i, 0))],
        out_shape=[jax.ShapeDtypeStruct((n * tm, D), BF16)] * 2
        + [jax.ShapeDtypeStruct((n * tm // ML_CHUNK, D, ML_CHUNK), BF16), jax.ShapeDtypeStruct((n * tm, D), BF16),
           jax.ShapeDtypeStruct((n * tm, D), BF16), jax.ShapeDtypeStruct((n * tm, 2 * LANE), F32)],
        compiler_params=_params(1), name="mlstm_pre",
    )(x, mods, g.reshape(1, D), wqkv, wog, wif, bif)


def _log_sigmoid(x):
    return jnp.minimum(x, 0.0) - jnp.log1p(jnp.exp(-jnp.abs(x)))


def _scan_rows(x, fwd_lanes, op, identity):
    n = x.shape[0]
    row = lax.broadcasted_iota(jnp.int32, x.shape, 0)
    sh = 1
    while sh < n:
        prev = jnp.where(row >= sh, pltpu.roll(x, sh, 0), identity)
        nxt = jnp.where(row < n - sh, pltpu.roll(x, n - sh, 0), identity)
        x = op(x, jnp.where(fwd_lanes, prev, nxt))
        sh *= 2
    return x


def _ml_gates(fwd_refs, bwd_refs, m_sc, bi):
    L = ML_CHUNK
    lane = lax.broadcasted_iota(jnp.int32, (L, LANE), 1)
    fwd_lanes = lane < ML_HEADS
    gf_ref, gb_ref = fwd_refs[4], bwd_refs[4]
    i_pre = jnp.where(fwd_lanes, gf_ref[bi, :, :LANE], gb_ref[bi, :, :LANE])
    f_pre = jnp.where(fwd_lanes, gf_ref[bi, :, LANE:], gb_ref[bi, :, LANE:])
    log_f = jnp.where(lane < N_STATE, _log_sigmoid(f_pre), 0.0)
    bsum = _scan_rows(log_f, fwd_lanes, jnp.add, 0.0)
    rel = i_pre - bsum
    m_old = m_sc[bi]
    mm = jnp.maximum(m_old, _scan_rows(rel, fwd_lanes, jnp.maximum, NEG_INF))
    mx = jnp.maximum(m_old, jnp.max(rel, axis=0, keepdims=True))
    b_last = jnp.where(fwd_lanes[0:1], bsum[L - 1:L, :], bsum[0:1, :])
    m_sc[bi] = b_last + mx
    return dict(mm=mm, mx=mx,
                w_inter=jnp.exp(m_old - mm),
                floor=jnp.exp(-(bsum + mm)),
                decay=jnp.exp(m_old - mx),
                rel_t=rel.T)


def _ml_chain(refs, h_ref, c_sc, n_sc, gt, bi, d, hd):
    L = ML_CHUNK
    q_ref, k_ref, kt_ref, v_ref, _ = refs
    t_idx = lax.broadcasted_iota(jnp.int32, (L, L), 0)
    s_idx = lax.broadcasted_iota(jnp.int32, (L, L), 1)
    visible = (s_idx >= t_idx) if d == 1 else (s_idx <= t_idx)
    r = d * ML_HEADS + hd
    lo, hi = hd * ML_DK, (hd + 1) * ML_DK
    rel_row = gt["rel_t"][r:r + 1, :]
    w_col = gt["w_inter"][:, r:r + 1]
    n_old = n_sc[bi, r:r + 1, :]
    c_old = c_sc[bi, r]
    qh, kh, vh = q_ref[bi, :, lo:hi], k_ref[bi, :, lo:hi], v_ref[bi, :, lo:hi]
    kth = kt_ref[bi, lo:hi, :]

    a = jnp.exp(jnp.where(visible, rel_row - gt["mm"][:, r:r + 1], NEG_INF)) * _dot(qh, kth)
    num = _dot(a.astype(BF16), vh) + w_col * _dot(qh, c_old.astype(BF16))
    den = jnp.sum(a, axis=1, keepdims=True) + w_col * jnp.sum(qh.astype(F32) * n_old, axis=1, keepdims=True)
    h_ref[bi, :, lo:hi] = (num * (1.0 / jnp.maximum(jnp.abs(den), gt["floor"][:, r:r + 1]))).astype(BF16)

    w_row = jnp.exp(rel_row - gt["mx"][:, r:r + 1])
    dec = gt["decay"][:, r:r + 1]
    c_sc[bi, r] = dec * c_old + _dot((kth.astype(F32) * w_row).astype(BF16), vh)
    w_rows = jnp.broadcast_to(w_row, (BF16_ROWS, L)).astype(BF16)
    n_sc[bi, r:r + 1, :] = dec * n_old + _dot(w_rows, kh)[0:1, :]


def _ml_scan_kernel(*refs, with_init, emit_state):
    fwd_refs, bwd_refs, refs = refs[:5], refs[5:10], refs[10:]
    if with_init:
        (c0_ref, n0_ref, m0_ref), refs = refs[:3], refs[3:]
    (hf_ref, hb_ref), refs = refs[:2], refs[2:]
    if emit_state:
        (co_ref, no_ref, mo_ref), refs = refs[:3], refs[3:]
    c_sc, n_sc, m_sc = refs
    c = pl.program_id(1)

    @pl.when(c == 0)
    def _():
        if with_init:
            c_sc[...] = c0_ref[...]
            n_sc[...] = n0_ref[...]
            m_sc[...] = m0_ref[...]
        else:
            c_sc[...] = jnp.zeros_like(c_sc)
            n_sc[...] = jnp.zeros_like(n_sc)
            m_sc[...] = jnp.zeros_like(m_sc)

    seqs = range(c_sc.shape[0])
    gates = [_ml_gates(fwd_refs, bwd_refs, m_sc, bi) for bi in seqs]
    for d, (refs_d, h_ref) in enumerate(((fwd_refs, hf_ref), (bwd_refs, hb_ref))):
        for hd in range(ML_HEADS):
            for bi in seqs:
                _ml_chain(refs_d, h_ref, c_sc, n_sc, gates[bi], bi, d, hd)

    if emit_state:
        @pl.when(c == pl.num_programs(1) - 1)
        def _():
            co_ref[...] = c_sc[...]
            no_ref[...] = n_sc[...]
            mo_ref[...] = m_sc[...]


def _ml_scan(q, k, kt, v, gates, nb, seq, init=None, emit_state=False, group=ML_SEQ_GROUP):
    nc = seq // ML_CHUNK
    per_chunk = lambda a: a.reshape(nb, nc, *a.shape[-2:]) if a.ndim == 3 else a.reshape(nb, nc, ML_CHUNK, -1)
    fwd = lambda b, c: (b, c, 0, 0)
    bwd = lambda b, c: (b, nc - 1 - c, 0, 0)
    blk = lambda a, im: pl.BlockSpec((group, None) + a.shape[2:], im)
    st_c = pl.BlockSpec((group, N_STATE, ML_DK, ML_DK), lambda b, c: (b, 0, 0, 0))
    st_n = pl.BlockSpec((group, N_STATE, ML_DK), lambda b, c: (b, 0, 0))
    st_m = pl.BlockSpec((group, 1, LANE), lambda b, c: (b, 0, 0))
    arrays = [per_chunk(a) for a in (q, k, kt, v, gates)]
    in_specs = [blk(a, fwd) for a in arrays] + [blk(a, bwd) for a in arrays]
    args = arrays * 2
    if init is not None:
        in_specs += [st_c, st_n, st_m]
        args += list(init)
    h_shape = jax.ShapeDtypeStruct((nb, nc, ML_CHUNK, D), BF16)
    out_specs = [blk(h_shape, fwd), blk(h_shape, bwd)]
    out_shape = [h_shape] * 2
    if emit_state:
        out_specs += [st_c, st_n, st_m]
        out_shape += [jax.ShapeDtypeStruct((nb, N_STATE, ML_DK, ML_DK), F32),
                      jax.ShapeDtypeStruct((nb, N_STATE, ML_DK), F32),
                      jax.ShapeDtypeStruct((nb, 1, LANE), F32)]
    outs = pl.pallas_call(
        functools.partial(_ml_scan_kernel, with_init=init is not None, emit_state=emit_state),
        grid=(nb // group, nc), in_specs=in_specs, out_specs=out_specs, out_shape=out_shape,
        scratch_shapes=[pltpu.VMEM((group, N_STATE, ML_DK, ML_DK), F32),
                        pltpu.VMEM((group, N_STATE, ML_DK), F32), pltpu.VMEM((group, 1, LANE), F32)],
        compiler_params=_params(2, last_arbitrary=True), name="mlstm_scan",
    )(*args)
    return [outs[0].reshape(nb * seq, D), outs[1].reshape(nb * seq, D)] + list(outs[2:])


def _ml_post_kernel(x_ref, hf_ref, hb_ref, og_ref, mods_ref, hg_ref, w_ref, o_ref):
    for rows in _row_groups(x_ref.shape[0]):
        hsum = hf_ref[rows, :].astype(F32) + hb_ref[rows, :].astype(F32)
        hn = jnp.concatenate(
            [_rms(hsum[:, hd * ML_DK:(hd + 1) * ML_DK]) for hd in range(ML_HEADS)], axis=1)
        y = _dot((og_ref[rows, :].astype(F32) * (hn * hg_ref[...])).astype(BF16), w_ref[...])
        o_ref[rows, :] = x_ref[rows, :] + _gate(mods_ref, 1) * y


def _ml_post(x, hf, hb, og, mods, head_g, w_out, layer, off, n, tm=TOK_TILE):
    tok = pl.BlockSpec((tm, D), lambda i: (i, 0))
    return pl.pallas_call(
        _ml_post_kernel,
        grid=(n,),
        in_specs=[_x_spec(tm, off), tok, tok, tok, _mods_spec(layer, tm, off),
                  _resident((1, D)), _resident((D, D))],
        out_specs=_x_spec(tm, off),
        out_shape=jax.ShapeDtypeStruct(x.shape, F32),
        input_output_aliases={0: 0}, compiler_params=_params(1), name="mlstm_post",
    )(x, hf, hb, og, mods, head_g.reshape(1, D), w_out)


def _fn_chan_kernel(x_ref, mods_ref, g_ref, cc_ref, sc_ref, a_ref, b_ref):
    hb = _mod_norm(x_ref[...], g_ref, mods_ref, 1).astype(BF16)
    for gi in range(FOURIER_GROUPS):
        sl = slice(gi * FG, (gi + 1) * FG)
        a_ref[:, sl] = _dot(hb[:, sl], cc_ref[...]).astype(BF16)
        b_ref[:, sl] = _dot(hb[:, sl], sc_ref[...]).astype(BF16)


def _fn_chan(x, mods, g, cc, sc, layer, off, n, tm=TOK_TILE):
    tok = pl.BlockSpec((tm, D), lambda i: (i, 0))
    return pl.pallas_call(
        _fn_chan_kernel,
        grid=(n,),
        in_specs=[_x_spec(tm, off), _mods_spec(layer, tm, off), _resident((1, D)),
                  _resident((FG, FG)), _resident((FG, FG))],
        out_specs=[tok, tok],
        out_shape=[jax.ShapeDtypeStruct((n * tm, D), BF16)] * 2,
        compiler_params=_params(1), name="fourier_chan",
    )(x, mods, g.reshape(1, D), cc, sc)


def _fn_seq_kernel(x_ref, a_ref, b_ref, cs_ref, ss_ref, mods_ref, w_ref, bias_ref, o_ref, *, scale):
    f = (_dot(cs_ref[...], a_ref[...]) - _dot(ss_ref[...], b_ref[...])) * scale
    y = _dot(f.astype(BF16), w_ref[...]) + bias_ref[...]
    o_ref[...] = x_ref[...] + _gate(mods_ref, 1) * y


def _fn_seq(x, a, b, cs, ss, mods, w_out, b_out, layer, off_tok, nb, seq, tr):
    nt = seq // tr
    off = off_tok // tr
    xs = pl.BlockSpec((tr, D), lambda bi, t: (off + bi * nt + t, 0))
    ab = pl.BlockSpec((seq, D), lambda bi, t: (bi, 0))
    tbl = pl.BlockSpec((tr, seq), lambda bi, t: (t, 0))
    mods_spec = pl.BlockSpec(
        (None, 9, D), lambda bi, t: (layer * N_COND + _tile_cond(off + bi * nt + t, tr), 0, 0))
    return pl.pallas_call(
        functools.partial(_fn_seq_kernel, scale=float((seq * FG) ** -0.5)),
        grid=(nb, nt),
        in_specs=[xs, ab, ab, tbl, tbl, mods_spec, _resident((D, D)), _resident((1, D))],
        out_specs=xs,
        out_shape=jax.ShapeDtypeStruct(x.shape, F32),
        input_output_aliases={0: 0}, compiler_params=_params(2), name="fourier_seq",
    )(x, a, b, cs, ss, mods, w_out, b_out.reshape(1, D))


def _fn_fused_kernel(x_ref, mods_ref, g_ref, cc_ref, sc_ref, cs_ref, ss_ref, w_ref, bias_ref, o_ref,
                     a_sc, b_sc, *, seq, scale):
    hb = _mod_norm(x_ref[...], g_ref, mods_ref, 1).astype(BF16)
    for gi in range(FOURIER_GROUPS):
        sl = slice(gi * FG, (gi + 1) * FG)
        a_sc[:, sl] = _dot(hb[:, sl], cc_ref[...]).astype(BF16)
        b_sc[:, sl] = _dot(hb[:, sl], sc_ref[...]).astype(BF16)
    for r0 in range(0, x_ref.shape[0], seq):
        rows = slice(r0, r0 + seq)
        f = (_dot(cs_ref[...], a_sc[rows, :]) - _dot(ss_ref[...], b_sc[rows, :])) * scale
        y = _dot(f.astype(BF16), w_ref[...]) + bias_ref[...]
        o_ref[rows, :] = x_ref[rows, :] + _gate(mods_ref, 1) * y


def _fn_fused(x, mods, g, cc, sc, cs, ss, w_out, b_out, layer, off, n, seq, tm=TOK_TILE):
    return pl.pallas_call(
        functools.partial(_fn_fused_kernel, seq=seq, scale=float((seq * FG) ** -0.5)),
        grid=(n,),
        in_specs=[_x_spec(tm, off), _mods_spec(layer, tm, off), _resident((1, D)),
                  _resident((FG, FG)), _resident((FG, FG)), _resident((seq, seq)), _resident((seq, seq)),
                  _resident((D, D)), _resident((1, D))],
        out_specs=_x_spec(tm, off),
        out_shape=jax.ShapeDtypeStruct(x.shape, F32),
        scratch_shapes=[pltpu.VMEM((tm, D), BF16), pltpu.VMEM((tm, D), BF16)],
        input_output_aliases={0: 0}, compiler_params=_params(1), name="fourier_fused",
    )(x, mods, g.reshape(1, D), cc, sc, cs, ss, w_out, b_out.reshape(1, D))


def _dft_tables(n):
    ang = 2.0 * np.pi * ((np.arange(n)[:, None] * np.arange(n)[None, :]) % n) / n
    return (jnp.asarray(np.cos(ang), F32).astype(BF16), jnp.asarray(np.sin(ang), F32).astype(BF16))


def _gm_kernel(x_ref, mods_ref, g_ref, win_ref, bin_ref, vg_ref, ws_ref, bs_ref, wout_ref, o_ref, sv_sc):
    gw = GM_W // GM_GROUPS
    groups = _row_groups(x_ref.shape[0], 2 * GM_CHUNK)
    zs = [_dot(_mod_norm(x_ref[rows, :], g_ref, mods_ref, 1).astype(BF16), win_ref[...]) + bin_ref[...]
          for rows in groups]
    us = []
    for rows, z in zip(groups, zs):
        z = z * (0.5 * (1.0 + jnp.tanh(np.sqrt(2.0 / np.pi) * (z + 0.044715 * (z * z * z)))))
        us.append(z[:, :GM_W])
        v = (_rms(z[:, GM_W:]) * vg_ref[...]).astype(BF16)
        for ch in range(2):
            crows = slice(ch * GM_CHUNK, (ch + 1) * GM_CHUNK)
            for gi in range(GM_GROUPS):
                cols = slice(gi * gw, (gi + 1) * gw)
                sv_sc[rows.start + ch * GM_CHUNK:rows.start + (ch + 1) * GM_CHUNK, cols] = (
                    _dot(ws_ref[gi], v[crows, cols]) + bs_ref[:, gi:gi + 1])
    for rows, u in zip(groups, us):
        y = _dot((u * sv_sc[rows, :]).astype(BF16), wout_ref[...])
        o_ref[rows, :] = x_ref[rows, :] + _gate(mods_ref, 1) * y


def _gmlp(x, mods, g, w_in, b_in, v_g, w_s, b_s_t, w_out, layer, tm=TOK_TILE):
    return pl.pallas_call(
        _gm_kernel,
        grid=(N_TOK // tm,),
        in_specs=[_x_spec(tm, 0), _mods_spec(layer, tm, 0), _resident((1, D)),
                  _resident((D, 2 * GM_W)), _resident((1, 2 * GM_W)), _resident((1, GM_W)),
                  _resident((GM_GROUPS, GM_CHUNK, GM_CHUNK)), _resident((GM_CHUNK, GM_GROUPS)),
                  _resident((GM_W, D))],
        out_specs=_x_spec(tm, 0),
        out_shape=jax.ShapeDtypeStruct(x.shape, F32),
        scratch_shapes=[pltpu.VMEM((tm, GM_W), F32)],
        input_output_aliases={0: 0}, compiler_params=_params(1), name="gmlp",
    )(x, mods, g.reshape(1, D), w_in, b_in.reshape(1, 2 * GM_W), v_g.reshape(1, GM_W), w_s, b_s_t, w_out)


def _na_pre_kernel(x_ref, mods_ref, g_ref, w_ref, q_ref, k_ref, v_ref, *f32_refs):
    for rows in _row_groups(x_ref.shape[0], SEQ):
        hb = _mod_norm(x_ref[rows, :], g_ref, mods_ref, 1).astype(BF16)
        qkv = _dot(hb, w_ref[...])
        q_ref[rows, :] = (qkv[:, :D] * (NA_HD ** -0.5)).astype(BF16)
        k, v = qkv[:, D:2 * D], qkv[:, 2 * D:]
        k_ref[rows, :] = k.astype(BF16)
        v_ref[rows, :] = v.astype(BF16)
        if f32_refs:
            f32_refs[0][rows.start // SEQ] = k.T
            f32_refs[1][rows.start // SEQ] = v.T


def _na_pre(x, mods, g, w_qkv, layer, off, n, emit_f32, tm=TOK_TILE):
    tok = pl.BlockSpec((tm, D), lambda i: (i, 0))
    out_specs = [tok, tok, tok]
    out_shape = [jax.ShapeDtypeStruct((n * tm, D), BF16)] * 3
    if emit_f32:
        assert tm % SEQ == 0
        out_specs += [pl.BlockSpec((tm // SEQ, D, SEQ), lambda i: (i, 0, 0))] * 2
        out_shape += [jax.ShapeDtypeStruct((n * tm // SEQ, D, SEQ), F32)] * 2
    return pl.pallas_call(
        _na_pre_kernel,
        grid=(n,),
        in_specs=[_x_spec(tm, off), _mods_spec(layer, tm, off), _resident((1, D)), _resident((D, 3 * D))],
        out_specs=out_specs, out_shape=out_shape,
        compiler_params=_params(1), name="na_qkv",
    )(x, mods, g.reshape(1, D), w_qkv)


def _half_mask(shape, e):
    lane = lax.broadcasted_iota(jnp.int32, shape, len(shape) - 1)
    return (lane < NA_HD) if e == 0 else (lane >= NA_HD)


def _ctx_attn_kernel(x_ref, q_ref, k_ref, v_ref, mods_ref, w_ref, o_ref, att_sc):
    for rows in _row_groups(x_ref.shape[0], SEQ):
        for j in range(NA_PAIRS):
            sl = slice(j * LANE, (j + 1) * LANE)
            s = _dot_nt(_pair_rows(q_ref[rows, sl]), k_ref[rows, sl])
            p = jnp.exp(s - jnp.max(s, axis=1, keepdims=True))
            o = _dot(p.astype(BF16), v_ref[rows, sl]) / jnp.sum(p, axis=1, keepdims=True)
            att_sc[rows, sl] = jnp.where(_half_mask((SEQ, LANE), 0), o[:SEQ], o[SEQ:]).astype(BF16)
    o_ref[...] = x_ref[...] + _gate(mods_ref, 1) * _dot(att_sc[...], w_ref[...])


def _ctx_attn(x, q, k, v, mods, w_out, layer, tm=TOK_TILE):
    tok = pl.BlockSpec((tm, D), lambda b: (b, 0))
    return pl.pallas_call(
        _ctx_attn_kernel,
        grid=(NP_TOK // tm,),
        in_specs=[tok, tok, tok, tok, _mods_spec(layer, tm, 0), _resident((D, D))],
        out_specs=tok,
        out_shape=jax.ShapeDtypeStruct(x.shape, F32),
        scratch_shapes=[pltpu.VMEM((tm, D), BF16)],
        input_output_aliases={0: 0}, compiler_params=_params(1), name="ctx_attn",
    )(x, q, k, v, mods, w_out)


def _na_bias_kernel(rpb_ref, o_ref, pair_sc):
    h = pl.program_id(0)
    shape = (GRID_W, LANE)
    q = lax.broadcasted_iota(jnp.int32, shape, 0)
    lane = lax.broadcasted_iota(jnp.int32, shape, 1)
    x = lane & (GRID_W - 1)
    first = lane < GRID_W
    dc = x - q + (NA_KW - 1)
    q_start = jnp.clip(q - NA_KW // 2, 0, GRID_W - NA_KW)
    in_window = (x >= q_start) & (x < q_start + NA_KW)
    n_dc = 2 * NA_KW - 1
    n_pairs = 2 * NA_KH - 2
    for dr0 in range(0, n_pairs, 5):
        drs = range(dr0, min(dr0 + 5, n_pairs))
        accs = [jnp.zeros(shape, F32) for _ in drs]
        for j in range(n_dc):
            hit = dc == j
            for a, dr in enumerate(drs):
                val = jnp.where(first, rpb_ref[h, dr * n_dc + j], rpb_ref[h, (dr + 1) * n_dc + j])
                accs[a] = jnp.where(hit, val, accs[a])
        for a, dr in enumerate(drs):
            pair_sc[dr] = jnp.where(in_window, accs[a], NEG_INF)
    for o in range(NA_KH):
        for t in range(NA_KH // 2):
            o_ref[o, :, t * LANE:(t + 1) * LANE] = pair_sc[NA_KH - 1 - o + 2 * t]


def _na_bias_tables(rpb):
    n_rel = (2 * NA_KH - 1) * (2 * NA_KW - 1)
    t = pl.pallas_call(
        _na_bias_kernel,
        grid=(NA_HEADS,),
        in_specs=[pl.BlockSpec(memory_space=pltpu.SMEM)],
        out_specs=pl.BlockSpec((NA_KH, None, GRID_W, NA_WIN), lambda h: (0, h, 0, 0)),
        out_shape=jax.ShapeDtypeStruct((NA_KH, NA_HEADS, GRID_W, NA_WIN), F32),
        scratch_shapes=[pltpu.VMEM((2 * NA_KH - 2, GRID_W, LANE), F32)],
        compiler_params=_params(1), name="na_bias",
    )(rpb.reshape(NA_HEADS, n_rel))
    return t.reshape(NA_KH, NA_HEADS * GRID_W, NA_WIN)


def _pair_rows(qp):
    zero = jnp.zeros_like(qp)
    return jnp.concatenate([jnp.where(_half_mask(qp.shape, 0), qp, zero),
                            jnp.where(_half_mask(qp.shape, 1), qp, zero)], axis=0)


def _na_lat_kernel(q_ref, k_ref, v_ref, kc_ref, vc_ref, *rest):
    bias_refs, (o_ref, p_sc) = rest[:NA_ROWS_PER_STEP], rest[NA_ROWS_PER_STEP:]
    for i, bias_ref in enumerate(bias_refs):
        rows = slice(i * GRID_W, (i + 1) * GRID_W)
        _na_lat_row(q_ref, k_ref, v_ref, kc_ref, vc_ref, bias_ref, o_ref, p_sc,
                    pl.program_id(1) * NA_ROWS_PER_STEP + i, rows)


def _na_lat_row(q_ref, k_ref, v_ref, kc_ref, vc_ref, bias_ref, o_ref, p_sc, r, rows):
    start = pl.multiple_of(jnp.clip(r - NA_KH // 2, 0, NA_ROWS - NA_KH) * GRID_W, GRID_W)
    inv = []
    for j in range(NA_PAIRS):
        sl = slice(j * LANE, (j + 1) * LANE)
        qb = _pair_rows(q_ref[rows, sl])
        s_w = _dot_nt(qb, k_ref[pl.ds(start, NA_WIN), sl]) + bias_ref[sl, :]
        s_c = _dot_nt(qb, kc_ref[:, sl])
        m = jnp.maximum(jnp.max(s_w, axis=1, keepdims=True), jnp.max(s_c, axis=1, keepdims=True))
        p_w = jnp.exp(s_w - m)
        p_c = jnp.exp(s_c - m)
        inv.append(1.0 / (jnp.sum(p_w, axis=1, keepdims=True) + jnp.sum(p_c, axis=1, keepdims=True)))
        p_sc[sl, :NA_WIN] = p_w.astype(BF16)
        p_sc[sl, NA_WIN:] = p_c.astype(BF16)
    for j in range(NA_PAIRS):
        sl = slice(j * LANE, (j + 1) * LANE)
        o = (_dot(p_sc[sl, :NA_WIN], v_ref[pl.ds(start, NA_WIN), sl])
             + _dot(p_sc[sl, NA_WIN:], vc_ref[:, sl])) * inv[j]
        o_ref[rows, sl] = jnp.where(_half_mask((GRID_W, LANE), 0), o[:GRID_W], o[GRID_W:]).astype(BF16)


def _na_latent(q, k, v, kc, vc, bias):
    row_class = lambda r: r - jnp.clip(r - NA_KH // 2, 0, NA_ROWS - NA_KH)
    seq_kv = pl.BlockSpec((None, DEC_SEQ, D), lambda b, t: (b, 0, 0))
    ctx_kv = pl.BlockSpec((None, PAST_LEN, D), lambda b, t: (b, 0, 0))
    n_rows = NA_HEADS * GRID_W
    steps = NA_ROWS // NA_ROWS_PER_STEP
    tok = pl.BlockSpec((NA_ROWS_PER_STEP * GRID_W, D), lambda b, t: (b * steps + t, 0))
    bias_spec = lambda i: pl.BlockSpec(
        (None, n_rows, NA_WIN), lambda b, t: (row_class(t * NA_ROWS_PER_STEP + i), 0, 0))
    return pl.pallas_call(
        _na_lat_kernel,
        grid=(DEC_BATCH, steps),
        in_specs=[tok, seq_kv, seq_kv, ctx_kv, ctx_kv] + [bias_spec(i) for i in range(NA_ROWS_PER_STEP)],
        out_specs=tok,
        out_shape=jax.ShapeDtypeStruct((NS_TOK, D), BF16),
        scratch_shapes=[pltpu.VMEM((n_rows, NA_WIN + PAST_LEN), BF16)],
        compiler_params=_params(2), name="na_latent",
    )(q, k.reshape(DEC_BATCH, DEC_SEQ, D), v.reshape(DEC_BATCH, DEC_SEQ, D), kc, vc,
      *([bias] * NA_ROWS_PER_STEP))


def kernel(x_prompt, x_sample, state_mlstm_C, state_mlstm_n, state_mlstm_m, cache_na_k, cache_na_v, c, c_ctx, w_ada, b_ada, norm_g, final_g, ffn_w1, ffn_w3, ffn_w2, ml_w_qkv, ml_w_if, ml_b_if, ml_w_og, ml_head_g, ml_w_out, fn_w_out, fn_b_out, gm_w_in, gm_b_in, gm_v_g, gm_w_s, gm_b_s, gm_w_out, na_w_qkv, na_w_out, na_rpb):
    n_p, n_s = NP_TOK // TOK_TILE, NS_TOK // TOK_TILE
    x = (x_prompt.reshape(NP_TOK, D), x_sample.reshape(NS_TOK, D))

    cond = jnp.zeros((N_COND, D), F32).at[0].set(c_ctx).at[1:1 + DEC_BATCH].set(c)
    mods = _adaln(cond, w_ada, b_ada).reshape(DEPTH * N_COND, 9, D)

    ffn_f32 = (ffn_w1, ffn_w3, ffn_w2)
    wb = tuple(w[0, 0].astype(BF16) for w in ffn_f32)
    outs = {}
    for l in range(DEPTH):
        kind, j = l % 4, l // 4
        x, wb = _ffn(x, mods, norm_g[l, 0], wb, l, 0, 0, N_TOK // FFN_TILE, next_f32=(*ffn_f32, l, 1))
        g = norm_g[l, 1]
        if kind == 0:
            w_dir = jnp.transpose(ml_w_if[j], (1, 0, 2))
            lane_pad = ((0, 0), (0, LANE - N_STATE))
            wif = jnp.concatenate(
                [jnp.pad(w_dir[:, :, :ML_HEADS].reshape(D, N_STATE), lane_pad),
                 jnp.pad(w_dir[:, :, ML_HEADS:].reshape(D, N_STATE), lane_pad)], axis=1)
            bif = jnp.concatenate(
                [jnp.pad(ml_b_if[j][:, :ML_HEADS].reshape(1, N_STATE), lane_pad),
                 jnp.pad(ml_b_if[j][:, ML_HEADS:].reshape(1, N_STATE), lane_pad)], axis=1)
            wqkv, wog, wout = ml_w_qkv[j].astype(BF16), ml_w_og[j].astype(BF16), ml_w_out[j].astype(BF16)
            for off, n, nb, seq in ((0, n_p, BATCH, SEQ), (n_p, n_s, DEC_BATCH, DEC_SEQ)):
                q, k, kt, v, og, gates = _ml_pre(x, mods, g, wqkv, wog, wif.astype(BF16), bif, l, off, n)
                if off == 0:
                    hf, hb, c_new, n_new, m_new = _ml_scan(q, k, kt, v, gates, nb, seq, emit_state=True,
                                                           group=2 * ML_SEQ_GROUP)
                    outs["C"] = c_new.reshape(BATCH, 1, 2, ML_HEADS, ML_DK, ML_DK)
                    outs["n"] = n_new.reshape(BATCH, 1, 2, ML_HEADS, ML_DK)
                    outs["m"] = m_new[:, 0, :N_STATE].reshape(BATCH, 1, 2, ML_HEADS)
                else:
                    init = (state_mlstm_C[:, j].reshape(DEC_BATCH, N_STATE, ML_DK, ML_DK),
                            state_mlstm_n[:, j].reshape(DEC_BATCH, N_STATE, ML_DK),
                            jnp.pad(state_mlstm_m[:, j].reshape(DEC_BATCH, 1, N_STATE),
                                    ((0, 0), (0, 0), (0, LANE - N_STATE))))
                    hf, hb = _ml_scan(q, k, kt, v, gates, nb, seq, init=init)
                x = _ml_post(x, hf, hb, og, mods, ml_head_g[j], wout, l, off, n)
        elif kind == 1:
            cc, sc = _dft_tables(FG)
            wout = fn_w_out[j].astype(BF16)
            cs, ss = _dft_tables(SEQ)
            x = _fn_fused(x, mods, g, cc, sc, cs, ss, wout, fn_b_out[j], l, 0, n_p, SEQ)
            a, b = _fn_chan(x, mods, g, cc, sc, l, n_p, n_s)
            cs, ss = _dft_tables(DEC_SEQ)
            x = _fn_seq(x, a, b, cs, ss, mods, wout, fn_b_out[j], l, NP_TOK, DEC_BATCH, DEC_SEQ, TOK_TILE)
        elif kind == 2:
            x = _gmlp(x, mods, g, gm_w_in[j].astype(BF16), gm_b_in[j], gm_v_g[j],
                      gm_w_s[j].astype(BF16), gm_b_s[j].T, gm_w_out[j].astype(BF16), l)
        else:
            wqkv, wout = na_w_qkv[j].astype(BF16), na_w_out[j].astype(BF16)
            q, k, v, k_heads, v_heads = _na_pre(x, mods, g, wqkv, l, 0, n_p, True)
            per_head = lambda t: jnp.transpose(t.reshape(BATCH, 1, NA_HEADS, NA_HD, SEQ), (0, 1, 4, 2, 3))
            outs["k"], outs["v"] = per_head(k_heads), per_head(v_heads)
            x = _ctx_attn(x, q, k, v, mods, wout, l)
            q, k, v = _na_pre(x, mods, g, wqkv, l, n_p, n_s, False)
            att = _na_latent(q, k, v,
                             cache_na_k[:, j].reshape(DEC_BATCH, PAST_LEN, D).astype(BF16),
                             cache_na_v[:, j].reshape(DEC_BATCH, PAST_LEN, D).astype(BF16),
                             _na_bias_tables(na_rpb[j]))
            x = _proj_residual(x, att, mods, wout, l, n_p, n_s)
        if l < DEPTH - 1:
            x, wb = _ffn(x, mods, norm_g[l, 2], wb, l, 1, 0, N_TOK // FFN_TILE, next_f32=(*ffn_f32, l + 1, 0))
        else:
            y_p = _ffn(x, mods, norm_g[l, 2], wb, l, 1, 0, NP_TOK // FFN_TILE, final_g=final_g)
            y_s = _ffn(x, mods, norm_g[l, 2], wb, l, 1, NP_TOK // FFN_TILE, NS_TOK // FFN_TILE, final_g=final_g)
    return (y_p.reshape(BATCH, SEQ, D), y_s.reshape(DEC_BATCH, DEC_SEQ, D),
            outs["C"], outs["n"], outs["m"], outs["k"], outs["v"])
```

```python
import functools

import numpy as np
import jax
import jax.numpy as jnp
from jax import lax
from jax.experimental import pallas as pl
from jax.experimental.pallas import tpu as pltpu

D = 1024
BATCH = 32
SEQ = 256
DEPTH = 4
DEC_BATCH = 2
DEC_SEQ = 2048
PAST_LEN = 512
GRID_W = 64
D_FF = 2816
EPS = 1e-6

NP_TOK = BATCH * SEQ
NS_TOK = DEC_BATCH * DEC_SEQ
N_TOK = NP_TOK + NS_TOK

ML_HEADS = 4
ML_DK = D // ML_HEADS
ML_CHUNK = 128
N_STATE = 2 * ML_HEADS
ML_SEQ_GROUP = 2

FOURIER_GROUPS = 4
FG = D // FOURIER_GROUPS

GM_W = D
GM_GROUPS = 4
GM_CHUNK = 128

NA_HEADS = 16
NA_HD = D // NA_HEADS
NA_KH = 8
NA_KW = 16
NA_ROWS = DEC_SEQ // GRID_W
NA_WIN = NA_KH * GRID_W
NA_PAIRS = NA_HEADS // 2
NA_ROWS_PER_STEP = 4
N_COND = 8

LANE = 128
BF16_ROWS = 16
TOK_TILE = 1024
ROW_GROUP = 256
ADALN_COLS = 2304
FFN_ROWS = 256
FFN_TILE = 1024
FFN_CAST_CHUNKS = 8
VMEM_LIMIT = 60 * 1024 * 1024

F32 = jnp.float32
BF16 = jnp.bfloat16
NEG_INF = float("-inf")


def _params(n_axes, last_arbitrary=False):
    sem = ["parallel"] * n_axes
    if last_arbitrary:
        sem[-1] = "arbitrary"
    return pltpu.CompilerParams(dimension_semantics=tuple(sem), vmem_limit_bytes=VMEM_LIMIT)


def _resident(shape):
    zeros = (0,) * len(shape)
    return pl.BlockSpec(shape, lambda *_: zeros, pipeline_mode=pl.Buffered(1))


def _tile_cond(i, tm):
    row0 = i * tm
    return jnp.where(row0 < NP_TOK, 0, 1 + (row0 - NP_TOK) // DEC_SEQ)


def _x_spec(tm, off):
    return pl.BlockSpec((tm, D), lambda i: (i + off, 0))


def _mods_spec(layer, tm, off):
    return pl.BlockSpec((None, 9, D), lambda i: (layer * N_COND + _tile_cond(i + off, tm), 0, 0))


def _dot(a, b):
    return jnp.dot(a, b, preferred_element_type=F32)


def _dot_nt(a, b):
    return lax.dot_general(a, b, (((1,), (1,)), ((), ())), preferred_element_type=F32)


def _rms(x):
    return x * lax.rsqrt(jnp.mean(x * x, axis=-1, keepdims=True) + EPS)


def _mod_norm(x, g_ref, mods_ref, idx):
    h = _rms(x) * g_ref[...]
    return h * (1.0 + mods_ref[3 * idx + 1:3 * idx + 2, :]) + mods_ref[3 * idx:3 * idx + 1, :]


def _gate(mods_ref, idx):
    return mods_ref[3 * idx + 2:3 * idx + 3, :]


def _row_groups(n_rows, group=ROW_GROUP):
    return [slice(r0, r0 + group) for r0 in range(0, n_rows, group)]


def _adaln_kernel(c_ref, w_ref, b_ref, o_ref):
    c = c_ref[...]
    s = (c * jax.nn.sigmoid(c)).astype(BF16)
    o_ref[...] = _dot(s, w_ref[...].astype(BF16)) + b_ref[...]


def _adaln(cond, w_ada, b_ada):
    tn = ADALN_COLS
    nj = 9 * D // tn
    return pl.pallas_call(
        _adaln_kernel,
        grid=(DEPTH, nj),
        in_specs=[
            pl.BlockSpec((N_COND, D), lambda l, j: (0, 0)),
            pl.BlockSpec((None, D, tn), lambda l, j: (l, 0, j)),
            pl.BlockSpec((None, 1, tn), lambda l, j: (l, 0, j)),
        ],
        out_specs=pl.BlockSpec((None, N_COND, tn), lambda l, j: (l, 0, j)),
        out_shape=jax.ShapeDtypeStruct((DEPTH, N_COND, 9 * D), F32),
        compiler_params=_params(2),
        name="adaln",
    )(cond, w_ada, b_ada.reshape(DEPTH, 1, 9 * D))


def _ffn_kernel(*refs, idx, final, convert_next, split_at):
    if split_at is None:
        x_ref, refs = refs[0], refs[1:]
        load_x = lambda rows: x_ref[rows, :]
    else:
        (xa_ref, xb_ref), refs = refs[:2], refs[2:]
        first = pl.program_id(0) < split_at
        load_x = lambda rows: jnp.where(first, xa_ref[rows, :], xb_ref[rows, :])
    mods_ref, g_ref, w1_ref, w3_ref, w2_ref = refs[:5]
    rest = refs[5:]
    if convert_next:
        (nw1_ref, nw3_ref, nw2_ref), rest = rest[:3], rest[3:]
        cast_out, rest = rest[-3:], rest[:-3]
    o_ref = rest[-1]
    for rows in _row_groups(o_ref.shape[0], FFN_ROWS):
        x = load_x(rows)
        hb = _mod_norm(x, g_ref, mods_ref, idx).astype(BF16)
        a = _dot(hb, w1_ref[...])
        b = _dot(hb, w3_ref[...])
        act = (a * jax.nn.sigmoid(a) * b).astype(BF16)
        y = x + (0.5 * _gate(mods_ref, idx)) * _dot(act, w2_ref[...])
        if final:
            y = _rms(y) * rest[0][...]
        o_ref[rows, :] = y
    if convert_next:
        for src, dst in zip((nw1_ref, nw3_ref, nw2_ref), cast_out):
            dst[...] = src[...].astype(BF16)


def _ffn(x, mods, g, wb, layer, f, off, n, final_g=None, next_f32=None, tm=FFN_TILE):
    final = final_g is not None
    convert_next = next_f32 is not None
    idx = 2 * f
    split_at = None
    if isinstance(x, tuple):
        xa, xb = x
        split_at = xa.shape[0] // tm
        assert not final and off == 0 and xa.shape[0] % tm == 0 and n * tm == xa.shape[0] + xb.shape[0]
        x_specs = [pl.BlockSpec((tm, D), lambda i: (jnp.minimum(i, split_at - 1), 0)),
                   pl.BlockSpec((tm, D), lambda i: (jnp.maximum(i - split_at, 0), 0))]
        x_args = [xa, xb]
    else:
        x_specs, x_args = [_x_spec(tm, off)], [x]
    in_specs = x_specs + [_mods_spec(layer, tm, off), _resident((1, D)),
                          _resident((D, D_FF)), _resident((D, D_FF)), _resident((D_FF, D))]
    args = x_args + [mods, g.reshape(1, D), *wb]
    if convert_next:
        nw1, nw3, nw2, nl, nf = next_f32
        chunk = lambda i: jnp.minimum(i, FFN_CAST_CHUNKS - 1)
        for w in (nw1, nw3, nw2):
            rows, cols = w.shape[2] // FFN_CAST_CHUNKS, w.shape[3]
            in_specs.append(pl.BlockSpec((None, None, rows, cols), lambda i: (nl, nf, chunk(i), 0)))
            args.append(w)
    if final:
        in_specs.append(_resident((1, D)))
        args.append(final_g.reshape(1, D))
        out_specs = [pl.BlockSpec((tm, D), lambda i: (i, 0))]
        out_shape = [jax.ShapeDtypeStruct((n * tm, D), F32)]
        aliases = {}
    elif split_at is not None:
        out_specs = [_x_spec(tm, 0)]
        out_shape = [jax.ShapeDtypeStruct((n * tm, D), F32)]
        aliases = {}
    else:
        out_specs = [_x_spec(tm, off)]
        out_shape = [jax.ShapeDtypeStruct(x.shape, F32)]
        aliases = {0: 0}
    if convert_next:
        assert n >= FFN_CAST_CHUNKS
        for w in next_f32[:3]:
            rows, cols = w.shape[2] // FFN_CAST_CHUNKS, w.shape[3]
            out_specs.append(pl.BlockSpec((rows, cols), lambda i: (chunk(i), 0)))
            out_shape.append(jax.ShapeDtypeStruct(w.shape[2:], BF16))
    outs = pl.pallas_call(
        functools.partial(_ffn_kernel, idx=idx, final=final, convert_next=convert_next, split_at=split_at),
        grid=(n,), in_specs=in_specs, out_specs=out_specs, out_shape=out_shape,
        input_output_aliases=aliases, compiler_params=_params(1, last_arbitrary=True), name="ffn",
    )(*args)
    return (outs[0], tuple(outs[1:])) if convert_next else outs[0]


def _proj_kernel(x_ref, a_ref, mods_ref, w_ref, o_ref):
    y = _dot(a_ref[...].astype(BF16), w_ref[...])
    o_ref[...] = x_ref[...] + _gate(mods_ref, 1) * y


def _proj_residual(x, a, mods, w, layer, off, n, tm=TOK_TILE):
    return pl.pallas_call(
        _proj_kernel,
        grid=(n,),
        in_specs=[_x_spec(tm, off), pl.BlockSpec((tm, D), lambda i: (i, 0)),
                  _mods_spec(layer, tm, off), _resident((D, D))],
        out_specs=_x_spec(tm, off),
        out_shape=jax.ShapeDtypeStruct(x.shape, F32),
        input_output_aliases={0: 0}, compiler_params=_params(1), name="proj_residual",
    )(x, a, mods, w)


def _ml_pre_kernel(x_ref, mods_ref, g_ref, wqkv_ref, wog_ref, wif_ref, bif_ref,
                   q_ref, k_ref, kt_ref, v_ref, og_ref, gates_ref):
    for rows in _row_groups(x_ref.shape[0]):
        hb = _mod_norm(x_ref[rows, :], g_ref, mods_ref, 1).astype(BF16)
        qkv = _dot(hb, wqkv_ref[...])
        q_ref[rows, :] = qkv[:, :D].astype(BF16)
        k = qkv[:, D:2 * D] * (ML_DK ** -0.5)
        k_ref[rows, :] = k.astype(BF16)
        for ch in range(rows.start // ML_CHUNK, rows.stop // ML_CHUNK):
            kt_ref[ch] = k[ch * ML_CHUNK - rows.start:(ch + 1) * ML_CHUNK - rows.start, :].T.astype(BF16)
        v_ref[rows, :] = qkv[:, 2 * D:].astype(BF16)
        og_ref[rows, :] = jax.nn.sigmoid(_dot(hb, wog_ref[...])).astype(BF16)
        gates_ref[rows, :] = _dot(hb, wif_ref[...]) + bif_ref[...]


def _ml_pre(x, mods, g, wqkv, wog, wif, bif, layer, off, n, tm=TOK_TILE):
    tok = pl.BlockSpec((tm, D), lambda i: (i, 0))
    return pl.pallas_call(
        _ml_pre_kernel,
        grid=(n,),
        in_specs=[_x_spec(tm, off), _mods_spec(layer, tm, off), _resident((1, D)),
                  _resident((D, 3 * D)), _resident((D, D)), _resident((D, 2 * LANE)), _resident((1, 2 * LANE))],
        out_specs=[tok, tok, pl.BlockSpec((tm // ML_CHUNK, D, ML_CHUNK), lambda i: (i, 0, 0)), tok, tok,
                   pl.BlockSpec((tm, 2 * LANE), lambda i: (i, 0))],
        out_shape=[jax.ShapeDtypeStruct((n * tm, D), BF16)] * 2
        + [jax.ShapeDtypeStruct((n * tm // ML_CHUNK, D, ML_CHUNK), BF16), jax.ShapeDtypeStruct((n * tm, D), BF16),
           jax.ShapeDtypeStruct((n * tm, D), BF16), jax.ShapeDtypeStruct((n * tm, 2 * LANE), F32)],
        compiler_params=_params(1), name="mlstm_pre",
    )(x, mods, g.reshape(1, D), wqkv, wog, wif, bif)


def _log_sigmoid(x):
    return jnp.minimum(x, 0.0) - jnp.log1p(jnp.exp(-jnp.abs(x)))


def _scan_rows(x, fwd_lanes, op, identity):
    n = x.shape[0]
    row = lax.broadcasted_iota(jnp.int32, x.shape, 0)
    sh = 1
    while sh < n:
        prev = jnp.where(row >= sh, pltpu.roll(x, sh, 0), identity)
        nxt = jnp.where(row < n - sh, pltpu.roll(x, n - sh, 0), identity)
        x = op(x, jnp.where(fwd_lanes, prev, nxt))
        sh *= 2
    return x


def _ml_gates(fwd_refs, bwd_refs, m_sc, bi):
    L = ML_CHUNK
    lane = lax.broadcasted_iota(jnp.int32, (L, LANE), 1)
    fwd_lanes = lane < ML_HEADS
    gf_ref, gb_ref = fwd_refs[4], bwd_refs[4]
    i_pre = jnp.where(fwd_lanes, gf_ref[bi, :, :LANE], gb_ref[bi, :, :LANE])
    f_pre = jnp.where(fwd_lanes, gf_ref[bi, :, LANE:], gb_ref[bi, :, LANE:])
    log_f = jnp.where(lane < N_STATE, _log_sigmoid(f_pre), 0.0)
    bsum = _scan_rows(log_f, fwd_lanes, jnp.add, 0.0)
    rel = i_pre - bsum
    m_old = m_sc[bi]
    mm = jnp.maximum(m_old, _scan_rows(rel, fwd_lanes, jnp.maximum, NEG_INF))
    mx = jnp.maximum(m_old, jnp.max(rel, axis=0, keepdims=True))
    b_last = jnp.where(fwd_lanes[0:1], bsum[L - 1:L, :], bsum[0:1, :])
    m_sc[bi] = b_last + mx
    return dict(mm=mm, mx=mx,
                w_inter=jnp.exp(m_old - mm),
                floor=jnp.exp(-(bsum + mm)),
                decay=jnp.exp(m_old - mx),
                rel_t=rel.T)


def _ml_chain(refs, h_ref, c_sc, n_sc, gt, bi, d, hd):
    L = ML_CHUNK
    q_ref, k_ref, kt_ref, v_ref, _ = refs
    t_idx = lax.broadcasted_iota(jnp.int32, (L, L), 0)
    s_idx = lax.broadcasted_iota(jnp.int32, (L, L), 1)
    visible = (s_idx >= t_idx) if d == 1 else (s_idx <= t_idx)
    r = d * ML_HEADS + hd
    lo, hi = hd * ML_DK, (hd + 1) * ML_DK
    rel_row = gt["rel_t"][r:r + 1, :]
    w_col = gt["w_inter"][:, r:r + 1]
    n_old = n_sc[bi, r:r + 1, :]
    c_old = c_sc[bi, r]
    qh, kh, vh = q_ref[bi, :, lo:hi], k_ref[bi, :, lo:hi], v_ref[bi, :, lo:hi]
    kth = kt_ref[bi, lo:hi, :]

    a = jnp.exp(jnp.where(visible, rel_row - gt["mm"][:, r:r + 1], NEG_INF)) * _dot(qh, kth)
    num = _dot(a.astype(BF16), vh) + w_col * _dot(qh, c_old.astype(BF16))
    den = jnp.sum(a, axis=1, keepdims=True) + w_col * jnp.sum(qh.astype(F32) * n_old, axis=1, keepdims=True)
    h_ref[bi, :, lo:hi] = (num * (1.0 / jnp.maximum(jnp.abs(den), gt["floor"][:, r:r + 1]))).astype(BF16)

    w_row = jnp.exp(rel_row - gt["mx"][:, r:r + 1])
    dec = gt["decay"][:, r:r + 1]
    c_sc[bi, r] = dec * c_old + _dot((kth.astype(F32) * w_row).astype(BF16), vh)
    w_rows = jnp.broadcast_to(w_row, (BF16_ROWS, L)).astype(BF16)
    n_sc[bi, r:r + 1, :] = dec * n_old + _dot(w_rows, kh)[0:1, :]


def _ml_scan_kernel(*refs, with_init, emit_state):
    fwd_refs, bwd_refs, refs = refs[:5], refs[5:10], refs[10:]
    if with_init:
        (c0_ref, n0_ref, m0_ref), refs = refs[:3], refs[3:]
    (hf_ref, hb_ref), refs = refs[:2], refs[2:]
    if emit_state:
        (co_ref, no_ref, mo_ref), refs = refs[:3], refs[3:]
    c_sc, n_sc, m_sc = refs
    c = pl.program_id(1)

    @pl.when(c == 0)
    def _():
        if with_init:
            c_sc[...] = c0_ref[...]
            n_sc[...] = n0_ref[...]
            m_sc[...] = m0_ref[...]
        else:
            c_sc[...] = jnp.zeros_like(c_sc)
            n_sc[...] = jnp.zeros_like(n_sc)
            m_sc[...] = jnp.zeros_like(m_sc)

    seqs = range(c_sc.shape[0])
    gates = [_ml_gates(fwd_refs, bwd_refs, m_sc, bi) for bi in seqs]
    for d, (refs_d, h_ref) in enumerate(((fwd_refs, hf_ref), (bwd_refs, hb_ref))):
        for hd in range(ML_HEADS):
            for bi in seqs:
                _ml_chain(refs_d, h_ref, c_sc, n_sc, gates[bi], bi, d, hd)

    if emit_state:
        @pl.when(c == pl.num_programs(1) - 1)
        def _():
            co_ref[...] = c_sc[...]
            no_ref[...] = n_sc[...]
            mo_ref[...] = m_sc[...]


def _ml_scan(q, k, kt, v, gates, nb, seq, init=None, emit_state=False, group=ML_SEQ_GROUP):
    nc = seq // ML_CHUNK
    per_chunk = lambda a: a.reshape(nb, nc, *a.shape[-2:]) if a.ndim == 3 else a.reshape(nb, nc, ML_CHUNK, -1)
    fwd = lambda b, c: (b, c, 0, 0)
    bwd = lambda b, c: (b, nc - 1 - c, 0, 0)
    blk = lambda a, im: pl.BlockSpec((group, None) + a.shape[2:], im)
    st_c = pl.BlockSpec((group, N_STATE, ML_DK, ML_DK), lambda b, c: (b, 0, 0, 0))
    st_n = pl.BlockSpec((group, N_STATE, ML_DK), lambda b, c: (b, 0, 0))
    st_m = pl.BlockSpec((group, 1, LANE), lambda b, c: (b, 0, 0))
    arrays = [per_chunk(a) for a in (q, k, kt, v, gates)]
    in_specs = [blk(a, fwd) for a in arrays] + [blk(a, bwd) for a in arrays]
    args = arrays * 2
    if init is not None:
        in_specs += [st_c, st_n, st_m]
        args += list(init)
    h_shape = jax.ShapeDtypeStruct((nb, nc, ML_CHUNK, D), BF16)
    out_specs = [blk(h_shape, fwd), blk(h_shape, bwd)]
    out_shape = [h_shape] * 2
    if emit_state:
        out_specs += [st_c, st_n, st_m]
        out_shape += [jax.ShapeDtypeStruct((nb, N_STATE, ML_DK, ML_DK), F32),
                      jax.ShapeDtypeStruct((nb, N_STATE, ML_DK), F32),
                      jax.ShapeDtypeStruct((nb, 1, LANE), F32)]
    outs = pl.pallas_call(
        functools.partial(_ml_scan_kernel, with_init=init is not None, emit_state=emit_state),
        grid=(nb // group, nc), in_specs=in_specs, out_specs=out_specs, out_shape=out_shape,
        scratch_shapes=[pltpu.VMEM((group, N_STATE, ML_DK, ML_DK), F32),
                        pltpu.VMEM((group, N_STATE, ML_DK), F32), pltpu.VMEM((group, 1, LANE), F32)],
        compiler_params=_params(2, last_arbitrary=True), name="mlstm_scan",
    )(*args)
    return [outs[0].reshape(nb * seq, D), outs[1].reshape(nb * seq, D)] + list(outs[2:])


def _ml_post_kernel(x_ref, hf_ref, hb_ref, og_ref, mods_ref, hg_ref, w_ref, o_ref):
    for rows in _row_groups(x_ref.shape[0]):
        hsum = hf_ref[rows, :].astype(F32) + hb_ref[rows, :].astype(F32)
        hn = jnp.concatenate(
            [_rms(hsum[:, hd * ML_DK:(hd + 1) * ML_DK]) for hd in range(ML_HEADS)], axis=1)
        y = _dot((og_ref[rows, :].astype(F32) * (hn * hg_ref[...])).astype(BF16), w_ref[...])
        o_ref[rows, :] = x_ref[rows, :] + _gate(mods_ref, 1) * y


def _ml_post(x, hf, hb, og, mods, head_g, w_out, layer, off, n, tm=TOK_TILE):
    tok = pl.BlockSpec((tm, D), lambda i: (i, 0))
    return pl.pallas_call(
        _ml_post_kernel,
        grid=(n,),
        in_specs=[_x_spec(tm, off), tok, tok, tok, _mods_spec(layer, tm, off),
                  _resident((1, D)), _resident((D, D))],
        out_specs=_x_spec(tm, off),
        out_shape=jax.ShapeDtypeStruct(x.shape, F32),
        input_output_aliases={0: 0}, compiler_params=_params(1), name="mlstm_post",
    )(x, hf, hb, og, mods, head_g.reshape(1, D), w_out)


def _fn_chan_kernel(x_ref, mods_ref, g_ref, cc_ref, sc_ref, a_ref, b_ref):
    hb = _mod_norm(x_ref[...], g_ref, mods_ref, 1).astype(BF16)
    for gi in range(FOURIER_GROUPS):
        sl = slice(gi * FG, (gi + 1) * FG)
        a_ref[:, sl] = _dot(hb[:, sl], cc_ref[...]).astype(BF16)
        b_ref[:, sl] = _dot(hb[:, sl], sc_ref[...]).astype(BF16)


def _fn_chan(x, mods, g, cc, sc, layer, off, n, tm=TOK_TILE):
    tok = pl.BlockSpec((tm, D), lambda i: (i, 0))
    return pl.pallas_call(
        _fn_chan_kernel,
        grid=(n,),
        in_specs=[_x_spec(tm, off), _mods_spec(layer, tm, off), _resident((1, D)),
                  _resident((FG, FG)), _resident((FG, FG))],
        out_specs=[tok, tok],
        out_shape=[jax.ShapeDtypeStruct((n * tm, D), BF16)] * 2,
        compiler_params=_params(1), name="fourier_chan",
    )(x, mods, g.reshape(1, D), cc, sc)


def _fn_seq_kernel(x_ref, a_ref, b_ref, cs_ref, ss_ref, mods_ref, w_ref, bias_ref, o_ref, *, scale):
    f = (_dot(cs_ref[...], a_ref[...]) - _dot(ss_ref[...], b_ref[...])) * scale
    y = _dot(f.astype(BF16), w_ref[...]) + bias_ref[...]
    o_ref[...] = x_ref[...] + _gate(mods_ref, 1) * y


def _fn_seq(x, a, b, cs, ss, mods, w_out, b_out, layer, off_tok, nb, seq, tr):
    nt = seq // tr
    off = off_tok // tr
    xs = pl.BlockSpec((tr, D), lambda bi, t: (off + bi * nt + t, 0))
    ab = pl.BlockSpec((seq, D), lambda bi, t: (bi, 0))
    tbl = pl.BlockSpec((tr, seq), lambda bi, t: (t, 0))
    mods_spec = pl.BlockSpec(
        (None, 9, D), lambda bi, t: (layer * N_COND + _tile_cond(off + bi * nt + t, tr), 0, 0))
    return pl.pallas_call(
        functools.partial(_fn_seq_kernel, scale=float((seq * FG) ** -0.5)),
        grid=(nb, nt),
        in_specs=[xs, ab, ab, tbl, tbl, mods_spec, _resident((D, D)), _resident((1, D))],
        out_specs=xs,
        out_shape=jax.ShapeDtypeStruct(x.shape, F32),
        input_output_aliases={0: 0}, compiler_params=_params(2), name="fourier_seq",
    )(x, a, b, cs, ss, mods, w_out, b_out.reshape(1, D))


def _fn_fused_kernel(x_ref, mods_ref, g_ref, cc_ref, sc_ref, cs_ref, ss_ref, w_ref, bias_ref, o_ref,
                     a_sc, b_sc, *, seq, scale):
    hb = _mod_norm(x_ref[...], g_ref, mods_ref, 1).astype(BF16)
    for gi in range(FOURIER_GROUPS):
        sl = slice(gi * FG, (gi + 1) * FG)
        a_sc[:, sl] = _dot(hb[:, sl], cc_ref[...]).astype(BF16)
        b_sc[:, sl] = _dot(hb[:, sl], sc_ref[...]).astype(BF16)
    for r0 in range(0, x_ref.shape[0], seq):
        rows = slice(r0, r0 + seq)
        f = (_dot(cs_ref[...], a_sc[rows, :]) - _dot(ss_ref[...], b_sc[rows, :])) * scale
        y = _dot(f.astype(BF16), w_ref[...]) + bias_ref[...]
        o_ref[rows, :] = x_ref[rows, :] + _gate(mods_ref, 1) * y


def _fn_fused(x, mods, g, cc, sc, cs, ss, w_out, b_out, layer, off, n, seq, tm=TOK_TILE):
    return pl.pallas_call(
        functools.partial(_fn_fused_kernel, seq=seq, scale=float((seq * FG) ** -0.5)),
        grid=(n,),
        in_specs=[_x_spec(tm, off), _mods_spec(layer, tm, off), _resident((1, D)),
                  _resident((FG, FG)), _resident((FG, FG)), _resident((seq, seq)), _resident((seq, seq)),
                  _resident((D, D)), _resident((1, D))],
        out_specs=_x_spec(tm, off),
        out_shape=jax.ShapeDtypeStruct(x.shape, F32),
        scratch_shapes=[pltpu.VMEM((tm, D), BF16), pltpu.VMEM((tm, D), BF16)],
        input_output_aliases={0: 0}, compiler_params=_params(1), name="fourier_fused",
    )(x, mods, g.reshape(1, D), cc, sc, cs, ss, w_out, b_out.reshape(1, D))


def _dft_tables(n):
    ang = 2.0 * np.pi * ((np.arange(n)[:, None] * np.arange(n)[None, :]) % n) / n
    return (jnp.asarray(np.cos(ang), F32).astype(BF16), jnp.asarray(np.sin(ang), F32).astype(BF16))


def _gm_kernel(x_ref, mods_ref, g_ref, win_ref, bin_ref, vg_ref, ws_ref, bs_ref, wout_ref, o_ref, sv_sc):
    gw = GM_W // GM_GROUPS
    groups = _row_groups(x_ref.shape[0], 2 * GM_CHUNK)
    zs = [_dot(_mod_norm(x_ref[rows, :], g_ref, mods_ref, 1).astype(BF16), win_ref[...]) + bin_ref[...]
          for rows in groups]
    us = []
    for rows, z in zip(groups, zs):
        z = z * (0.5 * (1.0 + jnp.tanh(np.sqrt(2.0 / np.pi) * (z + 0.044715 * (z * z * z)))))
        us.append(z[:, :GM_W])
        v = (_rms(z[:, GM_W:]) * vg_ref[...]).astype(BF16)
        for ch in range(2):
            crows = slice(ch * GM_CHUNK, (ch + 1) * GM_CHUNK)
            for gi in range(GM_GROUPS):
                cols = slice(gi * gw, (gi + 1) * gw)
                sv_sc[rows.start + ch * GM_CHUNK:rows.start + (ch + 1) * GM_CHUNK, cols] = (
                    _dot(ws_ref[gi], v[crows, cols]) + bs_ref[:, gi:gi + 1])
    for rows, u in zip(groups, us):
        y = _dot((u * sv_sc[rows, :]).astype(BF16), wout_ref[...])
        o_ref[rows, :] = x_ref[rows, :] + _gate(mods_ref, 1) * y


def _gmlp(x, mods, g, w_in, b_in, v_g, w_s, b_s_t, w_out, layer, tm=TOK_TILE):
    return pl.pallas_call(
        _gm_kernel,
        grid=(N_TOK // tm,),
        in_specs=[_x_spec(tm, 0), _mods_spec(layer, tm, 0), _resident((1, D)),
                  _resident((D, 2 * GM_W)), _resident((1, 2 * GM_W)), _resident((1, GM_W)),
                  _resident((GM_GROUPS, GM_CHUNK, GM_CHUNK)), _resident((GM_CHUNK, GM_GROUPS)),
                  _resident((GM_W, D))],
        out_specs=_x_spec(tm, 0),
        out_shape=jax.ShapeDtypeStruct(x.shape, F32),
        scratch_shapes=[pltpu.VMEM((tm, GM_W), F32)],
        input_output_aliases={0: 0}, compiler_params=_params(1), name="gmlp",
    )(x, mods, g.reshape(1, D), w_in, b_in.reshape(1, 2 * GM_W), v_g.reshape(1, GM_W), w_s, b_s_t, w_out)


def _na_pre_kernel(x_ref, mods_ref, g_ref, w_ref, q_ref, k_ref, v_ref, *f32_refs):
    for rows in _row_groups(x_ref.shape[0], SEQ):
        hb = _mod_norm(x_ref[rows, :], g_ref, mods_ref, 1).astype(BF16)
        qkv = _dot(hb, w_ref[...])
        q_ref[rows, :] = (qkv[:, :D] * (NA_HD ** -0.5)).astype(BF16)
        k, v = qkv[:, D:2 * D], qkv[:, 2 * D:]
        k_ref[rows, :] = k.astype(BF16)
        v_ref[rows, :] = v.astype(BF16)
        if f32_refs:
            f32_refs[0][rows.start // SEQ] = k.T
            f32_refs[1][rows.start // SEQ] = v.T


def _na_pre(x, mods, g, w_qkv, layer, off, n, emit_f32, tm=TOK_TILE):
    tok = pl.BlockSpec((tm, D), lambda i: (i, 0))
    out_specs = [tok, tok, tok]
    out_shape = [jax.ShapeDtypeStruct((n * tm, D), BF16)] * 3
    if emit_f32:
        assert tm % SEQ == 0
        out_specs += [pl.BlockSpec((tm // SEQ, D, SEQ), lambda i: (i, 0, 0))] * 2
        out_shape += [jax.ShapeDtypeStruct((n * tm // SEQ, D, SEQ), F32)] * 2
    return pl.pallas_call(
        _na_pre_kernel,
        grid=(n,),
        in_specs=[_x_spec(tm, off), _mods_spec(layer, tm, off), _resident((1, D)), _resident((D, 3 * D))],
        out_specs=out_specs, out_shape=out_shape,
        compiler_params=_params(1), name="na_qkv",
    )(x, mods, g.reshape(1, D), w_qkv)


def _half_mask(shape, e):
    lane = lax.broadcasted_iota(jnp.int32, shape, len(shape) - 1)
    return (lane < NA_HD) if e == 0 else (lane >= NA_HD)


def _ctx_attn_kernel(x_ref, q_ref, k_ref, v_ref, mods_ref, w_ref, o_ref, att_sc):
    for rows in _row_groups(x_ref.shape[0], SEQ):
        for j in range(NA_PAIRS):
            sl = slice(j * LANE, (j + 1) * LANE)
            s = _dot_nt(_pair_rows(q_ref[rows, sl]), k_ref[rows, sl])
            p = jnp.exp(s - jnp.max(s, axis=1, keepdims=True))
            o = _dot(p.astype(BF16), v_ref[rows, sl]) / jnp.sum(p, axis=1, keepdims=True)
            att_sc[rows, sl] = jnp.where(_half_mask((SEQ, LANE), 0), o[:SEQ], o[SEQ:]).astype(BF16)
    o_ref[...] = x_ref[...] + _gate(mods_ref, 1) * _dot(att_sc[...], w_ref[...])


def _ctx_attn(x, q, k, v, mods, w_out, layer, tm=TOK_TILE):
    tok = pl.BlockSpec((tm, D), lambda b: (b, 0))
    return pl.pallas_call(
        _ctx_attn_kernel,
        grid=(NP_TOK // tm,),
        in_specs=[tok, tok, tok, tok, _mods_spec(layer, tm, 0), _resident((D, D))],
        out_specs=tok,
        out_shape=jax.ShapeDtypeStruct(x.shape, F32),
        scratch_shapes=[pltpu.VMEM((tm, D), BF16)],
        input_output_aliases={0: 0}, compiler_params=_params(1), name="ctx_attn",
    )(x, q, k, v, mods, w_out)


def _na_bias_kernel(rpb_ref, o_ref, pair_sc):
    h = pl.program_id(0)
    shape = (GRID_W, LANE)
    q = lax.broadcasted_iota(jnp.int32, shape, 0)
    lane = lax.broadcasted_iota(jnp.int32, shape, 1)
    x = lane & (GRID_W - 1)
    first = lane < GRID_W
    dc = x - q + (NA_KW - 1)
    q_start = jnp.clip(q - NA_KW // 2, 0, GRID_W - NA_KW)
    in_window = (x >= q_start) & (x < q_start + NA_KW)
    n_dc = 2 * NA_KW - 1
    n_pairs = 2 * NA_KH - 2
    for dr0 in range(0, n_pairs, 5):
        drs = range(dr0, min(dr0 + 5, n_pairs))
        accs = [jnp.zeros(shape, F32) for _ in drs]
        for j in range(n_dc):
            hit = dc == j
            for a, dr in enumerate(drs):
                val = jnp.where(first, rpb_ref[h, dr * n_dc + j], rpb_ref[h, (dr + 1) * n_dc + j])
                accs[a] = jnp.where(hit, val, accs[a])
        for a, dr in enumerate(drs):
            pair_sc[dr] = jnp.where(in_window, accs[a], NEG_INF)
    for o in range(NA_KH):
        for t in range(NA_KH // 2):
            o_ref[o, :, t * LANE:(t + 1) * LANE] = pair_sc[NA_KH - 1 - o + 2 * t]


def _na_bias_tables(rpb):
    n_rel = (2 * NA_KH - 1) * (2 * NA_KW - 1)
    t = pl.pallas_call(
        _na_bias_kernel,
        grid=(NA_HEADS,),
        in_specs=[pl.BlockSpec(memory_space=pltpu.SMEM)],
        out_specs=pl.BlockSpec((NA_KH, None, GRID_W, NA_WIN), lambda h: (0, h, 0, 0)),
        out_shape=jax.ShapeDtypeStruct((NA_KH, NA_HEADS, GRID_W, NA_WIN), F32),
        scratch_shapes=[pltpu.VMEM((2 * NA_KH - 2, GRID_W, LANE), F32)],
        compiler_params=_params(1), name="na_bias",
    )(rpb.reshape(NA_HEADS, n_rel))
    return t.reshape(NA_KH, NA_HEADS * GRID_W, NA_WIN)


def _pair_rows(qp):
    zero = jnp.zeros_like(qp)
    return jnp.concatenate([jnp.where(_half_mask(qp.shape, 0), qp, zero),
                            jnp.where(_half_mask(qp.shape, 1), qp, zero)], axis=0)


def _na_lat_kernel(q_ref, k_ref, v_ref, kc_ref, vc_ref, *rest):
    bias_refs, (o_ref, p_sc) = rest[:NA_ROWS_PER_STEP], rest[NA_ROWS_PER_STEP:]
    for i, bias_ref in enumerate(bias_refs):
        rows = slice(i * GRID_W, (i + 1) * GRID_W)
        _na_lat_row(q_ref, k_ref, v_ref, kc_ref, vc_ref, bias_ref, o_ref, p_sc,
                    pl.program_id(1) * NA_ROWS_PER_STEP + i, rows)


def _na_lat_row(q_ref, k_ref, v_ref, kc_ref, vc_ref, bias_ref, o_ref, p_sc, r, rows):
    start = pl.multiple_of(jnp.clip(r - NA_KH // 2, 0, NA_ROWS - NA_KH) * GRID_W, GRID_W)
    inv = []
    for j in range(NA_PAIRS):
        sl = slice(j * LANE, (j + 1) * LANE)
        qb = _pair_rows(q_ref[rows, sl])
        s_w = _dot_nt(qb, k_ref[pl.ds(start, NA_WIN), sl]) + bias_ref[sl, :]
        s_c = _dot_nt(qb, kc_ref[:, sl])
        m = jnp.maximum(jnp.max(s_w, axis=1, keepdims=True), jnp.max(s_c, axis=1, keepdims=True))
        p_w = jnp.exp(s_w - m)
        p_c = jnp.exp(s_c - m)
        inv.append(1.0 / (jnp.sum(p_w, axis=1, keepdims=True) + jnp.sum(p_c, axis=1, keepdims=True)))
        p_sc[sl, :NA_WIN] = p_w.astype(BF16)
        p_sc[sl, NA_WIN:] = p_c.astype(BF16)
    for j in range(NA_PAIRS):
        sl = slice(j * LANE, (j + 1) * LANE)
        o = (_dot(p_sc[sl, :NA_WIN], v_ref[pl.ds(start, NA_WIN), sl])
             + _dot(p_sc[sl, NA_WIN:], vc_ref[:, sl])) * inv[j]
        o_ref[rows, sl] = jnp.where(_half_mask((GRID_W, LANE), 0), o[:GRID_W], o[GRID_W:]).astype(BF16)


def _na_latent(q, k, v, kc, vc, bias):
    row_class = lambda r: r - jnp.clip(r - NA_KH // 2, 0, NA_ROWS - NA_KH)
    seq_kv = pl.BlockSpec((None, DEC_SEQ, D), lambda b, t: (b, 0, 0))
    ctx_kv = pl.BlockSpec((None, PAST_LEN, D), lambda b, t: (b, 0, 0))
    n_rows = NA_HEADS * GRID_W
    steps = NA_ROWS // NA_ROWS_PER_STEP
    tok = pl.BlockSpec((NA_ROWS_PER_STEP * GRID_W, D), lambda b, t: (b * steps + t, 0))
    bias_spec = lambda i: pl.BlockSpec(
        (None, n_rows, NA_WIN), lambda b, t: (row_class(t * NA_ROWS_PER_STEP + i), 0, 0))
    return pl.pallas_call(
        _na_lat_kernel,
        grid=(DEC_BATCH, steps),
        in_specs=[tok, seq_kv, seq_kv, ctx_kv, ctx_kv] + [bias_spec(i) for i in range(NA_ROWS_PER_STEP)],
        out_specs=tok,
        out_shape=jax.ShapeDtypeStruct((NS_TOK, D), BF16),
        scratch_shapes=[pltpu.VMEM((n_rows, NA_WIN + PAST_LEN), BF16)],
        compiler_params=_params(2), name="na_latent",
    )(q, k.reshape(DEC_BATCH, DEC_SEQ, D), v.reshape(DEC_BATCH, DEC_SEQ, D), kc, vc,
      *([bias] * NA_ROWS_PER_STEP))


def kernel(x_prompt, x_sample, state_mlstm_C, state_mlstm_n, state_mlstm_m, cache_na_k, cache_na_v, c, c_ctx, w_ada, b_ada, norm_g, final_g, ffn_w1, ffn_w3, ffn_w2, ml_w_qkv, ml_w_if, ml_b_if, ml_w_og, ml_head_g, ml_w_out, fn_w_out, fn_b_out, gm_w_in, gm_b_in, gm_v_g, gm_w_s, gm_b_s, gm_w_out, na_w_qkv, na_w_out, na_rpb):
    n_p, n_s = NP_TOK // TOK_TILE, NS_TOK // TOK_TILE
    x = (x_prompt.reshape(NP_TOK, D), x_sample.reshape(NS_TOK, D))

    cond = jnp.zeros((N_COND, D), F32).at[0].set(c_ctx).at[1:1 + DEC_BATCH].set(c)
    mods = _adaln(cond, w_ada, b_ada).reshape(DEPTH * N_COND, 9, D)

    ffn_f32 = (ffn_w1, ffn_w3, ffn_w2)
    wb = tuple(w[0, 0].astype(BF16) for w in ffn_f32)
    outs = {}
    for l in range(DEPTH):
        kind, j = l % 4, l // 4
        x, wb = _ffn(x, mods, norm_g[l, 0], wb, l, 0, 0, N_TOK // FFN_TILE, next_f32=(*ffn_f32, l, 1))
        g = norm_g[l, 1]
        if kind == 0:
            w_dir = jnp.transpose(ml_w_if[j], (1, 0, 2))
            lane_pad = ((0, 0), (0, LANE - N_STATE))
            wif = jnp.concatenate(
                [jnp.pad(w_dir[:, :, :ML_HEADS].reshape(D, N_STATE), lane_pad),
                 jnp.pad(w_dir[:, :, ML_HEADS:].reshape(D, N_STATE), lane_pad)], axis=1)
            bif = jnp.concatenate(
                [jnp.pad(ml_b_if[j][:, :ML_HEADS].reshape(1, N_STATE), lane_pad),
                 jnp.pad(ml_b_if[j][:, ML_HEADS:].reshape(1, N_STATE), lane_pad)], axis=1)
            wqkv, wog, wout = ml_w_qkv[j].astype(BF16), ml_w_og[j].astype(BF16), ml_w_out[j].astype(BF16)
            for off, n, nb, seq in ((0, n_p, BATCH, SEQ), (n_p, n_s, DEC_BATCH, DEC_SEQ)):
                q, k, kt, v, og, gates = _ml_pre(x, mods, g, wqkv, wog, wif.astype(BF16), bif, l, off, n)
                if off == 0:
                    hf, hb, c_new, n_new, m_new = _ml_scan(q, k, kt, v, gates, nb, seq, emit_state=True,
                                                           group=2 * ML_SEQ_GROUP)
                    outs["C"] = c_new.reshape(BATCH, 1, 2, ML_HEADS, ML_DK, ML_DK)
                    outs["n"] = n_new.reshape(BATCH, 1, 2, ML_HEADS, ML_DK)
                    outs["m"] = m_new[:, 0, :N_STATE].reshape(BATCH, 1, 2, ML_HEADS)
                else:
                    init = (state_mlstm_C[:, j].reshape(DEC_BATCH, N_STATE, ML_DK, ML_DK),
                            state_mlstm_n[:, j].reshape(DEC_BATCH, N_STATE, ML_DK),
                            jnp.pad(state_mlstm_m[:, j].reshape(DEC_BATCH, 1, N_STATE),
                                    ((0, 0), (0, 0), (0, LANE - N_STATE))))
                    hf, hb = _ml_scan(q, k, kt, v, gates, nb, seq, init=init)
                x = _ml_post(x, hf, hb, og, mods, ml_head_g[j], wout, l, off, n)
        elif kind == 1:
            cc, sc = _dft_tables(FG)
            wout = fn_w_out[j].astype(BF16)
            cs, ss = _dft_tables(SEQ)
            x = _fn_fused(x, mods, g, cc, sc, cs, ss, wout, fn_b_out[j], l, 0, n_p, SEQ)
            a, b = _fn_chan(x, mods, g, cc, sc, l, n_p, n_s)
            cs, ss = _dft_tables(DEC_SEQ)
            x = _fn_seq(x, a, b, cs, ss, mods, wout, fn_b_out[j], l, NP_TOK, DEC_BATCH, DEC_SEQ, TOK_TILE)
        elif kind == 2:
            x = _gmlp(x, mods, g, gm_w_in[j].astype(BF16), gm_b_in[j], gm_v_g[j],
                      gm_w_s[j].astype(BF16), gm_b_s[j].T, gm_w_out[j].astype(BF16), l)
        else:
            wqkv, wout = na_w_qkv[j].astype(BF16), na_w_out[j].astype(BF16)
            q, k, v, k_heads, v_heads = _na_pre(x, mods, g, wqkv, l, 0, n_p, True)
            per_head = lambda t: jnp.transpose(t.reshape(BATCH, 1, NA_HEADS, NA_HD, SEQ), (0, 1, 4, 2, 3))
            outs["k"], outs["v"] = per_head(k_heads), per_head(v_heads)
            x = _ctx_attn(x, q, k, v, mods, wout, l)
            q, k, v = _na_pre(x, mods, g, wqkv, l, n_p, n_s, False)
            att = _na_latent(q, k, v,
                             cache_na_k[:, j].reshape(DEC_BATCH, PAST_LEN, D).astype(BF16),
                             cache_na_v[:, j].reshape(DEC_BATCH, PAST_LEN, D).astype(BF16),
                             _na_bias_tables(na_rpb[j]))
            x = _proj_residual(x, att, mods, wout, l, n_p, n_s)
        if l < DEPTH - 1:
            x, wb = _ffn(x, mods, norm_g[l, 2], wb, l, 1, 0, N_TOK // FFN_TILE, next_f32=(*ffn_f32, l + 1, 0))
        else:
            y_p = _ffn(x, mods, norm_g[l, 2], wb, l, 1, 0, NP_TOK // FFN_TILE, final_g=final_g)
            y_s = _ffn(x, mods, norm_g[l, 2], wb, l, 1, NP_TOK // FFN_TILE, NS_TOK // FFN_TILE, final_g=final_g)
    return (y_p.reshape(BATCH, SEQ, D), y_s.reshape(DEC_BATCH, DEC_SEQ, D),
            outs["C"], outs["n"], outs["m"], outs["k"], outs["v"])
```
